```python
import math
import jax, jax.numpy as jnp
from jax import lax
import numpy as np

D_MODEL = 2048
BATCH = 2
SEQ = 16384
DEPTH = 2

CHUNK = 64
Q_BLOCK = 128
N_MIXERS = 2
N_LAYERS_A = (DEPTH + 1) // 2
N_LAYERS_B = DEPTH // 2

MLA_HEADS = 16
MLA_NOPE = 128
MLA_ROPE = 64
MLA_V = 128
MLA_Q_RANK = 448
MLA_KV_RANK = 512
MLA_IN = MLA_Q_RANK + MLA_KV_RANK + MLA_ROPE
ROPE_THETA = 10000.0

ML_HEADS = 4
ML_QK = 256
ML_V = 512
ML_IN = 2 * ML_HEADS * ML_QK + 2 * ML_HEADS * ML_V + 2 * ML_HEADS
GATE_SOFTCAP = 15.0

N_EXPERTS = 32
TOP_K = 4
D_EXPERT = 1024
SWIGLU_LIMIT = 7.0
SWIGLU_ALPHA = 1.702
MOE_BLOCK = 128

ALPHA = (2 * DEPTH) ** 0.25
BETA = (8 * DEPTH) ** -0.25
EPS = 1e-6
NEG_INF = -1e30

kernel_name = 'hybrid_mla_mlstm_moe_deepnorm'


def rms_norm(x, g):
    xf = x.astype(jnp.float32)
    y = xf * lax.rsqrt(jnp.mean(xf * xf, -1, keepdims=True) + EPS)
    return (y * g).astype(x.dtype)


def layer_norm(x, g, b):
    xf = x.astype(jnp.float32)
    mu = jnp.mean(xf, -1, keepdims=True)
    var = jnp.mean(jnp.square(xf - mu), -1, keepdims=True)
    return ((xf - mu) * lax.rsqrt(var + EPS) * g + b).astype(x.dtype)


def rope_tables(positions):
    inv = ROPE_THETA ** (-jnp.arange(0, MLA_ROPE, 2, dtype=jnp.float32) / MLA_ROPE)
    ang = positions.astype(jnp.float32)[..., None] * inv
    return jnp.cos(ang), jnp.sin(ang)


def apply_rope(x, cos, sin):
    half = x.shape[-1] // 2
    x1 = x[..., :half].astype(jnp.float32)
    x2 = x[..., half:].astype(jnp.float32)
    return jnp.concatenate([x1 * cos - x2 * sin, x2 * cos + x1 * sin], -1).astype(x.dtype)


def mla_mixer(h, cos, sin, w_in, q_norm, w_uq, kv_norm, w_ukv, w_out):
    B, S, _ = h.shape
    H = MLA_HEADS
    proj = h @ w_in
    c_q, c_kv, k_r = jnp.split(proj, [MLA_Q_RANK, MLA_Q_RANK + MLA_KV_RANK], -1)
    c_q = rms_norm(c_q, q_norm)
    c_kv = rms_norm(c_kv, kv_norm)
    q = (c_q @ w_uq).reshape(B, S, H, MLA_NOPE + MLA_ROPE)
    q_rope = apply_rope(q[..., MLA_NOPE:], cos[:, :, None, :], sin[:, :, None, :])
    q = jnp.concatenate([q[..., :MLA_NOPE], q_rope], -1)
    k_rope = apply_rope(k_r, cos, sin)
    kv = (c_kv @ w_ukv).reshape(B, S, H, MLA_NOPE + MLA_V)
    k = jnp.concatenate([kv[..., :MLA_NOPE],
                         jnp.broadcast_to(k_rope[:, :, None, :], (B, S, H, MLA_ROPE))], -1)
    v = kv[..., MLA_NOPE:]
    scale = (MLA_NOPE + MLA_ROPE) ** -0.5
    nb = S // Q_BLOCK
    qb = q.reshape(B, nb, Q_BLOCK, H, MLA_NOPE + MLA_ROPE).transpose(1, 0, 3, 2, 4)
    key_chunk = jnp.arange(S) // CHUNK

    def attend(args):
        i, q_blk = args
        q_chunk = (i * Q_BLOCK + jnp.arange(Q_BLOCK)) // CHUNK
        s = jnp.einsum('bhqd,bshd->bhqs', q_blk, k).astype(jnp.float32) * scale
        s = jnp.where(key_chunk[None, :] <= q_chunk[:, None], s, NEG_INF)
        p = jax.nn.softmax(s, -1).astype(v.dtype)
        return jnp.einsum('bhqs,bshv->bqhv', p, v)

    o = lax.map(attend, (jnp.arange(nb), qb))
    o = o.transpose(1, 0, 2, 3, 4).reshape(B, S, H * MLA_V)
    return o @ w_out


def mlstm_mixer(h, w_in, b_gates, head_norm, w_out):
    B, S, _ = h.shape
    H, L = ML_HEADS, CHUNK
    nc = S // L
    proj = h @ w_in
    sq = H * ML_QK
    sv = H * ML_V
    q, k, v, og, g = jnp.split(proj, [sq, 2 * sq, 2 * sq + sv, 2 * sq + 2 * sv], -1)
    g = GATE_SOFTCAP * jnp.tanh((g + b_gates).astype(jnp.float32) / GATE_SOFTCAP)
    log_i = g[..., :H]
    log_f = jax.nn.log_sigmoid(g[..., H:])

    def chunks(t, d):
        return t.reshape(B, nc, L, H, d).transpose(1, 0, 3, 2, 4)

    qc = chunks(q, ML_QK)
    kc = chunks(k, ML_QK) * (ML_QK ** -0.5)
    vc = chunks(v, ML_V)
    lic = log_i.reshape(B, nc, L, H).transpose(1, 0, 3, 2)
    lfc = log_f.reshape(B, nc, L, H).transpose(1, 0, 3, 2)
    causal = jnp.tril(jnp.ones((L, L), dtype=bool))

    def step(carry, inp):
        C, n, m = carry
        qt, kt, vt, li, lf = inp
        b = jnp.cumsum(lf, -1)
        dm = b[..., :, None] - b[..., None, :] + li[..., None, :]
        dm = jnp.where(causal, dm, NEG_INF)
        inter = b + m[..., None]
        m_t = jnp.maximum(inter, jnp.max(dm, -1))
        w_intra = jnp.exp(dm - m_t[..., None])
        w_inter = jnp.exp(inter - m_t)
        qk = jnp.einsum('bhtd,bhsd->bhts', qt, kt) * w_intra
        num = (w_inter[..., None] * jnp.einsum('bhtd,bhdv->bhtv', qt, C)
               + jnp.einsum('bhts,bhsv->bhtv', qk, vt))
        den = w_inter * jnp.einsum('bhtd,bhd->bht', qt, n) + jnp.sum(qk, -1)
        h_t = num / jnp.maximum(jnp.abs(den), jnp.exp(-m_t))[..., None]
        b_end = b[..., -1]
        d_end = b_end[..., None] - b + li
        m_new = jnp.maximum(b_end + m, jnp.max(d_end, -1))
        decay = jnp.exp(b_end + m - m_new)
        w_s = jnp.exp(d_end - m_new[..., None])
        C_new = decay[..., None, None] * C + jnp.einsum('bhs,bhsd,bhsv->bhdv', w_s, kt, vt)
        n_new = decay[..., None] * n + jnp.einsum('bhs,bhsd->bhd', w_s, kt)
        return (C_new, n_new, m_new), h_t

    init = (jnp.zeros((B, H, ML_QK, ML_V), jnp.float32),
            jnp.zeros((B, H, ML_QK), jnp.float32),
            jnp.zeros((B, H), jnp.float32))
    _, hs = lax.scan(step, init, (qc, kc, vc, lic, lfc))
    hs = hs.transpose(1, 0, 3, 2, 4).reshape(B, S, H, ML_V)
    hn = hs * lax.rsqrt(jnp.mean(hs * hs, -1, keepdims=True) + EPS) * head_norm.reshape(H, ML_V)
    hn = hn.reshape(B, S, H * ML_V).astype(h.dtype) * jax.nn.sigmoid(og)
    return hn @ w_out


def clamped_swiglu(gu):
    a, u = jnp.split(gu, 2, -1)
    a = jnp.minimum(a, SWIGLU_LIMIT)
    u = jnp.clip(u, -SWIGLU_LIMIT, SWIGLU_LIMIT)
    return (u + 1.0) * (a * jax.nn.sigmoid(SWIGLU_ALPHA * a))


def moe_ffn(h, w_router, b_router, w_gu, b_gu, w_down, b_down):
    B, S, D = h.shape
    N = B * S
    A = N * TOP_K
    xf = h.reshape(N, D)
    logits = (xf @ w_router + b_router).astype(jnp.float32)
    top_val, top_idx = lax.top_k(logits, TOP_K)
    gate = jax.nn.softmax(top_val, -1)
    flat_e = top_idx.reshape(A)
    flat_tok = jnp.repeat(jnp.arange(N, dtype=jnp.int32), TOP_K)
    flat_g = gate.reshape(A)
    order = jnp.argsort(flat_e)
    se, stok, sg = flat_e[order], flat_tok[order], flat_g[order]
    counts = jnp.zeros((N_EXPERTS,), jnp.int32).at[flat_e].add(1)
    starts = jnp.cumsum(counts) - counts
    pcounts = (counts + MOE_BLOCK - 1) // MOE_BLOCK * MOE_BLOCK
    pends = jnp.cumsum(pcounts)
    pstarts = pends - pcounts
    dest = pstarts[se] + (jnp.arange(A, dtype=jnp.int32) - starts[se])
    n_blocks = -(-(A + N_EXPERTS * (MOE_BLOCK - 1)) // MOE_BLOCK)
    P = n_blocks * MOE_BLOCK
    buf_tok = jnp.full((P,), N, jnp.int32).at[dest].set(stok)
    buf_g = jnp.zeros((P,), jnp.float32).at[dest].set(sg)
    block_e = jnp.minimum(jnp.searchsorted(pends, jnp.arange(n_blocks, dtype=jnp.int32) * MOE_BLOCK,
                                           side='right'), N_EXPERTS - 1)
    x_pad = jnp.concatenate([xf, jnp.zeros((1, D), xf.dtype)], 0)

    def body(y, inp):
        e, tok, g = inp
        xb = x_pad[tok]
        act = clamped_swiglu(xb @ w_gu[e] + b_gu[e])
        yb = (act @ w_down[e] + b_down[e]) * g[:, None].astype(act.dtype)
        return y.at[tok].add(yb.astype(y.dtype)), None

    y, _ = lax.scan(body, jnp.zeros((N + 1, D), xf.dtype),
                    (block_e, buf_tok.reshape(n_blocks, MOE_BLOCK), buf_g.reshape(n_blocks, MOE_BLOCK)))
    return y[:N].reshape(B, S, D)


def setup_inputs(seed: int = 0) -> dict:
    key = jax.random.key(seed)
    ks = jax.random.split(key, 24)
    f32 = jnp.float32

    def nrm(k, shape, scale):
        return jax.random.normal(k, shape, f32) * scale

    x = nrm(ks[0], (BATCH, SEQ, D_MODEL), 1.0)
    c = nrm(ks[1], (BATCH, D_MODEL), 1.0)
    offset = jax.random.randint(ks[2], (BATCH, 1), 0, 4096, dtype=jnp.int32)
    positions = offset + jnp.arange(SEQ, dtype=jnp.int32)[None, :]
    ada_w = nrm(ks[3], (DEPTH, D_MODEL, 6 * D_MODEL), 0.1 * D_MODEL ** -0.5)
    ada_b = nrm(ks[4], (DEPTH, 6 * D_MODEL), 0.01)
    ln_g = 1.0 + nrm(ks[5], (DEPTH, 2, D_MODEL), 0.02)
    ln_b = nrm(ks[6], (DEPTH, 2, D_MODEL), 0.02)
    mla_w_in = nrm(ks[7], (N_LAYERS_A, D_MODEL, MLA_IN), D_MODEL ** -0.5)
    mla_q_norm = 1.0 + nrm(ks[8], (N_LAYERS_A, MLA_Q_RANK), 0.02)
    mla_w_uq = nrm(ks[9], (N_LAYERS_A, MLA_Q_RANK, MLA_HEADS * (MLA_NOPE + MLA_ROPE)), MLA_Q_RANK ** -0.5)
    mla_kv_norm = 1.0 + nrm(ks[10], (N_LAYERS_A, MLA_KV_RANK), 0.02)
    mla_w_ukv = nrm(ks[11], (N_LAYERS_A, MLA_KV_RANK, MLA_HEADS * (MLA_NOPE + MLA_V)), MLA_KV_RANK ** -0.5)
    mla_w_out = nrm(ks[12], (N_LAYERS_A, MLA_HEADS * MLA_V, D_MODEL), BETA * (MLA_HEADS * MLA_V) ** -0.5)
    ml_w_in = nrm(ks[13], (N_LAYERS_B, D_MODEL, ML_IN), D_MODEL ** -0.5)
    i_bias = nrm(ks[14], (N_LAYERS_B, ML_HEADS), 0.5)
    f_bias = jnp.linspace(3.0, 6.0, ML_HEADS, dtype=f32) + nrm(ks[15], (N_LAYERS_B, ML_HEADS), 0.1)
    ml_b_gates = jnp.concatenate([i_bias, f_bias], -1)
    ml_head_norm = 1.0 + nrm(ks[16], (N_LAYERS_B, ML_HEADS * ML_V), 0.02)
    ml_w_out = nrm(ks[17], (N_LAYERS_B, ML_HEADS * ML_V, D_MODEL), BETA * (ML_HEADS * ML_V) ** -0.5)
    moe_w_router = nrm(ks[18], (DEPTH, D_MODEL, N_EXPERTS), D_MODEL ** -0.5)
    moe_b_router = nrm(ks[19], (DEPTH, N_EXPERTS), 0.01)
    moe_w_gu = nrm(ks[20], (DEPTH, N_EXPERTS, D_MODEL, 2 * D_EXPERT), D_MODEL ** -0.5)
    moe_b_gu = nrm(ks[21], (DEPTH, N_EXPERTS, 2 * D_EXPERT), 0.01)
    moe_w_down = nrm(ks[22], (DEPTH, N_EXPERTS, D_EXPERT, D_MODEL), BETA * D_EXPERT ** -0.5)
    moe_b_down = nrm(ks[23], (DEPTH, N_EXPERTS, D_MODEL), 0.01)
    return {'x': x, 'c': c, 'positions': positions, 'ada_w': ada_w, 'ada_b': ada_b,
            'ln_g': ln_g, 'ln_b': ln_b,
            'mla_w_in': mla_w_in, 'mla_q_norm': mla_q_norm, 'mla_w_uq': mla_w_uq,
            'mla_kv_norm': mla_kv_norm, 'mla_w_ukv': mla_w_ukv, 'mla_w_out': mla_w_out,
            'ml_w_in': ml_w_in, 'ml_b_gates': ml_b_gates, 'ml_head_norm': ml_head_norm, 'ml_w_out': ml_w_out,
            'moe_w_router': moe_w_router, 'moe_b_router': moe_b_router, 'moe_w_gu': moe_w_gu,
            'moe_b_gu': moe_b_gu, 'moe_w_down': moe_w_down, 'moe_b_down': moe_b_down}


def reference(x, c, positions, ada_w, ada_b, ln_g, ln_b,
              mla_w_in, mla_q_norm, mla_w_uq, mla_kv_norm, mla_w_ukv, mla_w_out,
              ml_w_in, ml_b_gates, ml_head_norm, ml_w_out,
              moe_w_router, moe_b_router, moe_w_gu, moe_b_gu, moe_w_down, moe_b_down):
    cos, sin = rope_tables(positions)
    c_act = jax.nn.silu(c)
    for i in range(DEPTH):
        j = i // N_MIXERS
        mods = (c_act @ ada_w[i] + ada_b[i])[:, None, :]
        sh1, sc1, g1, sh2, sc2, g2 = jnp.split(mods, 6, -1)
        h = x * (1.0 + sc1) + sh1
        if i % N_MIXERS == 0:
            y = mla_mixer(h, cos, sin, mla_w_in[j], mla_q_norm[j], mla_w_uq[j],
                          mla_kv_norm[j], mla_w_ukv[j], mla_w_out[j])
        else:
            y = mlstm_mixer(h, ml_w_in[j], ml_b_gates[j], ml_head_norm[j], ml_w_out[j])
        x = layer_norm(ALPHA * x + (1.0 + g1) * y, ln_g[i, 0], ln_b[i, 0])
        h = x * (1.0 + sc2) + sh2
        y = moe_ffn(h, moe_w_router[i], moe_b_router[i], moe_w_gu[i], moe_b_gu[i],
                    moe_w_down[i], moe_b_down[i])
        x = layer_norm(ALPHA * x + (1.0 + g2) * y, ln_g[i, 1], ln_b[i, 1])
    return x
```

```python
import functools
import math

import jax
import jax.numpy as jnp
from jax import lax
from jax.experimental import pallas as pl
from jax.experimental.pallas import tpu as pltpu

F32 = jnp.float32
BF16 = jnp.bfloat16
I32 = jnp.int32
HIGHEST = lax.Precision.HIGHEST

CHUNK = 64
MLA_HEADS = 16
MLA_NOPE = 128
MLA_ROPE = 64
MLA_V = 128
MLA_Q_RANK = 448
MLA_KV_RANK = 512
ROPE_THETA = 10000.0
ML_HEADS = 4
ML_QK = 256
ML_V = 512
GATE_SOFTCAP = 15.0
N_EXPERTS = 32
TOP_K = 4
D_EXPERT = 1024
SWIGLU_LIMIT = 7.0
SWIGLU_ALPHA = 1.702
DEPTH = 2
ALPHA = (2 * DEPTH) ** 0.25
EPS = 1e-6
NEG_BIG = -1e30

LANES = 128
VMEM_LIMIT = 56 * 1024 * 1024

Q_RANK_PAD = 512
ATT_BLOCK = 512
ML_CHUNK = 256
MOE_SLOT_BLOCK = 512
ROUTE_TILE = 1024
ROW_TILE = 256


def _cparams(sem, vmem=None):
    return pltpu.CompilerParams(dimension_semantics=sem, vmem_limit_bytes=vmem)


def _const_spec(shape):
    nd = len(shape)
    return pl.BlockSpec(shape, lambda *_: (0,) * nd, pipeline_mode=pl.Buffered(1))


def _split_bf16(x):
    hi = x.astype(BF16)
    lo = (x - hi.astype(F32)).astype(BF16)
    return hi, lo


def _dot(a, b):
    return jnp.dot(a, b, preferred_element_type=F32)


def _dot_nt(a, b):
    return lax.dot_general(a, b, (((1,), (1,)), ((), ())), preferred_element_type=F32)


def _dot3(x, w_hi, w_lo):
    x_hi, x_lo = _split_bf16(x)
    return _dot(x_hi, w_hi) + (_dot(x_lo, w_hi) + _dot(x_hi, w_lo))


def _layer_norm(z, g, b):
    mu = jnp.mean(z, axis=-1, keepdims=True)
    zc = z - mu
    var = jnp.mean(zc * zc, axis=-1, keepdims=True)
    return zc * lax.rsqrt(var + EPS) * g + b


def _mods_kernel(c_ref, w_ref, b_ref, o_ref):
    c = c_ref[...]
    ca = c / (1.0 + jnp.exp(-c))
    o_ref[0] = jnp.dot(ca, w_ref[0], precision=HIGHEST, preferred_element_type=F32) + b_ref[0]


def _mods(c, ada_w, ada_b):
    depth, d, d6 = ada_w.shape
    b = c.shape[0]
    rows = 8
    c8 = jnp.pad(c, ((0, rows - b), (0, 0)))
    tn = 1024
    out = pl.pallas_call(
        _mods_kernel,
        grid=(depth, d6 // tn),
        in_specs=[pl.BlockSpec((rows, d), lambda i, j: (0, 0)),
                  pl.BlockSpec((1, d, tn), lambda i, j: (i, 0, j)),
                  pl.BlockSpec((1, 1, tn), lambda i, j: (i, 0, j))],
        out_specs=pl.BlockSpec((1, rows, tn), lambda i, j: (i, 0, j)),
        out_shape=jax.ShapeDtypeStruct((depth, rows, d6), F32),
        compiler_params=_cparams(("arbitrary", "arbitrary"), VMEM_LIMIT),
        name="adaln_mods",
    )(c8, ada_w, ada_b.reshape(depth, 1, d6))
    return out[:, :b]


def _rope_kernel(pos_ref, inv_ref, cos_ref, sin_ref):
    ang = pos_ref[...].astype(F32) * inv_ref[...]
    cos_ref[...] = jnp.cos(ang)
    sin_ref[...] = jnp.sin(ang)


def _rope_tables(positions):
    n = positions.size
    half = MLA_ROPE // 2
    inv = ROPE_THETA ** (-jnp.arange(0, MLA_ROPE, 2, dtype=F32) / MLA_ROPE)
    inv_row = jnp.tile(inv, LANES // half).reshape(1, LANES)
    pos = jnp.broadcast_to(positions.reshape(n, 1), (n, LANES))
    tm = 1024
    spec = pl.BlockSpec((tm, LANES), lambda i: (i, 0))
    return pl.pallas_call(
        _rope_kernel,
        grid=(n // tm,),
        in_specs=[spec, pl.BlockSpec((1, LANES), lambda i: (0, 0))],
        out_specs=[spec, spec],
        out_shape=[jax.ShapeDtypeStruct((n, LANES), F32)] * 2,
        compiler_params=_cparams(("arbitrary",)),
        name="rope_tables",
    )(pos, inv_row)


def _mla_front_kernel(x_ref, sc_ref, sh_ref, cos_ref, sin_ref, win_ref, qn_ref, kvn_ref, wq_ref, wkv_ref,
                      q_ref, k_ref, v_ref, *, qscale):
    h = (x_ref[...] * (1.0 + sc_ref[0]) + sh_ref[0]).astype(BF16)
    proj = _dot(h, win_ref[...])
    cq = proj[:, :Q_RANK_PAD]
    ckv = proj[:, Q_RANK_PAD:Q_RANK_PAD + MLA_KV_RANK]
    cq = cq * lax.rsqrt(jnp.sum(cq * cq, -1, keepdims=True) * (1.0 / MLA_Q_RANK) + EPS) * qn_ref[...]
    ckv = ckv * lax.rsqrt(jnp.mean(ckv * ckv, -1, keepdims=True) + EPS) * kvn_ref[...]
    cq = cq.astype(BF16)
    ckv = ckv.astype(BF16)
    cos = cos_ref[...]
    sin = sin_ref[...]
    o = Q_RANK_PAD + MLA_KV_RANK
    kr = proj[:, o:o + LANES] * cos + proj[:, o + LANES:o + 2 * LANES] * sin
    lane = lax.broadcasted_iota(I32, kr.shape, 1)
    kr_even = jnp.where(lane < MLA_ROPE, kr, 0.0).astype(BF16)
    kr_odd = jnp.where(lane >= MLA_ROPE, kr, 0.0).astype(BF16)
    ones_col = jnp.where(lane == 0, 1.0, 0.0).astype(BF16)

    nheads = MLA_HEADS
    nope_w = nheads * MLA_NOPE
    rope_w = nheads * MLA_ROPE
    group = 4
    for g in range(nheads // group):
        qn = _dot(cq, wq_ref[:, g * group * MLA_NOPE:(g + 1) * group * MLA_NOPE]) * qscale
        kv = _dot(ckv, wkv_ref[:, g * group * 2 * LANES:(g + 1) * group * 2 * LANES])
        for j in range(group):
            hh = g * group + j
            q_ref[0, hh, :, :LANES] = qn[:, j * LANES:(j + 1) * LANES].astype(BF16)
            k_ref[0, hh, :, :LANES] = kv[:, 2 * j * LANES:(2 * j + 1) * LANES].astype(BF16)
            v_ref[0, hh, :, :LANES] = kv[:, (2 * j + 1) * LANES:(2 * j + 2) * LANES].astype(BF16)
            k_ref[0, hh, :, LANES:] = kr_even if hh % 2 == 0 else kr_odd
            v_ref[0, hh, :, LANES:] = ones_col
    for p in range(nheads // 2):
        lo = nope_w + p * LANES
        qr = (_dot(cq, wq_ref[:, lo:lo + LANES]) * cos
              + _dot(cq, wq_ref[:, rope_w + lo:rope_w + lo + LANES]) * sin) * qscale
        qr = qr.astype(BF16)
        q_ref[0, 2 * p, :, LANES:] = qr
        q_ref[0, 2 * p + 1, :, LANES:] = qr


def _mla_front(x2d, sc, sh, cos, sin, w_in, q_norm, w_uq, kv_norm, w_ukv, batch, seq):
    n, d = x2d.shape
    nh = MLA_HEADS
    half = MLA_ROPE // 2
    wq_lat = jnp.pad(w_in[:, :MLA_Q_RANK], ((0, 0), (0, Q_RANK_PAD - MLA_Q_RANK)))
    wkv_lat = w_in[:, MLA_Q_RANK:MLA_Q_RANK + MLA_KV_RANK]
    wkr = w_in[:, MLA_Q_RANK + MLA_KV_RANK:]
    wkr_rot = jnp.concatenate([-wkr[:, half:], wkr[:, :half]], 1)
    win = jnp.concatenate([wq_lat, wkv_lat, wkr, wkr, wkr_rot, wkr_rot], 1).astype(BF16)
    wq3 = w_uq.reshape(MLA_Q_RANK, nh, MLA_NOPE + MLA_ROPE)
    wq_nope = wq3[:, :, :MLA_NOPE].reshape(MLA_Q_RANK, nh * MLA_NOPE)
    wq_r = wq3[:, :, MLA_NOPE:]
    wq_rope = wq_r.reshape(MLA_Q_RANK, nh * MLA_ROPE)
    wq_rot = jnp.concatenate([-wq_r[:, :, half:], wq_r[:, :, :half]], -1).reshape(MLA_Q_RANK, nh * MLA_ROPE)
    wq = jnp.pad(jnp.concatenate([wq_nope, wq_rope, wq_rot], 1),
                 ((0, Q_RANK_PAD - MLA_Q_RANK), (0, 0))).astype(BF16)
    wkv = w_ukv.astype(BF16)
    qn = jnp.pad(q_norm, (0, Q_RANK_PAD - MLA_Q_RANK)).reshape(1, Q_RANK_PAD)
    kvn = kv_norm.reshape(1, MLA_KV_RANK)
    qscale = (MLA_NOPE + MLA_ROPE) ** -0.5 * math.log2(math.e)

    tm = ROW_TILE
    tiles_per_seq = seq // tm
    row = lambda i: (i, 0)
    per_b = lambda i: (i // tiles_per_seq, 0, 0)
    head_out = pl.BlockSpec((1, nh, tm, 2 * LANES), lambda i: (i // tiles_per_seq, 0, i % tiles_per_seq, 0))
    out_sds = jax.ShapeDtypeStruct((batch, nh, seq, 2 * LANES), BF16)
    return pl.pallas_call(
        functools.partial(_mla_front_kernel, qscale=qscale),
        grid=(n // tm,),
        in_specs=[pl.BlockSpec((tm, d), row),
                  pl.BlockSpec((1, 1, d), per_b), pl.BlockSpec((1, 1, d), per_b),
                  pl.BlockSpec((tm, LANES), row), pl.BlockSpec((tm, LANES), row),
                  _const_spec(win.shape), _const_spec(qn.shape), _const_spec(kvn.shape),
                  _const_spec(wq.shape), _const_spec(wkv.shape)],
        out_specs=[head_out, head_out, head_out],
        out_shape=[out_sds, out_sds, out_sds],
        compiler_params=_cparams(("arbitrary",), VMEM_LIMIT),
        name="mla_front",
    )(x2d, sc, sh, cos, sin, win, qn, kvn, wq, wkv)


def _flash_kernel(q_ref, k_ref, v_ref, o_ref, m_ref, acc_ref, *, blk):
    i = pl.program_id(2)
    q = q_ref[0, 0]
    m_ref[...] = jnp.full(m_ref.shape, NEG_BIG, F32)
    acc_ref[...] = jnp.zeros(acc_ref.shape, F32)

    def step(kb, masked):
        start = pl.multiple_of(kb * blk, blk)
        k = k_ref[0, 0, pl.ds(start, blk), :]
        v = v_ref[0, 0, pl.ds(start, blk), :]
        s = _dot_nt(q, k)
        if masked:
            shift = CHUNK.bit_length() - 1
            r = lax.shift_right_logical(lax.broadcasted_iota(I32, s.shape, 0), shift)
            c = lax.shift_right_logical(lax.broadcasted_iota(I32, s.shape, 1), shift)
            s = jnp.where(c <= r, s, NEG_BIG)
        m_prev = m_ref[...]
        m_new = jnp.maximum(m_prev, jnp.max(s, axis=1, keepdims=True))
        p = jnp.exp2(s - m_new)
        alpha = jnp.exp2(m_prev - m_new)
        acc_ref[...] = alpha * acc_ref[...] + _dot(p.astype(BF16), v)
        m_ref[...] = m_new

    def full_step(kb, carry):
        step(kb, False)
        return carry

    lax.fori_loop(0, i, full_step, 0)
    step(i, True)
    acc = acc_ref[...]
    o_ref[0] = (acc[:, :MLA_V] / acc[:, MLA_V:MLA_V + 1]).astype(o_ref.dtype)


def _flash_attention(q, k, v):
    batch, nh, seq, dk = q.shape
    blk = ATT_BLOCK
    qspec = pl.BlockSpec((1, 1, blk, dk), lambda b, h, i: (b, h, i, 0))
    kvspec = pl.BlockSpec((1, 1, seq, dk), lambda b, h, i: (b, h, 0, 0))
    return pl.pallas_call(
        functools.partial(_flash_kernel, blk=blk),
        grid=(batch, nh, seq // blk),
        in_specs=[qspec, kvspec, kvspec],
        out_specs=pl.BlockSpec((1, blk, MLA_V), lambda b, h, i: (b, i, h)),
        out_shape=jax.ShapeDtypeStruct((batch, seq, nh * MLA_V), BF16),
        scratch_shapes=[pltpu.VMEM((blk, 1), F32), pltpu.VMEM((blk, dk), F32)],
        compiler_params=_cparams(("arbitrary", "arbitrary", "arbitrary"), VMEM_LIMIT),
        name="flash_attention",
    )(q, k, v)


def _outproj_kernel(a_ref, x_ref, g1_ref, lg_ref, lb_ref, sc_ref, sh_ref, w_ref, rwh_ref, rwl_ref, rb_ref,
                    x1_ref, h2_ref, logit_ref):
    y = _dot(a_ref[...], w_ref[...])
    z = ALPHA * x_ref[...] + (1.0 + g1_ref[0]) * y
    x1 = _layer_norm(z, lg_ref[...], lb_ref[...])
    x1_ref[...] = x1
    h2 = x1 * (1.0 + sc_ref[0]) + sh_ref[0]
    h2_ref[...] = h2
    logit_ref[...] = _dot3(h2, rwh_ref[...], rwl_ref[...]) + rb_ref[...]


def _outproj_ln_router(a2d, x2d, w_out, g1, ln_g, ln_b, sc2, sh2, w_router, b_router, seq):
    n, d = x2d.shape
    da = a2d.shape[1]
    ne = w_router.shape[1]
    w = w_out.astype(BF16)
    rw = jnp.pad(w_router, ((0, 0), (0, LANES - ne)))
    rw_hi, rw_lo = _split_bf16(rw)
    rb = jnp.pad(b_router, (0, LANES - ne), constant_values=NEG_BIG).reshape(1, LANES)
    tm = ROW_TILE
    tiles_per_seq = seq // tm
    row = lambda i: (i, 0)
    per_b = lambda i: (i // tiles_per_seq, 0, 0)
    vec = pl.BlockSpec((1, 1, d), per_b)
    return pl.pallas_call(
        _outproj_kernel,
        grid=(n // tm,),
        in_specs=[pl.BlockSpec((tm, da), row), pl.BlockSpec((tm, d), row), vec,
                  _const_spec((1, d)), _const_spec((1, d)), vec, vec,
                  _const_spec(w.shape), _const_spec(rw_hi.shape), _const_spec(rw_lo.shape),
                  _const_spec(rb.shape)],
        out_specs=[pl.BlockSpec((tm, d), row), pl.BlockSpec((tm, d), row), pl.BlockSpec((tm, LANES), row)],
        out_shape=[jax.ShapeDtypeStruct((n, d), F32), jax.ShapeDtypeStruct((n, d), F32),
                   jax.ShapeDtypeStruct((n, LANES), F32)],
        compiler_params=_cparams(("arbitrary",), VMEM_LIMIT),
        name="outproj_ln_router",
    )(a2d, x2d, g1, ln_g.reshape(1, d), ln_b.reshape(1, d), sc2, sh2, w, rw_hi, rw_lo, rb)


def _gate_log(g, is_input_gate):
    g = GATE_SOFTCAP * jnp.tanh(g * (1.0 / GATE_SOFTCAP))
    log_f = jnp.minimum(g, 0.0) - jnp.log(1.0 + jnp.exp(-jnp.abs(g)))
    return jnp.where(is_input_gate, g, log_f)


def _ml_inproj_kernel(x_ref, sc_ref, sh_ref, w_ref, gwh_ref, gwl_ref, gb_ref, gwth_ref, gwtl_ref, gbt_ref,
                      proj_ref, gcol_ref, grow_ref, h_scr):
    j = pl.program_id(1)

    @pl.when(j == 0)
    def _():
        h = x_ref[...] * (1.0 + sc_ref[0]) + sh_ref[0]
        h_hi, h_lo = _split_bf16(h)
        h_scr[...] = h_hi
        nh = ML_HEADS
        g = _dot(h_hi, gwh_ref[...]) + (_dot(h_lo, gwh_ref[...]) + _dot(h_hi, gwl_ref[...])) + gb_ref[...]
        lane = lax.broadcasted_iota(I32, g.shape, 1)
        gcol_ref[...] = _gate_log(g, lane < nh)
        gt = (_dot_nt(gwth_ref[...], h_hi) + (_dot_nt(gwth_ref[...], h_lo) + _dot_nt(gwtl_ref[...], h_hi))
              + gbt_ref[...])
        sub = lax.broadcasted_iota(I32, gt.shape, 0)
        grow_ref[...] = _gate_log(gt, sub < nh)

    proj_ref[...] = _dot(h_scr[...], w_ref[...]).astype(proj_ref.dtype)


def _ml_inproj(x2d, sc, sh, w_in, b_gates, seq):
    n, d = x2d.shape
    ng = 2 * ML_HEADS
    wide = w_in.shape[1] - ng
    w = w_in[:, :wide].astype(BF16)
    gw = w_in[:, wide:]
    gw_hi, gw_lo = _split_bf16(jnp.pad(gw, ((0, 0), (0, LANES - ng))))
    gwt_hi, gwt_lo = _split_bf16(gw.T)
    gb = jnp.pad(b_gates, (0, LANES - ng)).reshape(1, LANES)
    gbt = b_gates.reshape(ng, 1)
    tm, tn = 512, 1024
    tiles_per_seq = seq // tm
    row = lambda i, j: (i, 0)
    per_b = lambda i, j: (i // tiles_per_seq, 0, 0)
    const = lambda shape: pl.BlockSpec(shape, lambda i, j: (0,) * len(shape), pipeline_mode=pl.Buffered(1))
    return pl.pallas_call(
        _ml_inproj_kernel,
        grid=(n // tm, wide // tn),
        in_specs=[pl.BlockSpec((tm, d), row),
                  pl.BlockSpec((1, 1, d), per_b), pl.BlockSpec((1, 1, d), per_b),
                  pl.BlockSpec((d, tn), lambda i, j: (0, j)),
                  const(gw_hi.shape), const(gw_lo.shape), const(gb.shape),
                  const(gwt_hi.shape), const(gwt_lo.shape), const(gbt.shape)],
        out_specs=[pl.BlockSpec((tm, tn), lambda i, j: (i, j)),
                   pl.BlockSpec((tm, LANES), row),
                   pl.BlockSpec((ng, tm), lambda i, j: (0, i))],
        out_shape=[jax.ShapeDtypeStruct((n, wide), BF16), jax.ShapeDtypeStruct((n, LANES), F32),
                   jax.ShapeDtypeStruct((ng, n), F32)],
        scratch_shapes=[pltpu.VMEM((tm, d), BF16)],
        compiler_params=_cparams(("arbitrary", "arbitrary"), VMEM_LIMIT),
        name="mlstm_inproj",
    )(x2d, sc, sh, w, gw_hi, gw_lo, gb, gwt_hi, gwt_lo, gbt)


def _split3_bf16(x):
    a = x.astype(BF16)
    r = x - a.astype(F32)
    b = r.astype(BF16)
    c = (r - b.astype(F32)).astype(BF16)
    return a, b, c


def _mlstm_kernel(q_ref, k_ref, v_ref, og_ref, gc_ref, gr_ref, hn_ref, o_ref, c_scr, m_scr, *, L):
    ci = pl.program_id(2)

    @pl.when(ci == 0)
    def _():
        c_scr[...] = jnp.zeros(c_scr.shape, F32)
        m_scr[...] = jnp.zeros(m_scr.shape, F32)

    q = q_ref[...]
    k = k_ref[...]
    v = v_ref[...]
    ones_col = jnp.where(lax.broadcasted_iota(I32, (L, LANES), 1) == 0, 1.0, 0.0).astype(BF16)
    v_ext = jnp.concatenate([v, ones_col], axis=1)

    gc = gc_ref[0]
    gr = gr_ref[0]
    li_c, lf_c = gc[:, 0:1], gc[:, 1:2]
    li_r, lf_r = gr[0:1, :], gr[1:2, :]

    row = lax.broadcasted_iota(I32, (L, L), 0)
    col = lax.broadcasted_iota(I32, (L, L), 1)
    causal = col <= row
    tri = jnp.where(causal, 1.0, 0.0).astype(BF16)

    lfc3 = _split3_bf16(jnp.broadcast_to(lf_c, (L, LANES)))
    b_c = (_dot(tri, lfc3[0]) + (_dot(tri, lfc3[1]) + _dot(tri, lfc3[2])))[:, 0:1]
    lfr3 = _split3_bf16(jnp.broadcast_to(lf_r, (8, L)))
    b_r = (_dot_nt(lfr3[0], tri) + (_dot_nt(lfr3[1], tri) + _dot_nt(lfr3[2], tri)))[0:1, :]

    m_prev = m_scr[...]
    dm = jnp.where(causal, b_c - (b_r - li_r), NEG_BIG)
    inter = b_c + m_prev
    m_t = jnp.maximum(inter, jnp.max(dm, axis=1, keepdims=True))
    w_intra = jnp.exp(dm - m_t)
    w_inter = jnp.exp(inter - m_t)

    kscale = ML_QK ** -0.5
    qk = _dot_nt(q, k) * (w_intra * kscale)
    c_state = c_scr[...]
    num_ext = w_inter * _dot(q, c_state.astype(BF16)) + _dot(qk.astype(BF16), v_ext)
    vdim = v.shape[1]
    num = num_ext[:, :vdim]
    den = num_ext[:, vdim:vdim + 1]
    h = num / jnp.maximum(jnp.abs(den), jnp.exp(-m_t))
    hn = h * lax.rsqrt(jnp.mean(h * h, axis=-1, keepdims=True) + EPS) * hn_ref[...]
    og = og_ref[...].astype(F32)
    o_ref[...] = (hn / (1.0 + jnp.exp(-og))).astype(o_ref.dtype)

    b_end = b_r[:, L - 1:L]
    d_end_r = b_end - b_r + li_r
    m_new = jnp.maximum(b_end + m_prev, jnp.max(d_end_r, axis=1, keepdims=True))
    decay = jnp.exp(b_end + m_prev - m_new)
    w_s = jnp.exp(b_end - b_c + li_c - m_new) * kscale
    kw = (k.astype(F32) * w_s).astype(BF16)
    upd = lax.dot_general(kw, v_ext, (((0,), (0,)), ((), ())), preferred_element_type=F32)
    c_scr[...] = decay * c_state + upd
    m_scr[...] = m_new


def _mlstm_scan(proj, gcol, grow, head_norm, batch, seq):
    n = proj.shape[0]
    nh = ML_HEADS
    L = ML_CHUNK
    nc = seq // L
    gc = gcol[:, :2 * nh].reshape(n, 2, nh).transpose(2, 0, 1)
    gr = grow.reshape(2, nh, n).transpose(1, 0, 2)
    hn = head_norm.reshape(1, nh * ML_V)
    qk_blocks = nh * ML_QK // ML_QK
    v_off = 2 * nh * ML_QK // ML_V
    og_off = v_off + nh
    tok = lambda b, h, c: b * nc + c
    return pl.pallas_call(
        functools.partial(_mlstm_kernel, L=L),
        grid=(batch, nh, nc),
        in_specs=[pl.BlockSpec((L, ML_QK), lambda b, h, c: (tok(b, h, c), h)),
                  pl.BlockSpec((L, ML_QK), lambda b, h, c: (tok(b, h, c), qk_blocks + h)),
                  pl.BlockSpec((L, ML_V), lambda b, h, c: (tok(b, h, c), v_off + h)),
                  pl.BlockSpec((L, ML_V), lambda b, h, c: (tok(b, h, c), og_off + h)),
                  pl.BlockSpec((1, L, 2), lambda b, h, c: (h, tok(b, h, c), 0)),
                  pl.BlockSpec((1, 2, L), lambda b, h, c: (h, 0, tok(b, h, c))),
                  pl.BlockSpec((1, ML_V), lambda b, h, c: (0, h))],
        out_specs=pl.BlockSpec((L, ML_V), lambda b, h, c: (tok(b, h, c), h)),
        out_shape=jax.ShapeDtypeStruct((n, nh * ML_V), BF16),
        scratch_shapes=[pltpu.VMEM((ML_QK, ML_V + LANES), F32), pltpu.VMEM((1, 1), F32)],
        compiler_params=_cparams(("arbitrary", "arbitrary", "arbitrary"), VMEM_LIMIT),
        name="mlstm_scan",
    )(proj, proj, proj, proj, gc, gr, hn)


def _route_kernel(lg_ref, dest_ref, gate_ref, meta_ref, blk_ref, tri_scr, cnt_scr, run_scr, pst_scr,
                  *, T, E, K, BLK, NBP):
    ph = pl.program_id(0)
    t = pl.program_id(1)
    nt = pl.num_programs(1)

    @pl.when((ph == 0) & (t == 0))
    def _():
        r = lax.broadcasted_iota(I32, (T, T), 0)
        c = lax.broadcasted_iota(I32, (T, T), 1)
        tri_scr[...] = jnp.where(c < r, 1.0, 0.0).astype(BF16)
        cnt_scr[...] = jnp.zeros(cnt_scr.shape, F32)

    lane = lax.broadcasted_iota(I32, (T, LANES), 1)
    l = jnp.where(lane < E, lg_ref[...], -jnp.inf)
    vals, hots = [], []
    for _ in range(K):
        mx = jnp.max(l, axis=1, keepdims=True)
        idx = jnp.min(jnp.where(l == mx, lane, LANES), axis=1, keepdims=True)
        hot = lane == idx
        vals.append(mx)
        hots.append(hot)
        l = jnp.where(hot, -jnp.inf, l)
    hot_all = jnp.zeros((T, LANES), F32)
    for hot in hots:
        hot_all = hot_all + jnp.where(hot, 1.0, 0.0)
    colsum = jnp.sum(hot_all, axis=0, keepdims=True)

    @pl.when(ph == 0)
    def _():
        cnt_scr[...] = cnt_scr[...] + colsum

    @pl.when((ph == 1) & (t == 0))
    def _():
        cnt = cnt_scr[...].astype(I32)
        pc = ((cnt + (BLK - 1)) & (-BLK)).astype(F32)
        r = lax.broadcasted_iota(I32, (LANES, LANES), 0)
        c = lax.broadcasted_iota(I32, (LANES, LANES), 1)
        upper = jnp.where(r < c, 1.0, 0.0)
        pstart = jnp.dot(jnp.broadcast_to(pc, (8, LANES)), upper, precision=HIGHEST,
                         preferred_element_type=F32)[0:1, :]
        pst_scr[...] = pstart
        run_scr[...] = jnp.zeros(run_scr.shape, F32)
        pend = pstart + pc
        meta_ref[0:1, :] = cnt_scr[...].astype(I32)
        meta_ref[1:2, :] = pstart.astype(I32)
        meta_ref[2:3, :] = pend.astype(I32)
        meta_ref[3:8, :] = jnp.zeros((5, LANES), I32)
        jstart = (lax.broadcasted_iota(I32, (NBP, LANES), 0) * BLK).astype(F32)
        elane = lax.broadcasted_iota(I32, (NBP, LANES), 1)
        owned = jnp.where((pend <= jstart) & (elane < E), 1.0, 0.0)
        be = jnp.minimum(jnp.sum(owned, axis=1, keepdims=True), float(E - 1))
        blk_ref[...] = jnp.broadcast_to(be, (NBP, LANES)).astype(I32)

    @pl.when(ph == 1)
    def _():
        earlier = _dot(tri_scr[...], hot_all.astype(BF16))
        pos = earlier + (pst_scr[...] + run_scr[...])
        e0 = jnp.ones_like(vals[0])
        es = [e0] + [jnp.exp(v - vals[0]) for v in vals[1:]]
        tot = es[0]
        for e in es[1:]:
            tot = tot + e
        for kk in range(K):
            d = jnp.sum(jnp.where(hots[kk], pos, 0.0), axis=1, keepdims=True)
            dest_ref[:, kk:kk + 1] = d.astype(I32)
            gate_ref[:, kk:kk + 1] = es[kk] / tot
        run_scr[...] = run_scr[...] + colsum


def _route(logits, n_blocks):
    n = logits.shape[0]
    T = min(ROUTE_TILE, n)
    nbp = -(-n_blocks // 8) * 8
    kern = functools.partial(_route_kernel, T=T, E=N_EXPERTS, K=TOP_K, BLK=MOE_SLOT_BLOCK, NBP=nbp)
    tok_out = lambda ph, t: (t * ph, 0)
    fixed = lambda ph, t: (0, 0)
    return pl.pallas_call(
        kern,
        grid=(2, n // T),
        in_specs=[pl.BlockSpec((T, LANES), lambda ph, t: (t, 0))],
        out_specs=[pl.BlockSpec((T, TOP_K), tok_out), pl.BlockSpec((T, TOP_K), tok_out),
                   pl.BlockSpec((8, LANES), fixed), pl.BlockSpec((nbp, LANES), fixed)],
        out_shape=[jax.ShapeDtypeStruct((n, TOP_K), I32), jax.ShapeDtypeStruct((n, TOP_K), F32),
                   jax.ShapeDtypeStruct((8, LANES), I32), jax.ShapeDtypeStruct((nbp, LANES), I32)],
        scratch_shapes=[pltpu.VMEM((T, T), BF16), pltpu.VMEM((1, LANES), F32), pltpu.VMEM((1, LANES), F32),
                        pltpu.VMEM((1, LANES), F32)],
        compiler_params=_cparams(("arbitrary", "arbitrary")),
        name="moe_route",
    )(logits)


def _dispatch_kernel(meta_ref, dest_ref, h_ref, xs_ref, zero_scr, sem, zsem, *, T, K, E, BLK, NB):
    i = pl.program_id(0)
    last = pl.num_programs(0) - 1

    def row_copy(r, kk):
        d = dest_ref[r * K + kk]
        return pltpu.make_async_copy(h_ref.at[pl.ds(r, 1)], xs_ref.at[pl.ds(d, 1)], sem)

    def issue(r, carry):
        for kk in range(K):
            row_copy(r, kk).start()
        return carry

    lax.fori_loop(0, T, issue, 0)

    @pl.when(i == last)
    def _():
        zero_scr[...] = jnp.zeros(zero_scr.shape, F32)

        def pad_copy(slot):
            return pltpu.make_async_copy(zero_scr.at[pl.ds(0, 1)], xs_ref.at[pl.ds(slot, 1)], zsem)

        for e in range(E):
            cnt = meta_ref[0, e]
            first = meta_ref[1, e] + cnt
            npad = meta_ref[2, e] - first

            def zissue(r, carry, first=first):
                pad_copy(first + r).start()
                return carry

            def zwait(r, carry, first=first):
                pad_copy(first + r).wait()
                return carry

            lax.fori_loop(0, npad, zissue, 0)
            lax.fori_loop(0, npad, zwait, 0)

        used = meta_ref[2, E - 1] // BLK

        def tail_copy(j):
            return pltpu.make_async_copy(zero_scr, xs_ref.at[pl.ds(pl.multiple_of(j * BLK, BLK), BLK)], zsem)

        def tissue(j, carry):
            tail_copy(j).start()
            return carry

        def twait(j, carry):
            tail_copy(j).wait()
            return carry

        lax.fori_loop(used, NB, tissue, 0)
        lax.fori_loop(used, NB, twait, 0)

    def drain(r, carry):
        for kk in range(K):
            row_copy(r, kk).wait()
        return carry

    lax.fori_loop(0, T, drain, 0)


def _dispatch(h2, dest_flat, meta, n_blocks):
    n, d = h2.shape
    T = ROW_TILE
    blk = MOE_SLOT_BLOCK
    kern = functools.partial(_dispatch_kernel, T=T, K=TOP_K, E=N_EXPERTS, BLK=blk, NB=n_blocks)
    gs = pltpu.PrefetchScalarGridSpec(
        num_scalar_prefetch=1,
        grid=(n // T,),
        in_specs=[pl.BlockSpec((T * TOP_K,), lambda i, m: (i,), memory_space=pltpu.SMEM),
                  pl.BlockSpec((T, d), lambda i, m: (i, 0))],
        out_specs=pl.BlockSpec(memory_space=pl.ANY),
        scratch_shapes=[pltpu.VMEM((blk, d), F32), pltpu.SemaphoreType.DMA(()), pltpu.SemaphoreType.DMA(())],
    )
    return pl.pallas_call(
        kern,
        grid_spec=gs,
        out_shape=jax.ShapeDtypeStruct((n_blocks * blk, d), F32),
        compiler_params=_cparams(("arbitrary",), VMEM_LIMIT),
        name="moe_dispatch",
    )(meta, dest_flat, h2)


def _expert_kernel(be_ref, used_ref, xs_ref, wgu_ref, bgu_ref, wd_ref, bd_ref, ys_ref):
    j = pl.program_id(0)

    @pl.when(j < used_ref[0])
    def _():
        x = xs_ref[...].astype(BF16)
        gu = _dot(x, wgu_ref[0]) + bgu_ref[0]
        de = gu.shape[1] // 2
        a = jnp.minimum(gu[:, :de], SWIGLU_LIMIT)
        u = jnp.clip(gu[:, de:], -SWIGLU_LIMIT, SWIGLU_LIMIT)
        act = (u + 1.0) * (a / (1.0 + jnp.exp(-SWIGLU_ALPHA * a)))
        ys_ref[...] = _dot(act.astype(BF16), wd_ref[0]) + bd_ref[0]

    @pl.when(j >= used_ref[0])
    def _():
        ys_ref[...] = jnp.zeros(ys_ref.shape, F32)


def _experts(xs, block_e, used, w_gu, b_gu, w_down, b_down):
    p, d = xs.shape
    blk = MOE_SLOT_BLOCK
    ne, _, de2 = w_gu.shape
    nb = p // blk
    last_used = lambda j, be, used: jnp.minimum(j, jnp.maximum(used[0] - 1, 0))
    gs = pltpu.PrefetchScalarGridSpec(
        num_scalar_prefetch=2,
        grid=(nb,),
        in_specs=[pl.BlockSpec((blk, d), lambda j, be, used: (last_used(j, be, used), 0)),
                  pl.BlockSpec((1, d, de2), lambda j, be, used: (be[j], 0, 0)),
                  pl.BlockSpec((1, 1, de2), lambda j, be, used: (be[j], 0, 0)),
                  pl.BlockSpec((1, de2 // 2, d), lambda j, be, used: (be[j], 0, 0)),
                  pl.BlockSpec((1, 1, d), lambda j, be, used: (be[j], 0, 0))],
        out_specs=pl.BlockSpec((blk, d), lambda j, be, used: (j, 0)),
    )
    return pl.pallas_call(
        _expert_kernel,
        grid_spec=gs,
        out_shape=jax.ShapeDtypeStruct((p, d), F32),
        compiler_params=_cparams(("arbitrary",), VMEM_LIMIT),
        name="moe_experts",
    )(block_e, used, xs, w_gu.astype(BF16), b_gu.reshape(ne, 1, de2), w_down.astype(BF16),
      b_down.reshape(ne, 1, d))


def _combine_kernel(dest_ref, ys_ref, gate_ref, x_ref, g2_ref, lg_ref, lb_ref, o_ref, buf, sem, *, T, K):
    def row_copy(r, kk):
        d = dest_ref[r * K + kk]
        return pltpu.make_async_copy(ys_ref.at[pl.ds(d, 1)], buf.at[kk, pl.ds(r, 1)], sem)

    def issue(r, carry):
        for kk in range(K):
            row_copy(r, kk).start()
        return carry

    def drain(r, carry):
        for kk in range(K):
            row_copy(r, kk).wait()
        return carry

    lax.fori_loop(0, T, issue, 0)
    lax.fori_loop(0, T, drain, 0)
    g = gate_ref[...]
    y = g[:, 0:1] * buf[0]
    for kk in range(1, K):
        y = y + g[:, kk:kk + 1] * buf[kk]
    z = ALPHA * x_ref[...] + (1.0 + g2_ref[0]) * y
    o_ref[...] = _layer_norm(z, lg_ref[...], lb_ref[...])


def _combine(ys, dest_flat, gates, x1, g2, ln_g, ln_b, seq):
    n, d = x1.shape
    T = ROW_TILE
    tiles_per_seq = seq // T
    row = lambda i: (i, 0)
    return pl.pallas_call(
        functools.partial(_combine_kernel, T=T, K=TOP_K),
        grid=(n // T,),
        in_specs=[pl.BlockSpec((T * TOP_K,), lambda i: (i,), memory_space=pltpu.SMEM),
                  pl.BlockSpec(memory_space=pl.ANY),
                  pl.BlockSpec((T, TOP_K), row), pl.BlockSpec((T, d), row),
                  pl.BlockSpec((1, 1, d), lambda i: (i // tiles_per_seq, 0, 0)),
                  _const_spec((1, d)), _const_spec((1, d))],
        out_specs=pl.BlockSpec((T, d), row),
        out_shape=jax.ShapeDtypeStruct((n, d), F32),
        scratch_shapes=[pltpu.VMEM((TOP_K, T, d), F32), pltpu.SemaphoreType.DMA(())],
        compiler_params=_cparams(("arbitrary",), VMEM_LIMIT),
        name="moe_combine",
    )(dest_flat, ys, gates, x1, g2, ln_g.reshape(1, d), ln_b.reshape(1, d))


def _moe(h2, logits, x1, g2, ln_g, ln_b, w_gu, b_gu, w_down, b_down, seq):
    n = h2.shape[0]
    blk = MOE_SLOT_BLOCK
    n_blocks = -(-(n * TOP_K + N_EXPERTS * (blk - 1)) // blk)
    dest, gates, meta, block_e = _route(logits, n_blocks)
    dest_flat = dest.reshape(n * TOP_K)
    xs = _dispatch(h2, dest_flat, meta, n_blocks)
    used = (meta[2, N_EXPERTS - 1] // blk).reshape(1)
    ys = _experts(xs, block_e[:n_blocks, 0], used, w_gu, b_gu, w_down, b_down)
    return _combine(ys, dest_flat, gates, x1, g2, ln_g, ln_b, seq)


def kernel(x, c, positions, ada_w, ada_b, ln_g, ln_b, mla_w_in, mla_q_norm, mla_w_uq, mla_kv_norm, mla_w_ukv,
           mla_w_out, ml_w_in, ml_b_gates, ml_head_norm, ml_w_out, moe_w_router, moe_b_router, moe_w_gu,
           moe_b_gu, moe_w_down, moe_b_down):
    batch, seq, d = x.shape
    n = batch * seq
    mods = _mods(c, ada_w, ada_b)
    xf = x.reshape(n, d)
    for i in range(DEPTH):
        j = i // 2
        sh1, sc1, g1, sh2, sc2, g2 = [mods[i, :, s * d:(s + 1) * d].reshape(batch, 1, d) for s in range(6)]
        if i % 2 == 0:
            cos, sin = _rope_tables(positions)
            q, k, v = _mla_front(xf, sc1, sh1, cos, sin, mla_w_in[j], mla_q_norm[j], mla_w_uq[j],
                                 mla_kv_norm[j], mla_w_ukv[j], batch, seq)
            a = _flash_attention(q, k, v).reshape(n, MLA_HEADS * MLA_V)
            w_out = mla_w_out[j]
        else:
            proj, gcol, grow = _ml_inproj(xf, sc1, sh1, ml_w_in[j], ml_b_gates[j], seq)
            a = _mlstm_scan(proj, gcol, grow, ml_head_norm[j], batch, seq)
            w_out = ml_w_out[j]
        x1, h2, logits = _outproj_ln_router(a, xf, w_out, g1, ln_g[i, 0], ln_b[i, 0], sc2, sh2,
                                            moe_w_router[i], moe_b_router[i], seq)
        xf = _moe(h2, logits, x1, g2, ln_g[i, 1], ln_b[i, 1], moe_w_gu[i], moe_b_gu[i], moe_w_down[i],
                  moe_b_down[i], seq)
    return xf.reshape(batch, seq, d)
```

```python
import functools
import math

import jax
import jax.numpy as jnp
from jax import lax
from jax.experimental import pallas as pl
from jax.experimental.pallas import tpu as pltpu

F32 = jnp.float32
BF16 = jnp.bfloat16
I32 = jnp.int32
HIGHEST = lax.Precision.HIGHEST

CHUNK = 64
MLA_HEADS = 16
MLA_NOPE = 128
MLA_ROPE = 64
MLA_V = 128
MLA_Q_RANK = 448
MLA_KV_RANK = 512
ROPE_THETA = 10000.0
ML_HEADS = 4
ML_QK = 256
ML_V = 512
GATE_SOFTCAP = 15.0
N_EXPERTS = 32
TOP_K = 4
D_EXPERT = 1024
SWIGLU_LIMIT = 7.0
SWIGLU_ALPHA = 1.702
DEPTH = 2
ALPHA = (2 * DEPTH) ** 0.25
EPS = 1e-6
NEG_BIG = -1e30

LANES = 128
VMEM_LIMIT = 56 * 1024 * 1024

Q_RANK_PAD = 512
ATT_BLOCK = 512
ATT_CHAINS = 4
ML_CHUNK = 256
MOE_SLOT_BLOCK = 512
ROUTE_TILE = 1024
ROW_TILE = 256
OUTPROJ_TILE = 512
DMA_UNROLL = 8


def _cparams(sem, vmem=None):
    return pltpu.CompilerParams(dimension_semantics=sem, vmem_limit_bytes=vmem)


def _const_spec(shape):
    nd = len(shape)
    return pl.BlockSpec(shape, lambda *_: (0,) * nd, pipeline_mode=pl.Buffered(1))


def _split_bf16(x):
    hi = x.astype(BF16)
    lo = (x - hi.astype(F32)).astype(BF16)
    return hi, lo


def _dot(a, b):
    return jnp.dot(a, b, preferred_element_type=F32)


def _dot_nt(a, b):
    return lax.dot_general(a, b, (((1,), (1,)), ((), ())), preferred_element_type=F32)


def _dot3(x, w_hi, w_lo):
    x_hi, x_lo = _split_bf16(x)
    return _dot(x_hi, w_hi) + (_dot(x_lo, w_hi) + _dot(x_hi, w_lo))


def _layer_norm(z, g, b):
    mu = jnp.mean(z, axis=-1, keepdims=True)
    zc = z - mu
    var = jnp.mean(zc * zc, axis=-1, keepdims=True)
    return zc * lax.rsqrt(var + EPS) * g + b


def _mods_kernel(c_ref, w_ref, b_ref, o_ref):
    c = c_ref[...]
    ca = c / (1.0 + jnp.exp(-c))
    o_ref[0] = jnp.dot(ca, w_ref[0], precision=HIGHEST, preferred_element_type=F32) + b_ref[0]


def _mods(c, ada_w, ada_b):
    depth, d, d6 = ada_w.shape
    b = c.shape[0]
    rows = 8
    c8 = jnp.pad(c, ((0, rows - b), (0, 0)))
    tn = 1024
    out = pl.pallas_call(
        _mods_kernel,
        grid=(depth, d6 // tn),
        in_specs=[pl.BlockSpec((rows, d), lambda i, j: (0, 0)),
                  pl.BlockSpec((1, d, tn), lambda i, j: (i, 0, j)),
                  pl.BlockSpec((1, 1, tn), lambda i, j: (i, 0, j))],
        out_specs=pl.BlockSpec((1, rows, tn), lambda i, j: (i, 0, j)),
        out_shape=jax.ShapeDtypeStruct((depth, rows, d6), F32),
        compiler_params=_cparams(("arbitrary", "arbitrary"), VMEM_LIMIT),
        name="adaln_mods",
    )(c8, ada_w, ada_b.reshape(depth, 1, d6))
    return out[:, :b]


def _rope_kernel(pos_ref, inv_ref, cos_ref, sin_ref):
    ang = pos_ref[...].astype(F32) * inv_ref[...]
    cos_ref[...] = jnp.cos(ang)
    sin_ref[...] = jnp.sin(ang)


def _rope_tables(positions):
    n = positions.size
    half = MLA_ROPE // 2
    inv = ROPE_THETA ** (-jnp.arange(0, MLA_ROPE, 2, dtype=F32) / MLA_ROPE)
    inv_row = jnp.tile(inv, LANES // half).reshape(1, LANES)
    pos = jnp.broadcast_to(positions.reshape(n, 1), (n, LANES))
    tm = 1024
    spec = pl.BlockSpec((tm, LANES), lambda i: (i, 0))
    return pl.pallas_call(
        _rope_kernel,
        grid=(n // tm,),
        in_specs=[spec, pl.BlockSpec((1, LANES), lambda i: (0, 0))],
        out_specs=[spec, spec],
        out_shape=[jax.ShapeDtypeStruct((n, LANES), F32)] * 2,
        compiler_params=_cparams(("arbitrary",)),
        name="rope_tables",
    )(pos, inv_row)


def _mla_front_kernel(x_ref, sc_ref, sh_ref, cos_ref, sin_ref, win_ref, qn_ref, kvn_ref, wq_ref, wkv_ref,
                      q_ref, k_ref, v_ref, *, qscale):
    h = (x_ref[...] * (1.0 + sc_ref[0]) + sh_ref[0]).astype(BF16)
    proj = _dot(h, win_ref[...])
    cq = proj[:, :Q_RANK_PAD]
    ckv = proj[:, Q_RANK_PAD:Q_RANK_PAD + MLA_KV_RANK]
    cq = cq * lax.rsqrt(jnp.sum(cq * cq, -1, keepdims=True) * (1.0 / MLA_Q_RANK) + EPS) * qn_ref[...]
    ckv = ckv * lax.rsqrt(jnp.mean(ckv * ckv, -1, keepdims=True) + EPS) * kvn_ref[...]
    cq = cq.astype(BF16)
    ckv = ckv.astype(BF16)
    cos = cos_ref[...]
    sin = sin_ref[...]
    o = Q_RANK_PAD + MLA_KV_RANK
    kr = proj[:, o:o + LANES] * cos + proj[:, o + LANES:o + 2 * LANES] * sin
    lane = lax.broadcasted_iota(I32, kr.shape, 1)
    kr_even = jnp.where(lane < MLA_ROPE, kr, 0.0).astype(BF16)
    kr_odd = jnp.where(lane >= MLA_ROPE, kr, 0.0).astype(BF16)
    ones_col = jnp.where(lane == 0, 1.0, 0.0).astype(BF16)

    nheads = MLA_HEADS
    nope_w = nheads * MLA_NOPE
    rope_w = nheads * MLA_ROPE
    group = 4
    for g in range(nheads // group):
        qn = _dot(cq, wq_ref[:, g * group * MLA_NOPE:(g + 1) * group * MLA_NOPE]) * qscale
        kv = _dot(ckv, wkv_ref[:, g * group * 2 * LANES:(g + 1) * group * 2 * LANES])
        for j in range(group):
            hh = g * group + j
            q_ref[0, hh, :, :LANES] = qn[:, j * LANES:(j + 1) * LANES].astype(BF16)
            k_ref[0, hh, :, :LANES] = kv[:, 2 * j * LANES:(2 * j + 1) * LANES].astype(BF16)
            v_ref[0, hh, :, :LANES] = kv[:, (2 * j + 1) * LANES:(2 * j + 2) * LANES].astype(BF16)
            k_ref[0, hh, :, LANES:] = kr_even if hh % 2 == 0 else kr_odd
            v_ref[0, hh, :, LANES:] = ones_col
    for p in range(nheads // 2):
        lo = nope_w + p * LANES
        qr = (_dot(cq, wq_ref[:, lo:lo + LANES]) * cos
              + _dot(cq, wq_ref[:, rope_w + lo:rope_w + lo + LANES]) * sin) * qscale
        qr = qr.astype(BF16)
        q_ref[0, 2 * p, :, LANES:] = qr
        q_ref[0, 2 * p + 1, :, LANES:] = qr


def _mla_front(x2d, sc, sh, cos, sin, w_in, q_norm, w_uq, kv_norm, w_ukv, batch, seq):
    n, d = x2d.shape
    nh = MLA_HEADS
    half = MLA_ROPE // 2
    wq_lat = jnp.pad(w_in[:, :MLA_Q_RANK], ((0, 0), (0, Q_RANK_PAD - MLA_Q_RANK)))
    wkv_lat = w_in[:, MLA_Q_RANK:MLA_Q_RANK + MLA_KV_RANK]
    wkr = w_in[:, MLA_Q_RANK + MLA_KV_RANK:]
    wkr_rot = jnp.concatenate([-wkr[:, half:], wkr[:, :half]], 1)
    win = jnp.concatenate([wq_lat, wkv_lat, wkr, wkr, wkr_rot, wkr_rot], 1).astype(BF16)
    wq3 = w_uq.reshape(MLA_Q_RANK, nh, MLA_NOPE + MLA_ROPE)
    wq_nope = wq3[:, :, :MLA_NOPE].reshape(MLA_Q_RANK, nh * MLA_NOPE)
    wq_r = wq3[:, :, MLA_NOPE:]
    wq_rope = wq_r.reshape(MLA_Q_RANK, nh * MLA_ROPE)
    wq_rot = jnp.concatenate([-wq_r[:, :, half:], wq_r[:, :, :half]], -1).reshape(MLA_Q_RANK, nh * MLA_ROPE)
    wq = jnp.pad(jnp.concatenate([wq_nope, wq_rope, wq_rot], 1),
                 ((0, Q_RANK_PAD - MLA_Q_RANK), (0, 0))).astype(BF16)
    wkv = w_ukv.astype(BF16)
    qn = jnp.pad(q_norm, (0, Q_RANK_PAD - MLA_Q_RANK)).reshape(1, Q_RANK_PAD)
    kvn = kv_norm.reshape(1, MLA_KV_RANK)
    qscale = (MLA_NOPE + MLA_ROPE) ** -0.5 * math.log2(math.e)

    tm = ROW_TILE
    tiles_per_seq = seq // tm
    row = lambda i: (i, 0)
    per_b = lambda i: (i // tiles_per_seq, 0, 0)
    head_out = pl.BlockSpec((1, nh, tm, 2 * LANES), lambda i: (i // tiles_per_seq, 0, i % tiles_per_seq, 0))
    out_sds = jax.ShapeDtypeStruct((batch, nh, seq, 2 * LANES), BF16)
    return pl.pallas_call(
        functools.partial(_mla_front_kernel, qscale=qscale),
        grid=(n // tm,),
        in_specs=[pl.BlockSpec((tm, d), row),
                  pl.BlockSpec((1, 1, d), per_b), pl.BlockSpec((1, 1, d), per_b),
                  pl.BlockSpec((tm, LANES), row), pl.BlockSpec((tm, LANES), row),
                  _const_spec(win.shape), _const_spec(qn.shape), _const_spec(kvn.shape),
                  _const_spec(wq.shape), _const_spec(wkv.shape)],
        out_specs=[head_out, head_out, head_out],
        out_shape=[out_sds, out_sds, out_sds],
        compiler_params=_cparams(("arbitrary",), VMEM_LIMIT),
        name="mla_front",
    )(x2d, sc, sh, cos, sin, win, qn, kvn, wq, wkv)


def _flash_kernel(q_ref, k_ref, v_ref, o_ref, m_ref, acc_ref, *, blk, nsub):
    i = pl.program_id(2)
    m_ref[...] = jnp.full(m_ref.shape, NEG_BIG, F32)
    acc_ref[...] = jnp.zeros(acc_ref.shape, F32)
    slabs = blk // LANES

    def step(c, kb, masked):
        start = pl.multiple_of(kb * blk, blk)
        q = q_ref[0, 0, c * blk:(c + 1) * blk, :]
        k = k_ref[0, 0, pl.ds(start, blk), :]
        v = v_ref[0, 0, pl.ds(start, blk), :]
        s = _dot_nt(q, k)
        if masked:
            shift = CHUNK.bit_length() - 1
            r = lax.shift_right_logical(lax.broadcasted_iota(I32, s.shape, 0), shift)
            cc = lax.shift_right_logical(lax.broadcasted_iota(I32, s.shape, 1), shift)
            s = jnp.where(cc <= r, s, NEG_BIG)
        m_prev = m_ref[c]
        m_new = jnp.maximum(m_prev, jnp.max(s, axis=1, keepdims=True))
        p = jnp.concatenate([jnp.exp2(s[:, j * LANES:(j + 1) * LANES] - m_new).astype(BF16)
                             for j in range(slabs)], axis=1)
        alpha = jnp.exp2(m_prev - m_new)
        pv = _dot(p, v)
        acc_ref[c, :, :LANES] = alpha * acc_ref[c, :, :LANES] + pv[:, :LANES]
        acc_ref[c, :, LANES:] = alpha * acc_ref[c, :, LANES:] + pv[:, LANES:]
        m_ref[c] = m_new

    def full_steps(kb, carry):
        for c in range(nsub):
            step(c, kb, False)
        return carry

    lax.fori_loop(0, nsub * i, full_steps, 0)
    for t in range(nsub):
        for c in range(t, nsub):
            step(c, nsub * i + t, masked=(c == t))
    for c in range(nsub):
        l = acc_ref[c, :, LANES:LANES + 1]
        o_ref[0, c * blk:(c + 1) * blk, :] = (acc_ref[c, :, :MLA_V] / l).astype(o_ref.dtype)


def _flash_attention(q, k, v):
    batch, nh, seq, dk = q.shape
    blk = ATT_BLOCK
    nsub = ATT_CHAINS
    qspec = pl.BlockSpec((1, 1, nsub * blk, dk), lambda b, h, i: (b, h, i, 0))
    kvspec = pl.BlockSpec((1, 1, seq, dk), lambda b, h, i: (b, h, 0, 0))
    return pl.pallas_call(
        functools.partial(_flash_kernel, blk=blk, nsub=nsub),
        grid=(batch, nh, seq // (nsub * blk)),
        in_specs=[qspec, kvspec, kvspec],
        out_specs=pl.BlockSpec((1, nsub * blk, MLA_V), lambda b, h, i: (b, i, h)),
        out_shape=jax.ShapeDtypeStruct((batch, seq, nh * MLA_V), BF16),
        scratch_shapes=[pltpu.VMEM((nsub, blk, LANES), F32), pltpu.VMEM((nsub, blk, dk), F32)],
        compiler_params=_cparams(("arbitrary", "arbitrary", "arbitrary"), VMEM_LIMIT),
        name="flash_attention",
    )(q, k, v)


def _outproj_kernel(a_ref, x_ref, g1_ref, lg_ref, lb_ref, sc_ref, sh_ref, w_ref, rwh_ref, rwl_ref, rb_ref,
                    x1_ref, h2_ref, logit_ref):
    y = _dot(a_ref[...], w_ref[...])
    z = ALPHA * x_ref[...] + (1.0 + g1_ref[0]) * y
    x1 = _layer_norm(z, lg_ref[...], lb_ref[...])
    x1_ref[...] = x1
    h2 = x1 * (1.0 + sc_ref[0]) + sh_ref[0]
    h2_ref[...] = h2
    logit_ref[...] = _dot3(h2, rwh_ref[...], rwl_ref[...]) + rb_ref[...]


def _outproj_ln_router(a2d, x2d, w_out, g1, ln_g, ln_b, sc2, sh2, w_router, b_router, seq):
    n, d = x2d.shape
    da = a2d.shape[1]
    ne = w_router.shape[1]
    w = w_out.astype(BF16)
    rw = jnp.pad(w_router, ((0, 0), (0, LANES - ne)))
    rw_hi, rw_lo = _split_bf16(rw)
    rb = jnp.pad(b_router, (0, LANES - ne), constant_values=NEG_BIG).reshape(1, LANES)
    tm = OUTPROJ_TILE
    tiles_per_seq = seq // tm
    row = lambda i: (i, 0)
    per_b = lambda i: (i // tiles_per_seq, 0, 0)
    vec = pl.BlockSpec((1, 1, d), per_b)
    return pl.pallas_call(
        _outproj_kernel,
        grid=(n // tm,),
        in_specs=[pl.BlockSpec((tm, da), row), pl.BlockSpec((tm, d), row), vec,
                  _const_spec((1, d)), _const_spec((1, d)), vec, vec,
                  _const_spec(w.shape), _const_spec(rw_hi.shape), _const_spec(rw_lo.shape),
                  _const_spec(rb.shape)],
        out_specs=[pl.BlockSpec((tm, d), row), pl.BlockSpec((tm, d), row), pl.BlockSpec((tm, LANES), row)],
        out_shape=[jax.ShapeDtypeStruct((n, d), F32), jax.ShapeDtypeStruct((n, d), F32),
                   jax.ShapeDtypeStruct((n, LANES), F32)],
        compiler_params=_cparams(("arbitrary",), VMEM_LIMIT),
        name="outproj_ln_router",
    )(a2d, x2d, g1, ln_g.reshape(1, d), ln_b.reshape(1, d), sc2, sh2, w, rw_hi, rw_lo, rb)


def _gate_log(g, is_input_gate):
    g = GATE_SOFTCAP * jnp.tanh(g * (1.0 / GATE_SOFTCAP))
    log_f = jnp.minimum(g, 0.0) - jnp.log(1.0 + jnp.exp(-jnp.abs(g)))
    return jnp.where(is_input_gate, g, log_f)


def _ml_inproj_kernel(x_ref, sc_ref, sh_ref, w_ref, gwh_ref, gwl_ref, gb_ref, gwth_ref, gwtl_ref, gbt_ref,
                      proj_ref, gcol_ref, grow_ref, h_scr):
    j = pl.program_id(1)

    @pl.when(j == 0)
    def _():
        h = x_ref[...] * (1.0 + sc_ref[0]) + sh_ref[0]
        h_hi, h_lo = _split_bf16(h)
        h_scr[...] = h_hi
        nh = ML_HEADS
        g = _dot(h_hi, gwh_ref[...]) + (_dot(h_lo, gwh_ref[...]) + _dot(h_hi, gwl_ref[...])) + gb_ref[...]
        lane = lax.broadcasted_iota(I32, g.shape, 1)
        gcol_ref[...] = _gate_log(g, lane < nh)
        gt = (_dot_nt(gwth_ref[...], h_hi) + (_dot_nt(gwth_ref[...], h_lo) + _dot_nt(gwtl_ref[...], h_hi))
              + gbt_ref[...])
        sub = lax.broadcasted_iota(I32, gt.shape, 0)
        grow_ref[...] = _gate_log(gt, sub < nh)

    proj_ref[...] = _dot(h_scr[...], w_ref[...]).astype(proj_ref.dtype)


def _ml_inproj(x2d, sc, sh, w_in, b_gates, seq):
    n, d = x2d.shape
    ng = 2 * ML_HEADS
    wide = w_in.shape[1] - ng
    w = w_in[:, :wide].astype(BF16)
    gw = w_in[:, wide:]
    gw_hi, gw_lo = _split_bf16(jnp.pad(gw, ((0, 0), (0, LANES - ng))))
    gwt_hi, gwt_lo = _split_bf16(gw.T)
    gb = jnp.pad(b_gates, (0, LANES - ng)).reshape(1, LANES)
    gbt = b_gates.reshape(ng, 1)
    tm, tn = 1024, 1024
    tiles_per_seq = seq // tm
    row = lambda i, j: (i, 0)
    per_b = lambda i, j: (i // tiles_per_seq, 0, 0)
    const = lambda shape: pl.BlockSpec(shape, lambda i, j: (0,) * len(shape), pipeline_mode=pl.Buffered(1))
    return pl.pallas_call(
        _ml_inproj_kernel,
        grid=(n // tm, wide // tn),
        in_specs=[pl.BlockSpec((tm, d), row),
                  pl.BlockSpec((1, 1, d), per_b), pl.BlockSpec((1, 1, d), per_b),
                  pl.BlockSpec((d, tn), lambda i, j: (0, j)),
                  const(gw_hi.shape), const(gw_lo.shape), const(gb.shape),
                  const(gwt_hi.shape), const(gwt_lo.shape), const(gbt.shape)],
        out_specs=[pl.BlockSpec((tm, tn), lambda i, j: (i, j)),
                   pl.BlockSpec((tm, LANES), row),
                   pl.BlockSpec((ng, tm), lambda i, j: (0, i))],
        out_shape=[jax.ShapeDtypeStruct((n, wide), BF16), jax.ShapeDtypeStruct((n, LANES), F32),
                   jax.ShapeDtypeStruct((ng, n), F32)],
        scratch_shapes=[pltpu.VMEM((tm, d), BF16)],
        compiler_params=_cparams(("arbitrary", "arbitrary"), VMEM_LIMIT),
        name="mlstm_inproj",
    )(x2d, sc, sh, w, gw_hi, gw_lo, gb, gwt_hi, gwt_lo, gbt)


def _split3_bf16(x):
    a = x.astype(BF16)
    r = x - a.astype(F32)
    b = r.astype(BF16)
    c = (r - b.astype(F32)).astype(BF16)
    return a, b, c


def _mlstm_kernel(q_ref, k_ref, v_ref, og_ref, gc_ref, gr_ref, hn_ref, o_ref, c_scr, m_scr, *, L):
    ci = pl.program_id(2)

    @pl.when(ci == 0)
    def _():
        c_scr[...] = jnp.zeros(c_scr.shape, F32)
        m_scr[...] = jnp.zeros(m_scr.shape, F32)

    q = q_ref[...]
    k = k_ref[...]
    v = v_ref[...]
    ones_col = jnp.where(lax.broadcasted_iota(I32, (L, LANES), 1) == 0, 1.0, 0.0).astype(BF16)
    v_ext = jnp.concatenate([v, ones_col], axis=1)

    gc = gc_ref[0]
    gr = gr_ref[0]
    li_c, lf_c = gc[:, 0:1], gc[:, 1:2]
    li_r, lf_r = gr[0:1, :], gr[1:2, :]

    row = lax.broadcasted_iota(I32, (L, L), 0)
    col = lax.broadcasted_iota(I32, (L, L), 1)
    causal = col <= row
    tri = jnp.where(causal, 1.0, 0.0).astype(BF16)

    lfc3 = _split3_bf16(jnp.broadcast_to(lf_c, (L, LANES)))
    b_c = (_dot(tri, lfc3[0]) + (_dot(tri, lfc3[1]) + _dot(tri, lfc3[2])))[:, 0:1]
    lfr3 = _split3_bf16(jnp.broadcast_to(lf_r, (8, L)))
    b_r = (_dot_nt(lfr3[0], tri) + (_dot_nt(lfr3[1], tri) + _dot_nt(lfr3[2], tri)))[0:1, :]

    m_prev = m_scr[...]
    dm = jnp.where(causal, b_c - (b_r - li_r), NEG_BIG)
    inter = b_c + m_prev
    m_t = jnp.maximum(inter, jnp.max(dm, axis=1, keepdims=True))
    w_intra = jnp.exp(dm - m_t)
    w_inter = jnp.exp(inter - m_t)

    kscale = ML_QK ** -0.5
    qk = _dot_nt(q, k) * (w_intra * kscale)
    c_state = c_scr[...]
    num_ext = w_inter * _dot(q, c_state.astype(BF16)) + _dot(qk.astype(BF16), v_ext)
    vdim = v.shape[1]
    num = num_ext[:, :vdim]
    den = num_ext[:, vdim:vdim + 1]
    h = num / jnp.maximum(jnp.abs(den), jnp.exp(-m_t))
    hn = h * lax.rsqrt(jnp.mean(h * h, axis=-1, keepdims=True) + EPS) * hn_ref[...]
    og = og_ref[...].astype(F32)
    o_ref[...] = (hn / (1.0 + jnp.exp(-og))).astype(o_ref.dtype)

    b_end = b_r[:, L - 1:L]
    d_end_r = b_end - b_r + li_r
    m_new = jnp.maximum(b_end + m_prev, jnp.max(d_end_r, axis=1, keepdims=True))
    decay = jnp.exp(b_end + m_prev - m_new)
    w_s = jnp.exp(b_end - b_c + li_c - m_new) * kscale
    kw = (k.astype(F32) * w_s).astype(BF16)
    upd = lax.dot_general(kw, v_ext, (((0,), (0,)), ((), ())), preferred_element_type=F32)
    c_scr[...] = decay * c_state + upd
    m_scr[...] = m_new


def _mlstm_scan(proj, gcol, grow, head_norm, batch, seq):
    n = proj.shape[0]
    nh = ML_HEADS
    L = ML_CHUNK
    nc = seq // L
    gc = gcol[:, :2 * nh].reshape(n, 2, nh).transpose(2, 0, 1)
    gr = grow.reshape(2, nh, n).transpose(1, 0, 2)
    hn = head_norm.reshape(1, nh * ML_V)
    qk_blocks = nh * ML_QK // ML_QK
    v_off = 2 * nh * ML_QK // ML_V
    og_off = v_off + nh
    tok = lambda b, h, c: b * nc + c
    return pl.pallas_call(
        functools.partial(_mlstm_kernel, L=L),
        grid=(batch, nh, nc),
        in_specs=[pl.BlockSpec((L, ML_QK), lambda b, h, c: (tok(b, h, c), h)),
                  pl.BlockSpec((L, ML_QK), lambda b, h, c: (tok(b, h, c), qk_blocks + h)),
                  pl.BlockSpec((L, ML_V), lambda b, h, c: (tok(b, h, c), v_off + h)),
                  pl.BlockSpec((L, ML_V), lambda b, h, c: (tok(b, h, c), og_off + h)),
                  pl.BlockSpec((1, L, 2), lambda b, h, c: (h, tok(b, h, c), 0)),
                  pl.BlockSpec((1, 2, L), lambda b, h, c: (h, 0, tok(b, h, c))),
                  pl.BlockSpec((1, ML_V), lambda b, h, c: (0, h))],
        out_specs=pl.BlockSpec((L, ML_V), lambda b, h, c: (tok(b, h, c), h)),
        out_shape=jax.ShapeDtypeStruct((n, nh * ML_V), BF16),
        scratch_shapes=[pltpu.VMEM((ML_QK, ML_V + LANES), F32), pltpu.VMEM((1, 1), F32)],
        compiler_params=_cparams(("arbitrary", "arbitrary", "arbitrary"), VMEM_LIMIT),
        name="mlstm_scan",
    )(proj, proj, proj, proj, gc, gr, hn)


def _route_kernel(lg_ref, dest_ref, gate_ref, meta_ref, blk_ref, tri_scr, cnt_scr, run_scr, pst_scr,
                  *, T, E, K, BLK, NBP):
    ph = pl.program_id(0)
    t = pl.program_id(1)
    nt = pl.num_programs(1)

    @pl.when((ph == 0) & (t == 0))
    def _():
        r = lax.broadcasted_iota(I32, (T, T), 0)
        c = lax.broadcasted_iota(I32, (T, T), 1)
        tri_scr[...] = jnp.where(c < r, 1.0, 0.0).astype(BF16)
        cnt_scr[...] = jnp.zeros(cnt_scr.shape, F32)

    lane = lax.broadcasted_iota(I32, (T, LANES), 1)
    l = jnp.where(lane < E, lg_ref[...], -jnp.inf)
    vals, hots = [], []
    for _ in range(K):
        mx = jnp.max(l, axis=1, keepdims=True)
        idx = jnp.min(jnp.where(l == mx, lane, LANES), axis=1, keepdims=True)
        hot = lane == idx
        vals.append(mx)
        hots.append(hot)
        l = jnp.where(hot, -jnp.inf, l)
    hot_all = jnp.zeros((T, LANES), F32)
    for hot in hots:
        hot_all = hot_all + jnp.where(hot, 1.0, 0.0)
    colsum = jnp.sum(hot_all, axis=0, keepdims=True)

    @pl.when(ph == 0)
    def _():
        cnt_scr[...] = cnt_scr[...] + colsum

    @pl.when((ph == 1) & (t == 0))
    def _():
        cnt = cnt_scr[...].astype(I32)
        pc = ((cnt + (BLK - 1)) & (-BLK)).astype(F32)
        r = lax.broadcasted_iota(I32, (LANES, LANES), 0)
        c = lax.broadcasted_iota(I32, (LANES, LANES), 1)
        upper = jnp.where(r < c, 1.0, 0.0)
        pstart = jnp.dot(jnp.broadcast_to(pc, (8, LANES)), upper, precision=HIGHEST,
                         preferred_element_type=F32)[0:1, :]
        pst_scr[...] = pstart
        run_scr[...] = jnp.zeros(run_scr.shape, F32)
        pend = pstart + pc
        meta_ref[0:1, :] = cnt_scr[...].astype(I32)
        meta_ref[1:2, :] = pstart.astype(I32)
        meta_ref[2:3, :] = pend.astype(I32)
        meta_ref[3:8, :] = jnp.zeros((5, LANES), I32)
        jstart = (lax.broadcasted_iota(I32, (NBP, LANES), 0) * BLK).astype(F32)
        elane = lax.broadcasted_iota(I32, (NBP, LANES), 1)
        owned = jnp.where((pend <= jstart) & (elane < E), 1.0, 0.0)
        be = jnp.minimum(jnp.sum(owned, axis=1, keepdims=True), float(E - 1))
        blk_ref[...] = jnp.broadcast_to(be, (NBP, LANES)).astype(I32)

    @pl.when(ph == 1)
    def _():
        earlier = _dot(tri_scr[...], hot_all.astype(BF16))
        pos = earlier + (pst_scr[...] + run_scr[...])
        e0 = jnp.ones_like(vals[0])
        es = [e0] + [jnp.exp(v - vals[0]) for v in vals[1:]]
        tot = es[0]
        for e in es[1:]:
            tot = tot + e
        for kk in range(K):
            d = jnp.sum(jnp.where(hots[kk], pos, 0.0), axis=1, keepdims=True)
            dest_ref[:, kk:kk + 1] = d.astype(I32)
            gate_ref[:, kk:kk + 1] = es[kk] / tot
        run_scr[...] = run_scr[...] + colsum


def _route(logits, n_blocks):
    n = logits.shape[0]
    T = min(ROUTE_TILE, n)
    nbp = -(-n_blocks // 8) * 8
    kern = functools.partial(_route_kernel, T=T, E=N_EXPERTS, K=TOP_K, BLK=MOE_SLOT_BLOCK, NBP=nbp)
    tok_out = lambda ph, t: (t * ph, 0)
    fixed = lambda ph, t: (0, 0)
    return pl.pallas_call(
        kern,
        grid=(2, n // T),
        in_specs=[pl.BlockSpec((T, LANES), lambda ph, t: (t, 0))],
        out_specs=[pl.BlockSpec((T, TOP_K), tok_out), pl.BlockSpec((T, TOP_K), tok_out),
                   pl.BlockSpec((8, LANES), fixed), pl.BlockSpec((nbp, LANES), fixed)],
        out_shape=[jax.ShapeDtypeStruct((n, TOP_K), I32), jax.ShapeDtypeStruct((n, TOP_K), F32),
                   jax.ShapeDtypeStruct((8, LANES), I32), jax.ShapeDtypeStruct((nbp, LANES), I32)],
        scratch_shapes=[pltpu.VMEM((T, T), BF16), pltpu.VMEM((1, LANES), F32), pltpu.VMEM((1, LANES), F32),
                        pltpu.VMEM((1, LANES), F32)],
        compiler_params=_cparams(("arbitrary", "arbitrary")),
        name="moe_route",
    )(logits)


def _dispatch_kernel(meta_ref, dest_ref, h_ref, xs_ref, zero_scr, sem, zsem, *, T, K, E, BLK, NB):
    i = pl.program_id(0)
    last = pl.num_programs(0) - 1

    def row_copy(r, kk):
        d = dest_ref[r * K + kk]
        return pltpu.make_async_copy(h_ref.at[pl.ds(r, 1)], xs_ref.at[pl.ds(d, 1)], sem)

    def issue(r, carry):
        for kk in range(K):
            row_copy(r, kk).start(priority=kk % 2)
        return carry

    lax.fori_loop(0, T, issue, 0, unroll=DMA_UNROLL)

    @pl.when(i == last)
    def _():
        zero_scr[...] = jnp.zeros(zero_scr.shape, F32)

        def pad_copy(slot):
            return pltpu.make_async_copy(zero_scr.at[pl.ds(0, 1)], xs_ref.at[pl.ds(slot, 1)], zsem)

        for e in range(E):
            cnt = meta_ref[0, e]
            first = meta_ref[1, e] + cnt
            npad = meta_ref[2, e] - first

            def zissue(r, carry, first=first):
                pad_copy(first + r).start()
                return carry

            def zwait(r, carry, first=first):
                pad_copy(first + r).wait()
                return carry

            lax.fori_loop(0, npad, zissue, 0)
            lax.fori_loop(0, npad, zwait, 0)

        used = meta_ref[2, E - 1] // BLK

        def tail_copy(j):
            return pltpu.make_async_copy(zero_scr, xs_ref.at[pl.ds(pl.multiple_of(j * BLK, BLK), BLK)], zsem)

        def tissue(j, carry):
            tail_copy(j).start()
            return carry

        def twait(j, carry):
            tail_copy(j).wait()
            return carry

        lax.fori_loop(used, NB, tissue, 0)
        lax.fori_loop(used, NB, twait, 0)

    def drain(r, carry):
        for kk in range(K):
            row_copy(r, kk).wait()
        return carry

    lax.fori_loop(0, T, drain, 0, unroll=DMA_UNROLL)


def _dispatch(h2, dest_flat, meta, n_blocks):
    n, d = h2.shape
    T = ROW_TILE
    blk = MOE_SLOT_BLOCK
    kern = functools.partial(_dispatch_kernel, T=T, K=TOP_K, E=N_EXPERTS, BLK=blk, NB=n_blocks)
    gs = pltpu.PrefetchScalarGridSpec(
        num_scalar_prefetch=1,
        grid=(n // T,),
        in_specs=[pl.BlockSpec((T * TOP_K,), lambda i, m: (i,), memory_space=pltpu.SMEM),
                  pl.BlockSpec((T, d), lambda i, m: (i, 0))],
        out_specs=pl.BlockSpec(memory_space=pl.ANY),
        scratch_shapes=[pltpu.VMEM((blk, d), F32), pltpu.SemaphoreType.DMA(()), pltpu.SemaphoreType.DMA(())],
    )
    return pl.pallas_call(
        kern,
        grid_spec=gs,
        out_shape=jax.ShapeDtypeStruct((n_blocks * blk, d), F32),
        compiler_params=_cparams(("arbitrary",), VMEM_LIMIT),
        name="moe_dispatch",
    )(meta, dest_flat, h2)


def _expert_kernel(be_ref, used_ref, xs_ref, wgu_ref, bgu_ref, wd_ref, bd_ref, ys_ref):
    j = pl.program_id(0)

    @pl.when(j < used_ref[0])
    def _():
        x = xs_ref[...].astype(BF16)
        gu = _dot(x, wgu_ref[0]) + bgu_ref[0]
        de = gu.shape[1] // 2
        a = jnp.minimum(gu[:, :de], SWIGLU_LIMIT)
        u = jnp.clip(gu[:, de:], -SWIGLU_LIMIT, SWIGLU_LIMIT)
        act = (u + 1.0) * (a / (1.0 + jnp.exp(-SWIGLU_ALPHA * a)))
        ys_ref[...] = _dot(act.astype(BF16), wd_ref[0]) + bd_ref[0]

    @pl.when(j >= used_ref[0])
    def _():
        ys_ref[...] = jnp.zeros(ys_ref.shape, F32)


def _experts(xs, block_e, used, layer, w_gu_all, b_gu, w_down_all, b_down):
    p, d = xs.shape
    blk = MOE_SLOT_BLOCK
    _, ne, _, de2 = w_gu_all.shape
    nb = p // blk
    last_used = lambda j, be, used: jnp.minimum(j, jnp.maximum(used[0] - 1, 0))
    gs = pltpu.PrefetchScalarGridSpec(
        num_scalar_prefetch=2,
        grid=(nb,),
        in_specs=[pl.BlockSpec((blk, d), lambda j, be, used: (last_used(j, be, used), 0)),
                  pl.BlockSpec((None, 1, d, de2), lambda j, be, used: (layer, be[j], 0, 0)),
                  pl.BlockSpec((1, 1, de2), lambda j, be, used: (be[j], 0, 0)),
                  pl.BlockSpec((None, 1, de2 // 2, d), lambda j, be, used: (layer, be[j], 0, 0)),
                  pl.BlockSpec((1, 1, d), lambda j, be, used: (be[j], 0, 0))],
        out_specs=pl.BlockSpec((blk, d), lambda j, be, used: (j, 0)),
    )
    return pl.pallas_call(
        _expert_kernel,
        grid_spec=gs,
        out_shape=jax.ShapeDtypeStruct((p, d), F32),
        compiler_params=_cparams(("arbitrary",), VMEM_LIMIT),
        name="moe_experts",
    )(block_e, used, xs, w_gu_all, b_gu.reshape(ne, 1, de2), w_down_all, b_down.reshape(ne, 1, d))


def _combine_kernel(dest_ref, ys_ref, gate_ref, x_ref, g2_ref, lg_ref, lb_ref, o_ref, buf, sem, *, T, K):
    def row_copy(r, kk):
        d = dest_ref[r * K + kk]
        return pltpu.make_async_copy(ys_ref.at[pl.ds(d, 1)], buf.at[kk, pl.ds(r, 1)], sem)

    def issue(r, carry):
        for kk in range(K):
            row_copy(r, kk).start(priority=kk % 2)
        return carry

    def drain(r, carry):
        for kk in range(K):
            row_copy(r, kk).wait()
        return carry

    lax.fori_loop(0, T, issue, 0, unroll=DMA_UNROLL)
    lax.fori_loop(0, T, drain, 0, unroll=DMA_UNROLL)
    g = gate_ref[...]
    y = g[:, 0:1] * buf[0]
    for kk in range(1, K):
        y = y + g[:, kk:kk + 1] * buf[kk]
    z = ALPHA * x_ref[...] + (1.0 + g2_ref[0]) * y
    o_ref[...] = _layer_norm(z, lg_ref[...], lb_ref[...])


def _combine(ys, dest_flat, gates, x1, g2, ln_g, ln_b, seq):
    n, d = x1.shape
    T = ROW_TILE
    tiles_per_seq = seq // T
    row = lambda i: (i, 0)
    return pl.pallas_call(
        functools.partial(_combine_kernel, T=T, K=TOP_K),
        grid=(n // T,),
        in_specs=[pl.BlockSpec((T * TOP_K,), lambda i: (i,), memory_space=pltpu.SMEM),
                  pl.BlockSpec(memory_space=pl.ANY),
                  pl.BlockSpec((T, TOP_K), row), pl.BlockSpec((T, d), row),
                  pl.BlockSpec((1, 1, d), lambda i: (i // tiles_per_seq, 0, 0)),
                  _const_spec((1, d)), _const_spec((1, d))],
        out_specs=pl.BlockSpec((T, d), row),
        out_shape=jax.ShapeDtypeStruct((n, d), F32),
        scratch_shapes=[pltpu.VMEM((TOP_K, T, d), F32), pltpu.SemaphoreType.DMA(())],
        compiler_params=_cparams(("arbitrary",), VMEM_LIMIT),
        name="moe_combine",
    )(dest_flat, ys, gates, x1, g2, ln_g.reshape(1, d), ln_b.reshape(1, d))


def _moe(h2, logits, x1, g2, ln_g, ln_b, layer, w_gu_all, b_gu, w_down_all, b_down, seq):
    n = h2.shape[0]
    blk = MOE_SLOT_BLOCK
    n_blocks = -(-(n * TOP_K + N_EXPERTS * (blk - 1)) // blk)
    dest, gates, meta, block_e = _route(logits, n_blocks)
    dest_flat = dest.reshape(n * TOP_K)
    xs = _dispatch(h2, dest_flat, meta, n_blocks)
    used = (meta[2, N_EXPERTS - 1] // blk).reshape(1)
    ys = _experts(xs, block_e[:n_blocks, 0], used, layer, w_gu_all, b_gu, w_down_all, b_down)
    return _combine(ys, dest_flat, gates, x1, g2, ln_g, ln_b, seq)


def kernel(x, c, positions, ada_w, ada_b, ln_g, ln_b, mla_w_in, mla_q_norm, mla_w_uq, mla_kv_norm, mla_w_ukv,
           mla_w_out, ml_w_in, ml_b_gates, ml_head_norm, ml_w_out, moe_w_router, moe_b_router, moe_w_gu,
           moe_b_gu, moe_w_down, moe_b_down):
    batch, seq, d = x.shape
    n = batch * seq
    mods = _mods(c, ada_w, ada_b)
    xf = x.reshape(n, d)
    w_gu_bf = moe_w_gu.astype(BF16)
    w_down_bf = moe_w_down.astype(BF16)
    for i in range(DEPTH):
        j = i // 2
        sh1, sc1, g1, sh2, sc2, g2 = [mods[i, :, s * d:(s + 1) * d].reshape(batch, 1, d) for s in range(6)]
        if i % 2 == 0:
            cos, sin = _rope_tables(positions)
            q, k, v = _mla_front(xf, sc1, sh1, cos, sin, mla_w_in[j], mla_q_norm[j], mla_w_uq[j],
                                 mla_kv_norm[j], mla_w_ukv[j], batch, seq)
            a = _flash_attention(q, k, v).reshape(n, MLA_HEADS * MLA_V)
            w_out = mla_w_out[j]
        else:
            proj, gcol, grow = _ml_inproj(xf, sc1, sh1, ml_w_in[j], ml_b_gates[j], seq)
            a = _mlstm_scan(proj, gcol, grow, ml_head_norm[j], batch, seq)
            w_out = ml_w_out[j]
        x1, h2, logits = _outproj_ln_router(a, xf, w_out, g1, ln_g[i, 0], ln_b[i, 0], sc2, sh2,
                                            moe_w_router[i], moe_b_router[i], seq)
        xf = _moe(h2, logits, x1, g2, ln_g[i, 1], ln_b[i, 1], i, w_gu_bf, moe_b_gu[i], w_down_bf,
                  moe_b_down[i], seq)
    return xf.reshape(batch, seq, d)
```

```python
import functools
import math

import jax
import jax.numpy as jnp
from jax import lax
from jax.experimental import pallas as pl
from jax.experimental.pallas import tpu as pltpu

F32 = jnp.float32
BF16 = jnp.bfloat16
I32 = jnp.int32
HIGHEST = lax.Precision.HIGHEST

CHUNK = 64
MLA_HEADS = 16
MLA_NOPE = 128
MLA_ROPE = 64
MLA_V = 128
MLA_Q_RANK = 448
MLA_KV_RANK = 512
ROPE_THETA = 10000.0
ML_HEADS = 4
ML_QK = 256
ML_V = 512
GATE_SOFTCAP = 15.0
N_EXPERTS = 32
TOP_K = 4
D_EXPERT = 1024
SWIGLU_LIMIT = 7.0
SWIGLU_ALPHA = 1.702
DEPTH = 2
ALPHA = (2 * DEPTH) ** 0.25
EPS = 1e-6
NEG_BIG = -1e30

LANES = 128
VMEM_LIMIT = 56 * 1024 * 1024

Q_RANK_PAD = 512
ATT_BLOCK = 512
ATT_CHAINS = 4
ML_CHUNK = 256
ML_HEADS_PER_STEP = 4
MOE_SLOT_BLOCK = 512
ROUTE_TILE = 1024
ROW_TILE = 256
OUTPROJ_TILE = 512
DMA_UNROLL = 8


def _cparams(sem, vmem=None):
    return pltpu.CompilerParams(dimension_semantics=sem, vmem_limit_bytes=vmem)


def _const_spec(shape):
    nd = len(shape)
    return pl.BlockSpec(shape, lambda *_: (0,) * nd, pipeline_mode=pl.Buffered(1))


def _split_bf16(x):
    hi = x.astype(BF16)
    lo = (x - hi.astype(F32)).astype(BF16)
    return hi, lo


def _dot(a, b):
    return jnp.dot(a, b, preferred_element_type=F32)


def _dot_nt(a, b):
    return lax.dot_general(a, b, (((1,), (1,)), ((), ())), preferred_element_type=F32)


def _dot3(x, w_hi, w_lo):
    x_hi, x_lo = _split_bf16(x)
    return _dot(x_hi, w_hi) + (_dot(x_lo, w_hi) + _dot(x_hi, w_lo))


def _layer_norm(z, g, b):
    mu = jnp.mean(z, axis=-1, keepdims=True)
    zc = z - mu
    var = jnp.mean(zc * zc, axis=-1, keepdims=True)
    return zc * lax.rsqrt(var + EPS) * g + b


def _mods_kernel(c_ref, w_ref, b_ref, o_ref):
    c = c_ref[...]
    ca = c / (1.0 + jnp.exp(-c))
    o_ref[0] = jnp.dot(ca, w_ref[0], precision=HIGHEST, preferred_element_type=F32) + b_ref[0]


def _mods(c, ada_w, ada_b):
    depth, d, d6 = ada_w.shape
    b = c.shape[0]
    rows = 8
    c8 = jnp.pad(c, ((0, rows - b), (0, 0)))
    tn = 1024
    out = pl.pallas_call(
        _mods_kernel,
        grid=(depth, d6 // tn),
        in_specs=[pl.BlockSpec((rows, d), lambda i, j: (0, 0)),
                  pl.BlockSpec((1, d, tn), lambda i, j: (i, 0, j)),
                  pl.BlockSpec((1, 1, tn), lambda i, j: (i, 0, j))],
        out_specs=pl.BlockSpec((1, rows, tn), lambda i, j: (i, 0, j)),
        out_shape=jax.ShapeDtypeStruct((depth, rows, d6), F32),
        compiler_params=_cparams(("arbitrary", "arbitrary"), VMEM_LIMIT),
        name="adaln_mods",
    )(c8, ada_w, ada_b.reshape(depth, 1, d6))
    return out[:, :b]


def _rope_kernel(pos_ref, inv_ref, cos_ref, sin_ref):
    ang = pos_ref[...].astype(F32) * inv_ref[...]
    cos_ref[...] = jnp.cos(ang)
    sin_ref[...] = jnp.sin(ang)


def _rope_tables(positions):
    n = positions.size
    half = MLA_ROPE // 2
    inv = ROPE_THETA ** (-jnp.arange(0, MLA_ROPE, 2, dtype=F32) / MLA_ROPE)
    inv_row = jnp.tile(inv, LANES // half).reshape(1, LANES)
    pos = jnp.broadcast_to(positions.reshape(n, 1), (n, LANES))
    tm = 1024
    spec = pl.BlockSpec((tm, LANES), lambda i: (i, 0))
    return pl.pallas_call(
        _rope_kernel,
        grid=(n // tm,),
        in_specs=[spec, pl.BlockSpec((1, LANES), lambda i: (0, 0))],
        out_specs=[spec, spec],
        out_shape=[jax.ShapeDtypeStruct((n, LANES), F32)] * 2,
        compiler_params=_cparams(("arbitrary",)),
        name="rope_tables",
    )(pos, inv_row)


def _mla_front_kernel(x_ref, sc_ref, sh_ref, cos_ref, sin_ref, win_ref, qn_ref, kvn_ref, wq_ref, wkv_ref,
                      q_ref, k_ref, v_ref, *, qscale):
    h = (x_ref[...] * (1.0 + sc_ref[0]) + sh_ref[0]).astype(BF16)
    proj = _dot(h, win_ref[...])
    cq = proj[:, :Q_RANK_PAD]
    ckv = proj[:, Q_RANK_PAD:Q_RANK_PAD + MLA_KV_RANK]
    cq = cq * lax.rsqrt(jnp.sum(cq * cq, -1, keepdims=True) * (1.0 / MLA_Q_RANK) + EPS) * qn_ref[...]
    ckv = ckv * lax.rsqrt(jnp.mean(ckv * ckv, -1, keepdims=True) + EPS) * kvn_ref[...]
    cq = cq.astype(BF16)
    ckv = ckv.astype(BF16)
    cos = cos_ref[...]
    sin = sin_ref[...]
    o = Q_RANK_PAD + MLA_KV_RANK
    kr = proj[:, o:o + LANES] * cos + proj[:, o + LANES:o + 2 * LANES] * sin
    lane = lax.broadcasted_iota(I32, kr.shape, 1)
    kr_even = jnp.where(lane < MLA_ROPE, kr, 0.0).astype(BF16)
    kr_odd = jnp.where(lane >= MLA_ROPE, kr, 0.0).astype(BF16)
    ones_col = jnp.where(lane == 0, 1.0, 0.0).astype(BF16)

    nheads = MLA_HEADS
    nope_w = nheads * MLA_NOPE
    rope_w = nheads * MLA_ROPE
    group = 4
    for g in range(nheads // group):
        qn = _dot(cq, wq_ref[:, g * group * MLA_NOPE:(g + 1) * group * MLA_NOPE]) * qscale
        kv = _dot(ckv, wkv_ref[:, g * group * 2 * LANES:(g + 1) * group * 2 * LANES])
        for j in range(group):
            hh = g * group + j
            q_ref[0, hh, :, :LANES] = qn[:, j * LANES:(j + 1) * LANES].astype(BF16)
            k_ref[0, hh, :, :LANES] = kv[:, 2 * j * LANES:(2 * j + 1) * LANES].astype(BF16)
            v_ref[0, hh, :, :LANES] = kv[:, (2 * j + 1) * LANES:(2 * j + 2) * LANES].astype(BF16)
            k_ref[0, hh, :, LANES:] = kr_even if hh % 2 == 0 else kr_odd
            v_ref[0, hh, :, LANES:] = ones_col
    for p in range(nheads // 2):
        lo = nope_w + p * LANES
        qr = (_dot(cq, wq_ref[:, lo:lo + LANES]) * cos
              + _dot(cq, wq_ref[:, rope_w + lo:rope_w + lo + LANES]) * sin) * qscale
        qr = qr.astype(BF16)
        q_ref[0, 2 * p, :, LANES:] = qr
        q_ref[0, 2 * p + 1, :, LANES:] = qr


def _mla_front(x2d, sc, sh, cos, sin, w_in, q_norm, w_uq, kv_norm, w_ukv, batch, seq):
    n, d = x2d.shape
    nh = MLA_HEADS
    half = MLA_ROPE // 2
    wq_lat = jnp.pad(w_in[:, :MLA_Q_RANK], ((0, 0), (0, Q_RANK_PAD - MLA_Q_RANK)))
    wkv_lat = w_in[:, MLA_Q_RANK:MLA_Q_RANK + MLA_KV_RANK]
    wkr = w_in[:, MLA_Q_RANK + MLA_KV_RANK:]
    wkr_rot = jnp.concatenate([-wkr[:, half:], wkr[:, :half]], 1)
    win = jnp.concatenate([wq_lat, wkv_lat, wkr, wkr, wkr_rot, wkr_rot], 1).astype(BF16)
    wq3 = w_uq.reshape(MLA_Q_RANK, nh, MLA_NOPE + MLA_ROPE)
    wq_nope = wq3[:, :, :MLA_NOPE].reshape(MLA_Q_RANK, nh * MLA_NOPE)
    wq_r = wq3[:, :, MLA_NOPE:]
    wq_rope = wq_r.reshape(MLA_Q_RANK, nh * MLA_ROPE)
    wq_rot = jnp.concatenate([-wq_r[:, :, half:], wq_r[:, :, :half]], -1).reshape(MLA_Q_RANK, nh * MLA_ROPE)
    wq = jnp.pad(jnp.concatenate([wq_nope, wq_rope, wq_rot], 1),
                 ((0, Q_RANK_PAD - MLA_Q_RANK), (0, 0))).astype(BF16)
    wkv = w_ukv.astype(BF16)
    qn = jnp.pad(q_norm, (0, Q_RANK_PAD - MLA_Q_RANK)).reshape(1, Q_RANK_PAD)
    kvn = kv_norm.reshape(1, MLA_KV_RANK)
    qscale = (MLA_NOPE + MLA_ROPE) ** -0.5 * math.log2(math.e)

    tm = ROW_TILE
    tiles_per_seq = seq // tm
    row = lambda i: (i, 0)
    per_b = lambda i: (i // tiles_per_seq, 0, 0)
    head_out = pl.BlockSpec((1, nh, tm, 2 * LANES), lambda i: (i // tiles_per_seq, 0, i % tiles_per_seq, 0))
    out_sds = jax.ShapeDtypeStruct((batch, nh, seq, 2 * LANES), BF16)
    return pl.pallas_call(
        functools.partial(_mla_front_kernel, qscale=qscale),
        grid=(n // tm,),
        in_specs=[pl.BlockSpec((tm, d), row),
                  pl.BlockSpec((1, 1, d), per_b), pl.BlockSpec((1, 1, d), per_b),
                  pl.BlockSpec((tm, LANES), row), pl.BlockSpec((tm, LANES), row),
                  _const_spec(win.shape), _const_spec(qn.shape), _const_spec(kvn.shape),
                  _const_spec(wq.shape), _const_spec(wkv.shape)],
        out_specs=[head_out, head_out, head_out],
        out_shape=[out_sds, out_sds, out_sds],
        compiler_params=_cparams(("arbitrary",), VMEM_LIMIT),
        name="mla_front",
    )(x2d, sc, sh, cos, sin, win, qn, kvn, wq, wkv)


def _flash_kernel(q_ref, k_ref, v_ref, o_ref, m_ref, acc_ref, a0_ref, a1_ref, s0_ref, s1_ref, p0_ref, p1_ref,
                  *, blk, nsub):
    i = pl.program_id(2)
    s_bufs, p_bufs, a_bufs = (s0_ref, s1_ref), (p0_ref, p1_ref), (a0_ref, a1_ref)
    slabs = blk // LANES
    shift = CHUNK.bit_length() - 1

    m_ref[...] = jnp.full(m_ref.shape, NEG_BIG, F32)
    acc_ref[...] = jnp.zeros(acc_ref.shape, F32)
    s1_ref[...] = jnp.full(s1_ref.shape, -jnp.inf, F32)
    p0_ref[...] = jnp.zeros(p0_ref.shape, BF16)
    a0_ref[...] = jnp.ones(a0_ref.shape, F32)

    def stage_s(c, kb, par, masked):
        start = pl.multiple_of(kb * blk, blk)
        q = q_ref[0, 0, c * blk:(c + 1) * blk, :]
        k = k_ref[0, 0, pl.ds(start, blk), :]
        s = _dot_nt(q, k)
        if masked:
            r = lax.shift_right_logical(lax.broadcasted_iota(I32, s.shape, 0), shift)
            cc = lax.shift_right_logical(lax.broadcasted_iota(I32, s.shape, 1), shift)
            s = jnp.where(cc <= r, s, NEG_BIG)
        s_bufs[par][c] = s

    def stage_m(c, par):
        s = s_bufs[par][c]
        m_prev = m_ref[c]
        m_new = jnp.maximum(m_prev, jnp.max(s, axis=1, keepdims=True))
        for j in range(slabs):
            p_bufs[par][c, :, j * LANES:(j + 1) * LANES] = jnp.exp2(
                s[:, j * LANES:(j + 1) * LANES] - m_new).astype(BF16)
        a_bufs[par][c] = jnp.exp2(m_prev - m_new)
        m_ref[c] = m_new

    def stage_f(c, kb, par):
        start = pl.multiple_of(jnp.maximum(kb, 0) * blk, blk)
        v = v_ref[0, 0, pl.ds(start, blk), :]
        pv = _dot(p_bufs[par][c], v)
        alpha = a_bufs[par][c]
        acc_ref[c, :, :LANES] = alpha * acc_ref[c, :, :LANES] + pv[:, :LANES]
        acc_ref[c, :, LANES:] = alpha * acc_ref[c, :, LANES:] + pv[:, LANES:]

    def event(kb, par, s_chains, m_chains, f_chains, masked_chain=None):
        for c in f_chains:
            stage_f(c, kb - 2, par)
        for c in m_chains:
            stage_m(c, 1 - par)
        for c in s_chains:
            stage_s(c, kb, par, masked=(c == masked_chain))

    everyone = range(nsub)

    def two_events(kb2, carry):
        event(2 * kb2, 0, everyone, everyone, everyone)
        event(2 * kb2 + 1, 1, everyone, everyone, everyone)
        return carry

    first_diag = nsub * i
    lax.fori_loop(0, (nsub // 2) * i, two_events, 0)
    for t in range(nsub + 2):
        event(first_diag + t, t % 2,
              [c for c in everyone if c >= t],
              [c for c in everyone if c >= t - 1],
              [c for c in everyone if c >= t - 2],
              masked_chain=t)
    for c in everyone:
        l = acc_ref[c, :, LANES:LANES + 1]
        o_ref[0, c * blk:(c + 1) * blk, :] = (acc_ref[c, :, :MLA_V] / l).astype(o_ref.dtype)


def _flash_attention(q, k, v):
    batch, nh, seq, dk = q.shape
    blk = ATT_BLOCK
    nsub = ATT_CHAINS
    assert nsub % 2 == 0
    qspec = pl.BlockSpec((1, 1, nsub * blk, dk), lambda b, h, i: (b, h, i, 0))
    kspec = pl.BlockSpec((1, 1, seq, dk), lambda b, h, i: (b, h, 0, 0), pipeline_mode=pl.Buffered(1))
    vspec = pl.BlockSpec((1, 1, seq, dk), lambda b, h, i: (b, h, 0, 0))
    stat = pltpu.VMEM((nsub, blk, LANES), F32)
    return pl.pallas_call(
        functools.partial(_flash_kernel, blk=blk, nsub=nsub),
        grid=(batch, nh, seq // (nsub * blk)),
        in_specs=[qspec, kspec, vspec],
        out_specs=pl.BlockSpec((1, nsub * blk, MLA_V), lambda b, h, i: (b, i, h)),
        out_shape=jax.ShapeDtypeStruct((batch, seq, nh * MLA_V), BF16),
        scratch_shapes=[stat, pltpu.VMEM((nsub, blk, dk), F32), stat, stat,
                        pltpu.VMEM((nsub, blk, blk), F32), pltpu.VMEM((nsub, blk, blk), F32),
                        pltpu.VMEM((nsub, blk, blk), BF16), pltpu.VMEM((nsub, blk, blk), BF16)],
        compiler_params=_cparams(("arbitrary", "arbitrary", "arbitrary"), VMEM_LIMIT),
        name="flash_attention",
    )(q, k, v)


def _outproj_kernel(a_ref, x_ref, g1_ref, lg_ref, lb_ref, sc_ref, sh_ref, w_ref, rwh_ref, rwl_ref, rb_ref,
                    x1_ref, h2_ref, logit_ref):
    y = _dot(a_ref[...], w_ref[...])
    z = ALPHA * x_ref[...] + (1.0 + g1_ref[0]) * y
    x1 = _layer_norm(z, lg_ref[...], lb_ref[...])
    x1_ref[...] = x1
    h2 = x1 * (1.0 + sc_ref[0]) + sh_ref[0]
    h2_ref[...] = h2
    logit_ref[...] = _dot3(h2, rwh_ref[...], rwl_ref[...]) + rb_ref[...]


def _outproj_ln_router(a2d, x2d, w_out, g1, ln_g, ln_b, sc2, sh2, w_router, b_router, seq):
    n, d = x2d.shape
    da = a2d.shape[1]
    ne = w_router.shape[1]
    w = w_out.astype(BF16)
    rw = jnp.pad(w_router, ((0, 0), (0, LANES - ne)))
    rw_hi, rw_lo = _split_bf16(rw)
    rb = jnp.pad(b_router, (0, LANES - ne), constant_values=NEG_BIG).reshape(1, LANES)
    tm = OUTPROJ_TILE
    tiles_per_seq = seq // tm
    row = lambda i: (i, 0)
    per_b = lambda i: (i // tiles_per_seq, 0, 0)
    vec = pl.BlockSpec((1, 1, d), per_b)
    return pl.pallas_call(
        _outproj_kernel,
        grid=(n // tm,),
        in_specs=[pl.BlockSpec((tm, da), row), pl.BlockSpec((tm, d), row), vec,
                  _const_spec((1, d)), _const_spec((1, d)), vec, vec,
                  _const_spec(w.shape), _const_spec(rw_hi.shape), _const_spec(rw_lo.shape),
                  _const_spec(rb.shape)],
        out_specs=[pl.BlockSpec((tm, d), row), pl.BlockSpec((tm, d), row), pl.BlockSpec((tm, LANES), row)],
        out_shape=[jax.ShapeDtypeStruct((n, d), F32), jax.ShapeDtypeStruct((n, d), F32),
                   jax.ShapeDtypeStruct((n, LANES), F32)],
        compiler_params=_cparams(("arbitrary",), VMEM_LIMIT),
        name="outproj_ln_router",
    )(a2d, x2d, g1, ln_g.reshape(1, d), ln_b.reshape(1, d), sc2, sh2, w, rw_hi, rw_lo, rb)


def _gate_log(g, is_input_gate):
    g = GATE_SOFTCAP * jnp.tanh(g * (1.0 / GATE_SOFTCAP))
    log_f = jnp.minimum(g, 0.0) - jnp.log(1.0 + jnp.exp(-jnp.abs(g)))
    return jnp.where(is_input_gate, g, log_f)


def _ml_inproj_kernel(x_ref, sc_ref, sh_ref, w_ref, gwh_ref, gwl_ref, gb_ref, gwth_ref, gwtl_ref, gbt_ref,
                      proj_ref, gcol_ref, grow_ref, h_scr):
    j = pl.program_id(1)

    @pl.when(j == 0)
    def _():
        h = x_ref[...] * (1.0 + sc_ref[0]) + sh_ref[0]
        h_hi, h_lo = _split_bf16(h)
        h_scr[...] = h_hi
        nh = ML_HEADS
        g = _dot(h_hi, gwh_ref[...]) + (_dot(h_lo, gwh_ref[...]) + _dot(h_hi, gwl_ref[...])) + gb_ref[...]
        lane = lax.broadcasted_iota(I32, g.shape, 1)
        gcol_ref[...] = _gate_log(g, lane < nh)
        gt = (_dot_nt(gwth_ref[...], h_hi) + (_dot_nt(gwth_ref[...], h_lo) + _dot_nt(gwtl_ref[...], h_hi))
              + gbt_ref[...])
        sub = lax.broadcasted_iota(I32, gt.shape, 0)
        grow_ref[...] = _gate_log(gt, sub < nh)

    proj_ref[...] = _dot(h_scr[...], w_ref[...]).astype(proj_ref.dtype)


def _ml_inproj(x2d, sc, sh, w_in, b_gates, seq):
    n, d = x2d.shape
    ng = 2 * ML_HEADS
    wide = w_in.shape[1] - ng
    w = w_in[:, :wide].astype(BF16)
    gw = w_in[:, wide:]
    gw_hi, gw_lo = _split_bf16(jnp.pad(gw, ((0, 0), (0, LANES - ng))))
    gwt_hi, gwt_lo = _split_bf16(gw.T)
    gb = jnp.pad(b_gates, (0, LANES - ng)).reshape(1, LANES)
    gbt = b_gates.reshape(ng, 1)
    tm, tn = 1024, 1024
    tiles_per_seq = seq // tm
    row = lambda i, j: (i, 0)
    per_b = lambda i, j: (i // tiles_per_seq, 0, 0)
    const = lambda shape: pl.BlockSpec(shape, lambda i, j: (0,) * len(shape), pipeline_mode=pl.Buffered(1))
    return pl.pallas_call(
        _ml_inproj_kernel,
        grid=(n // tm, wide // tn),
        in_specs=[pl.BlockSpec((tm, d), row),
                  pl.BlockSpec((1, 1, d), per_b), pl.BlockSpec((1, 1, d), per_b),
                  pl.BlockSpec((d, tn), lambda i, j: (0, j)),
                  const(gw_hi.shape), const(gw_lo.shape), const(gb.shape),
                  const(gwt_hi.shape), const(gwt_lo.shape), const(gbt.shape)],
        out_specs=[pl.BlockSpec((tm, tn), lambda i, j: (i, j)),
                   pl.BlockSpec((tm, LANES), row),
                   pl.BlockSpec((ng, tm), lambda i, j: (0, i))],
        out_shape=[jax.ShapeDtypeStruct((n, wide), BF16), jax.ShapeDtypeStruct((n, LANES), F32),
                   jax.ShapeDtypeStruct((ng, n), F32)],
        scratch_shapes=[pltpu.VMEM((tm, d), BF16)],
        compiler_params=_cparams(("arbitrary", "arbitrary"), VMEM_LIMIT),
        name="mlstm_inproj",
    )(x2d, sc, sh, w, gw_hi, gw_lo, gb, gwt_hi, gwt_lo, gbt)


def _split3_bf16(x):
    a = x.astype(BF16)
    r = x - a.astype(F32)
    b = r.astype(BF16)
    c = (r - b.astype(F32)).astype(BF16)
    return a, b, c


def _mlstm_kernel(q_ref, k_ref, v_ref, og_ref, gc_ref, gr_ref, hn_ref, o_ref, c_scr, m_scr, *, L, G):
    ci = pl.program_id(2)

    @pl.when(ci == 0)
    def _():
        c_scr[...] = jnp.zeros(c_scr.shape, F32)
        m_scr[...] = jnp.zeros(m_scr.shape, F32)

    row = lax.broadcasted_iota(I32, (L, L), 0)
    col = lax.broadcasted_iota(I32, (L, L), 1)
    causal = col <= row
    tri = jnp.where(causal, 1.0, 0.0).astype(BF16)
    ones_col = jnp.where(lax.broadcasted_iota(I32, (L, LANES), 1) == 0, 1.0, 0.0).astype(BF16)
    for g in range(G):
        _mlstm_head(q_ref[:, g * ML_QK:(g + 1) * ML_QK], k_ref[:, g * ML_QK:(g + 1) * ML_QK],
                    v_ref[:, g * ML_V:(g + 1) * ML_V], og_ref[:, g * ML_V:(g + 1) * ML_V],
                    gc_ref[g], gr_ref[g], hn_ref[:, g * ML_V:(g + 1) * ML_V],
                    o_ref.at[:, g * ML_V:(g + 1) * ML_V], c_scr.at[g], m_scr.at[g],
                    causal, tri, ones_col, L)


def _mlstm_head(q, k, v, og, gc, gr, head_gain, o_ref, c_scr, m_scr, causal, tri, ones_col, L):
    v_ext = jnp.concatenate([v, ones_col], axis=1)
    li_c, lf_c = gc[:, 0:1], gc[:, 1:2]
    li_r, lf_r = gr[0:1, :], gr[1:2, :]

    lfc3 = _split3_bf16(jnp.broadcast_to(lf_c, (L, LANES)))
    b_c = (_dot(tri, lfc3[0]) + (_dot(tri, lfc3[1]) + _dot(tri, lfc3[2])))[:, 0:1]
    lfr3 = _split3_bf16(jnp.broadcast_to(lf_r, (8, L)))
    b_r = (_dot_nt(lfr3[0], tri) + (_dot_nt(lfr3[1], tri) + _dot_nt(lfr3[2], tri)))[0:1, :]

    m_prev = m_scr[...]
    dm = jnp.where(causal, b_c - (b_r - li_r), NEG_BIG)
    inter = b_c + m_prev
    m_t = jnp.maximum(inter, jnp.max(dm, axis=1, keepdims=True))
    w_intra = jnp.exp(dm - m_t)
    w_inter = jnp.exp(inter - m_t)

    kscale = ML_QK ** -0.5
    qk = _dot_nt(q, k) * (w_intra * kscale)
    c_state = c_scr[...]
    num_ext = w_inter * _dot(q, c_state.astype(BF16)) + _dot(qk.astype(BF16), v_ext)
    vdim = v.shape[1]
    num = num_ext[:, :vdim]
    den = num_ext[:, vdim:vdim + 1]
    h = num / jnp.maximum(jnp.abs(den), jnp.exp(-m_t))
    hn = h * lax.rsqrt(jnp.mean(h * h, axis=-1, keepdims=True) + EPS) * head_gain
    o_ref[...] = (hn / (1.0 + jnp.exp(-og.astype(F32)))).astype(o_ref.dtype)

    b_end = b_r[:, L - 1:L]
    d_end_r = b_end - b_r + li_r
    m_new = jnp.maximum(b_end + m_prev, jnp.max(d_end_r, axis=1, keepdims=True))
    decay = jnp.exp(b_end + m_prev - m_new)
    w_s = jnp.exp(b_end - b_c + li_c - m_new) * kscale
    kw = (k.astype(F32) * w_s).astype(BF16)
    upd = lax.dot_general(kw, v_ext, (((0,), (0,)), ((), ())), preferred_element_type=F32)
    c_scr[...] = decay * c_state + upd
    m_scr[...] = m_new


def _mlstm_scan(proj, gcol, grow, head_norm, batch, seq):
    n = proj.shape[0]
    nh = ML_HEADS
    L = ML_CHUNK
    nc = seq // L
    gc = gcol[:, :2 * nh].reshape(n, 2, nh).transpose(2, 0, 1)
    gr = grow.reshape(2, nh, n).transpose(1, 0, 2)
    hn = head_norm.reshape(1, nh * ML_V)
    G = ML_HEADS_PER_STEP
    groups = nh // G
    k_off = groups
    v_off = 2 * nh * ML_QK // (G * ML_V)
    og_off = v_off + groups
    tok = lambda b, h, c: b * nc + c
    return pl.pallas_call(
        functools.partial(_mlstm_kernel, L=L, G=G),
        grid=(batch, groups, nc),
        in_specs=[pl.BlockSpec((L, G * ML_QK), lambda b, h, c: (tok(b, h, c), h)),
                  pl.BlockSpec((L, G * ML_QK), lambda b, h, c: (tok(b, h, c), k_off + h)),
                  pl.BlockSpec((L, G * ML_V), lambda b, h, c: (tok(b, h, c), v_off + h)),
                  pl.BlockSpec((L, G * ML_V), lambda b, h, c: (tok(b, h, c), og_off + h)),
                  pl.BlockSpec((G, L, 2), lambda b, h, c: (h, tok(b, h, c), 0)),
                  pl.BlockSpec((G, 2, L), lambda b, h, c: (h, 0, tok(b, h, c))),
                  pl.BlockSpec((1, G * ML_V), lambda b, h, c: (0, h))],
        out_specs=pl.BlockSpec((L, G * ML_V), lambda b, h, c: (tok(b, h, c), h)),
        out_shape=jax.ShapeDtypeStruct((n, nh * ML_V), BF16),
        scratch_shapes=[pltpu.VMEM((G, ML_QK, ML_V + LANES), F32), pltpu.VMEM((G, 1, 1), F32)],
        compiler_params=_cparams(("arbitrary", "arbitrary", "arbitrary"), VMEM_LIMIT),
        name="mlstm_scan",
    )(proj, proj, proj, proj, gc, gr, hn)


def _route_kernel(lg_ref, dest_ref, gate_ref, meta_ref, blk_ref, tri_scr, cnt_scr, run_scr, pst_scr,
                  *, T, E, K, BLK, NBP):
    ph = pl.program_id(0)
    t = pl.program_id(1)
    nt = pl.num_programs(1)

    @pl.when((ph == 0) & (t == 0))
    def _():
        r = lax.broadcasted_iota(I32, (T, T), 0)
        c = lax.broadcasted_iota(I32, (T, T), 1)
        tri_scr[...] = jnp.where(c < r, 1.0, 0.0).astype(BF16)
        cnt_scr[...] = jnp.zeros(cnt_scr.shape, F32)

    lane = lax.broadcasted_iota(I32, (T, LANES), 1)
    l = jnp.where(lane < E, lg_ref[...], -jnp.inf)
    vals, hots = [], []
    for _ in range(K):
        mx = jnp.max(l, axis=1, keepdims=True)
        idx = jnp.min(jnp.where(l == mx, lane, LANES), axis=1, keepdims=True)
        hot = lane == idx
        vals.append(mx)
        hots.append(hot)
        l = jnp.where(hot, -jnp.inf, l)
    hot_all = jnp.zeros((T, LANES), F32)
    for hot in hots:
        hot_all = hot_all + jnp.where(hot, 1.0, 0.0)
    colsum = jnp.sum(hot_all, axis=0, keepdims=True)

    @pl.when(ph == 0)
    def _():
        cnt_scr[...] = cnt_scr[...] + colsum

    @pl.when((ph == 1) & (t == 0))
    def _():
        cnt = cnt_scr[...].astype(I32)
        pc = ((cnt + (BLK - 1)) & (-BLK)).astype(F32)
        r = lax.broadcasted_iota(I32, (LANES, LANES), 0)
        c = lax.broadcasted_iota(I32, (LANES, LANES), 1)
        upper = jnp.where(r < c, 1.0, 0.0)
        pstart = jnp.dot(jnp.broadcast_to(pc, (8, LANES)), upper, precision=HIGHEST,
                         preferred_element_type=F32)[0:1, :]
        pst_scr[...] = pstart
        run_scr[...] = jnp.zeros(run_scr.shape, F32)
        pend = pstart + pc
        meta_ref[0:1, :] = cnt_scr[...].astype(I32)
        meta_ref[1:2, :] = pstart.astype(I32)
        meta_ref[2:3, :] = pend.astype(I32)
        meta_ref[3:8, :] = jnp.zeros((5, LANES), I32)
        jstart = (lax.broadcasted_iota(I32, (NBP, LANES), 0) * BLK).astype(F32)
        elane = lax.broadcasted_iota(I32, (NBP, LANES), 1)
        owned = jnp.where((pend <= jstart) & (elane < E), 1.0, 0.0)
        be = jnp.minimum(jnp.sum(owned, axis=1, keepdims=True), float(E - 1))
        blk_ref[...] = jnp.broadcast_to(be, (NBP, LANES)).astype(I32)

    @pl.when(ph == 1)
    def _():
        earlier = _dot(tri_scr[...], hot_all.astype(BF16))
        pos = earlier + (pst_scr[...] + run_scr[...])
        e0 = jnp.ones_like(vals[0])
        es = [e0] + [jnp.exp(v - vals[0]) for v in vals[1:]]
        tot = es[0]
        for e in es[1:]:
            tot = tot + e
        for kk in range(K):
            d = jnp.sum(jnp.where(hots[kk], pos, 0.0), axis=1, keepdims=True)
            dest_ref[:, kk:kk + 1] = d.astype(I32)
            gate_ref[:, kk:kk + 1] = es[kk] / tot
        run_scr[...] = run_scr[...] + colsum


def _route(logits, n_blocks):
    n = logits.shape[0]
    T = min(ROUTE_TILE, n)
    nbp = -(-n_blocks // 8) * 8
    kern = functools.partial(_route_kernel, T=T, E=N_EXPERTS, K=TOP_K, BLK=MOE_SLOT_BLOCK, NBP=nbp)
    tok_out = lambda ph, t: (t * ph, 0)
    fixed = lambda ph, t: (0, 0)
    return pl.pallas_call(
        kern,
        grid=(2, n // T),
        in_specs=[pl.BlockSpec((T, LANES), lambda ph, t: (t, 0))],
        out_specs=[pl.BlockSpec((T, TOP_K), tok_out), pl.BlockSpec((T, TOP_K), tok_out),
                   pl.BlockSpec((8, LANES), fixed), pl.BlockSpec((nbp, LANES), fixed)],
        out_shape=[jax.ShapeDtypeStruct((n, TOP_K), I32), jax.ShapeDtypeStruct((n, TOP_K), F32),
                   jax.ShapeDtypeStruct((8, LANES), I32), jax.ShapeDtypeStruct((nbp, LANES), I32)],
        scratch_shapes=[pltpu.VMEM((T, T), BF16), pltpu.VMEM((1, LANES), F32), pltpu.VMEM((1, LANES), F32),
                        pltpu.VMEM((1, LANES), F32)],
        compiler_params=_cparams(("arbitrary", "arbitrary")),
        name="moe_route",
    )(logits)


def _dispatch_kernel(meta_ref, dest_ref, h_ref, xs_ref, zero_scr, sem, zsem, *, T, K, E, BLK, NB):
    i = pl.program_id(0)
    last = pl.num_programs(0) - 1

    def row_copy(r, kk):
        d = dest_ref[r * K + kk]
        return pltpu.make_async_copy(h_ref.at[pl.ds(r, 1)], xs_ref.at[pl.ds(d, 1)], sem)

    def issue(r, carry):
        for kk in range(K):
            row_copy(r, kk).start(priority=kk % 2)
        return carry

    lax.fori_loop(0, T, issue, 0, unroll=DMA_UNROLL)

    @pl.when(i == last)
    def _():
        zero_scr[...] = jnp.zeros(zero_scr.shape, F32)

        def pad_copy(slot):
            return pltpu.make_async_copy(zero_scr.at[pl.ds(0, 1)], xs_ref.at[pl.ds(slot, 1)], zsem)

        for e in range(E):
            cnt = meta_ref[0, e]
            first = meta_ref[1, e] + cnt
            npad = meta_ref[2, e] - first

            def zissue(r, carry, first=first):
                pad_copy(first + r).start()
                return carry

            def zwait(r, carry, first=first):
                pad_copy(first + r).wait()
                return carry

            lax.fori_loop(0, npad, zissue, 0)
            lax.fori_loop(0, npad, zwait, 0)

        used = meta_ref[2, E - 1] // BLK

        def tail_copy(j):
            return pltpu.make_async_copy(zero_scr, xs_ref.at[pl.ds(pl.multiple_of(j * BLK, BLK), BLK)], zsem)

        def tissue(j, carry):
            tail_copy(j).start()
            return carry

        def twait(j, carry):
            tail_copy(j).wait()
            return carry

        lax.fori_loop(used, NB, tissue, 0)
        lax.fori_loop(used, NB, twait, 0)

    def drain(r, carry):
        for kk in range(K):
            row_copy(r, kk).wait()
        return carry

    lax.fori_loop(0, T, drain, 0, unroll=DMA_UNROLL)


def _dispatch(h2, dest_flat, meta, n_blocks):
    n, d = h2.shape
    T = ROW_TILE
    blk = MOE_SLOT_BLOCK
    kern = functools.partial(_dispatch_kernel, T=T, K=TOP_K, E=N_EXPERTS, BLK=blk, NB=n_blocks)
    gs = pltpu.PrefetchScalarGridSpec(
        num_scalar_prefetch=1,
        grid=(n // T,),
        in_specs=[pl.BlockSpec((T * TOP_K,), lambda i, m: (i,), memory_space=pltpu.SMEM),
                  pl.BlockSpec((T, d), lambda i, m: (i, 0))],
        out_specs=pl.BlockSpec(memory_space=pl.ANY),
        scratch_shapes=[pltpu.VMEM((blk, d), F32), pltpu.SemaphoreType.DMA(()), pltpu.SemaphoreType.DMA(())],
    )
    return pl.pallas_call(
        kern,
        grid_spec=gs,
        out_shape=jax.ShapeDtypeStruct((n_blocks * blk, d), F32),
        compiler_params=_cparams(("arbitrary",), VMEM_LIMIT),
        name="moe_dispatch",
    )(meta, dest_flat, h2)


def _expert_kernel(be_ref, used_ref, xs_ref, wgu_ref, bgu_ref, wd_ref, bd_ref, ys_ref):
    j = pl.program_id(0)

    @pl.when(j < used_ref[0])
    def _():
        x = xs_ref[...].astype(BF16)
        gu = _dot(x, wgu_ref[0]) + bgu_ref[0]
        de = gu.shape[1] // 2
        a = jnp.minimum(gu[:, :de], SWIGLU_LIMIT)
        u = jnp.clip(gu[:, de:], -SWIGLU_LIMIT, SWIGLU_LIMIT)
        act = (u + 1.0) * (a / (1.0 + jnp.exp(-SWIGLU_ALPHA * a)))
        ys_ref[...] = _dot(act.astype(BF16), wd_ref[0]) + bd_ref[0]

    @pl.when(j >= used_ref[0])
    def _():
        ys_ref[...] = jnp.zeros(ys_ref.shape, F32)


def _experts(xs, block_e, used, layer, w_gu_all, b_gu, w_down_all, b_down):
    p, d = xs.shape
    blk = MOE_SLOT_BLOCK
    _, ne, _, de2 = w_gu_all.shape
    nb = p // blk
    last_used = lambda j, be, used: jnp.minimum(j, jnp.maximum(used[0] - 1, 0))
    gs = pltpu.PrefetchScalarGridSpec(
        num_scalar_prefetch=2,
        grid=(nb,),
        in_specs=[pl.BlockSpec((blk, d), lambda j, be, used: (last_used(j, be, used), 0)),
                  pl.BlockSpec((None, 1, d, de2), lambda j, be, used: (layer, be[j], 0, 0)),
                  pl.BlockSpec((1, 1, de2), lambda j, be, used: (be[j], 0, 0)),
                  pl.BlockSpec((None, 1, de2 // 2, d), lambda j, be, used: (layer, be[j], 0, 0)),
                  pl.BlockSpec((1, 1, d), lambda j, be, used: (be[j], 0, 0))],
        out_specs=pl.BlockSpec((blk, d), lambda j, be, used: (j, 0)),
    )
    return pl.pallas_call(
        _expert_kernel,
        grid_spec=gs,
        out_shape=jax.ShapeDtypeStruct((p, d), F32),
        compiler_params=_cparams(("arbitrary",), VMEM_LIMIT),
        name="moe_experts",
    )(block_e, used, xs, w_gu_all, b_gu.reshape(ne, 1, de2), w_down_all, b_down.reshape(ne, 1, d))


def _combine_kernel(dest_ref, dest_next_ref, ys_ref, gate_ref, x_ref, g2_ref, lg_ref, lb_ref, o_ref, buf, sems,
                    *, T, K):
    i = pl.program_id(0)
    n = pl.num_programs(0)
    slot = i % 2

    def row_copy(idx_ref, dst_slot, r, kk):
        d = idx_ref[r * K + kk]
        return pltpu.make_async_copy(ys_ref.at[pl.ds(d, 1)], buf.at[dst_slot, kk, pl.ds(r, 1)],
                                     sems.at[dst_slot])

    def issue_tile(idx_ref, dst_slot):
        def issue(r, carry):
            for kk in range(K):
                row_copy(idx_ref, dst_slot, r, kk).start(priority=kk % 2)
            return carry
        lax.fori_loop(0, T, issue, 0, unroll=DMA_UNROLL)

    @pl.when(i == 0)
    def _():
        issue_tile(dest_ref, slot)

    @pl.when(i + 1 < n)
    def _():
        issue_tile(dest_next_ref, 1 - slot)

    def drain(r, carry):
        for kk in range(K):
            row_copy(dest_ref, slot, r, kk).wait()
        return carry

    lax.fori_loop(0, T, drain, 0, unroll=DMA_UNROLL)
    g = gate_ref[...]
    y = g[:, 0:1] * buf[slot, 0]
    for kk in range(1, K):
        y = y + g[:, kk:kk + 1] * buf[slot, kk]
    z = ALPHA * x_ref[...] + (1.0 + g2_ref[0]) * y
    o_ref[...] = _layer_norm(z, lg_ref[...], lb_ref[...])


def _combine(ys, dest_flat, gates, x1, g2, ln_g, ln_b, seq):
    n, d = x1.shape
    T = ROW_TILE
    tiles_per_seq = seq // T
    row = lambda i: (i, 0)
    last = n // T - 1
    return pl.pallas_call(
        functools.partial(_combine_kernel, T=T, K=TOP_K),
        grid=(n // T,),
        in_specs=[pl.BlockSpec((T * TOP_K,), lambda i: (i,), memory_space=pltpu.SMEM),
                  pl.BlockSpec((T * TOP_K,), lambda i: (jnp.minimum(i + 1, last),), memory_space=pltpu.SMEM),
                  pl.BlockSpec(memory_space=pl.ANY),
                  pl.BlockSpec((T, TOP_K), row), pl.BlockSpec((T, d), row),
                  pl.BlockSpec((1, 1, d), lambda i: (i // tiles_per_seq, 0, 0)),
                  _const_spec((1, d)), _const_spec((1, d))],
        out_specs=pl.BlockSpec((T, d), row),
        out_shape=jax.ShapeDtypeStruct((n, d), F32),
        scratch_shapes=[pltpu.VMEM((2, TOP_K, T, d), F32), pltpu.SemaphoreType.DMA((2,))],
        compiler_params=_cparams(("arbitrary",), VMEM_LIMIT),
        name="moe_combine",
    )(dest_flat, dest_flat, ys, gates, x1, g2, ln_g.reshape(1, d), ln_b.reshape(1, d))


def _moe(h2, logits, x1, g2, ln_g, ln_b, layer, w_gu_all, b_gu, w_down_all, b_down, seq):
    n = h2.shape[0]
    blk = MOE_SLOT_BLOCK
    n_blocks = -(-(n * TOP_K + N_EXPERTS * (blk - 1)) // blk)
    dest, gates, meta, block_e = _route(logits, n_blocks)
    dest_flat = dest.reshape(n * TOP_K)
    xs = _dispatch(h2, dest_flat, meta, n_blocks)
    used = (meta[2, N_EXPERTS - 1] // blk).reshape(1)
    ys = _experts(xs, block_e[:n_blocks, 0], used, layer, w_gu_all, b_gu, w_down_all, b_down)
    return _combine(ys, dest_flat, gates, x1, g2, ln_g, ln_b, seq)


def kernel(x, c, positions, ada_w, ada_b, ln_g, ln_b, mla_w_in, mla_q_norm, mla_w_uq, mla_kv_norm, mla_w_ukv,
           mla_w_out, ml_w_in, ml_b_gates, ml_head_norm, ml_w_out, moe_w_router, moe_b_router, moe_w_gu,
           moe_b_gu, moe_w_down, moe_b_down):
    batch, seq, d = x.shape
    n = batch * seq
    mods = _mods(c, ada_w, ada_b)
    xf = x.reshape(n, d)
    w_gu_bf = moe_w_gu.astype(BF16)
    w_down_bf = moe_w_down.astype(BF16)
    for i in range(DEPTH):
        j = i // 2
        sh1, sc1, g1, sh2, sc2, g2 = [mods[i, :, s * d:(s + 1) * d].reshape(batch, 1, d) for s in range(6)]
        if i % 2 == 0:
            cos, sin = _rope_tables(positions)
            q, k, v = _mla_front(xf, sc1, sh1, cos, sin, mla_w_in[j], mla_q_norm[j], mla_w_uq[j],
                                 mla_kv_norm[j], mla_w_ukv[j], batch, seq)
            a = _flash_attention(q, k, v).reshape(n, MLA_HEADS * MLA_V)
            w_out = mla_w_out[j]
        else:
            proj, gcol, grow = _ml_inproj(xf, sc1, sh1, ml_w_in[j], ml_b_gates[j], seq)
            a = _mlstm_scan(proj, gcol, grow, ml_head_norm[j], batch, seq)
            w_out = ml_w_out[j]
        x1, h2, logits = _outproj_ln_router(a, xf, w_out, g1, ln_g[i, 0], ln_b[i, 0], sc2, sh2,
                                            moe_w_router[i], moe_b_router[i], seq)
        xf = _moe(h2, logits, x1, g2, ln_g[i, 1], ln_b[i, 1], i, w_gu_bf, moe_b_gu[i], w_down_bf,
                  moe_b_down[i], seq)
    return xf.reshape(batch, seq, d)
```

```python
import functools
import math

import jax
import jax.numpy as jnp
from jax import lax
from jax.experimental import pallas as pl
from jax.experimental.pallas import tpu as pltpu

F32 = jnp.float32
BF16 = jnp.bfloat16
I32 = jnp.int32
U32 = jnp.uint32
HIGHEST = lax.Precision.HIGHEST
HIGH_HALF = 0xFFFF0000

CHUNK = 64
MLA_HEADS = 16
MLA_NOPE = 128
MLA_ROPE = 64
MLA_V = 128
MLA_Q_RANK = 448
MLA_KV_RANK = 512
ROPE_THETA = 10000.0
ML_HEADS = 4
ML_QK = 256
ML_V = 512
GATE_SOFTCAP = 15.0
N_EXPERTS = 32
TOP_K = 4
D_EXPERT = 1024
SWIGLU_LIMIT = 7.0
SWIGLU_ALPHA = 1.702
DEPTH = 2
ALPHA = (2 * DEPTH) ** 0.25
EPS = 1e-6
NEG_BIG = -1e30

LANES = 128
VMEM_LIMIT = 56 * 1024 * 1024

Q_RANK_PAD = 512
ATT_BLOCK = 512
ATT_KBLOCK = 1024
ATT_CHAINS = 8
ML_CHUNK = 256
ML_HEADS_PER_STEP = 4
MOE_SLOT_BLOCK = 512
ROUTE_TILE = 1024
ROW_TILE = 256
MOE_ROW_TILE = 512
OUTPROJ_TILE = 512
DMA_UNROLL = 8


def _cparams(sem, vmem=None):
    return pltpu.CompilerParams(dimension_semantics=sem, vmem_limit_bytes=vmem)


def _const_spec(shape):
    nd = len(shape)
    return pl.BlockSpec(shape, lambda *_: (0,) * nd, pipeline_mode=pl.Buffered(1))


def _split_bf16(x):
    hi = x.astype(BF16)
    lo = (x - hi.astype(F32)).astype(BF16)
    return hi, lo


def _dot(a, b):
    return jnp.dot(a, b, preferred_element_type=F32)


def _dot_nt(a, b):
    return lax.dot_general(a, b, (((1,), (1,)), ((), ())), preferred_element_type=F32)


def _dot3(x, w_hi, w_lo):
    x_hi, x_lo = _split_bf16(x)
    return _dot(x_hi, w_hi) + (_dot(x_lo, w_hi) + _dot(x_hi, w_lo))


def _pack_rows(x):
    w = x.shape[1] // 2
    lo = pltpu.bitcast(x[:, :w].astype(BF16).astype(F32), U32)
    hi = pltpu.bitcast(x[:, w:].astype(BF16).astype(F32), U32)
    return (hi & jnp.uint32(HIGH_HALF)) | lax.shift_right_logical(lo, jnp.uint32(16))


def _unpack_rows(words):
    lo = pltpu.bitcast(lax.shift_left(words, jnp.uint32(16)), F32)
    hi = pltpu.bitcast(words & jnp.uint32(HIGH_HALF), F32)
    return lo, hi


def _layer_norm(z, g, b):
    mu = jnp.mean(z, axis=-1, keepdims=True)
    zc = z - mu
    var = jnp.mean(zc * zc, axis=-1, keepdims=True)
    return zc * lax.rsqrt(var + EPS) * g + b


def _mods_kernel(c_ref, w_ref, b_ref, o_ref):
    c = c_ref[...]
    ca = c / (1.0 + jnp.exp(-c))
    o_ref[0] = jnp.dot(ca, w_ref[0], precision=HIGHEST, preferred_element_type=F32) + b_ref[0]


def _mods(c, ada_w, ada_b):
    depth, d, d6 = ada_w.shape
    b = c.shape[0]
    rows = 8
    c8 = jnp.pad(c, ((0, rows - b), (0, 0)))
    tn = 1024
    out = pl.pallas_call(
        _mods_kernel,
        grid=(depth, d6 // tn),
        in_specs=[pl.BlockSpec((rows, d), lambda i, j: (0, 0)),
                  pl.BlockSpec((1, d, tn), lambda i, j: (i, 0, j)),
                  pl.BlockSpec((1, 1, tn), lambda i, j: (i, 0, j))],
        out_specs=pl.BlockSpec((1, rows, tn), lambda i, j: (i, 0, j)),
        out_shape=jax.ShapeDtypeStruct((depth, rows, d6), F32),
        compiler_params=_cparams(("arbitrary", "arbitrary"), VMEM_LIMIT),
        name="adaln_mods",
    )(c8, ada_w, ada_b.reshape(depth, 1, d6))
    return out[:, :b]


def _rope_kernel(pos_ref, inv_ref, cos_ref, sin_ref):
    ang = pos_ref[...].astype(F32) * inv_ref[...]
    cos_ref[...] = jnp.cos(ang)
    sin_ref[...] = jnp.sin(ang)


def _rope_tables(positions):
    n = positions.size
    half = MLA_ROPE // 2
    inv = ROPE_THETA ** (-jnp.arange(0, MLA_ROPE, 2, dtype=F32) / MLA_ROPE)
    inv_row = jnp.tile(inv, LANES // half).reshape(1, LANES)
    pos = jnp.broadcast_to(positions.reshape(n, 1), (n, LANES))
    tm = 1024
    spec = pl.BlockSpec((tm, LANES), lambda i: (i, 0))
    return pl.pallas_call(
        _rope_kernel,
        grid=(n // tm,),
        in_specs=[spec, pl.BlockSpec((1, LANES), lambda i: (0, 0))],
        out_specs=[spec, spec],
        out_shape=[jax.ShapeDtypeStruct((n, LANES), F32)] * 2,
        compiler_params=_cparams(("arbitrary",)),
        name="rope_tables",
    )(pos, inv_row)


def _mla_front_kernel(x_ref, sc_ref, sh_ref, cos_ref, sin_ref, win_ref, qn_ref, kvn_ref, wq_ref, wkv_ref,
                      q_ref, k_ref, v_ref, *, qscale):
    h = (x_ref[...] * (1.0 + sc_ref[0]) + sh_ref[0]).astype(BF16)
    proj = _dot(h, win_ref[...])
    cq = proj[:, :Q_RANK_PAD]
    ckv = proj[:, Q_RANK_PAD:Q_RANK_PAD + MLA_KV_RANK]
    cq = cq * lax.rsqrt(jnp.sum(cq * cq, -1, keepdims=True) * (1.0 / MLA_Q_RANK) + EPS) * qn_ref[...]
    ckv = ckv * lax.rsqrt(jnp.mean(ckv * ckv, -1, keepdims=True) + EPS) * kvn_ref[...]
    cq = cq.astype(BF16)
    ckv = ckv.astype(BF16)
    cos = cos_ref[...]
    sin = sin_ref[...]
    o = Q_RANK_PAD + MLA_KV_RANK
    kr = proj[:, o:o + LANES] * cos + proj[:, o + LANES:o + 2 * LANES] * sin
    lane = lax.broadcasted_iota(I32, kr.shape, 1)
    kr_even = jnp.where(lane < MLA_ROPE, kr, 0.0).astype(BF16)
    kr_odd = jnp.where(lane >= MLA_ROPE, kr, 0.0).astype(BF16)
    ones_col = jnp.where(lane == 0, 1.0, 0.0).astype(BF16)

    nheads = MLA_HEADS
    nope_w = nheads * MLA_NOPE
    rope_w = nheads * MLA_ROPE
    group = 4
    for g in range(nheads // group):
        qn = _dot(cq, wq_ref[:, g * group * MLA_NOPE:(g + 1) * group * MLA_NOPE]) * qscale
        kv = _dot(ckv, wkv_ref[:, g * group * 2 * LANES:(g + 1) * group * 2 * LANES])
        for j in range(group):
            hh = g * group + j
            q_ref[0, hh, :, :LANES] = qn[:, j * LANES:(j + 1) * LANES].astype(BF16)
            k_ref[0, hh, :, :LANES] = kv[:, 2 * j * LANES:(2 * j + 1) * LANES].astype(BF16)
            v_ref[0, hh, :, :LANES] = kv[:, (2 * j + 1) * LANES:(2 * j + 2) * LANES].astype(BF16)
            k_ref[0, hh, :, LANES:] = kr_even if hh % 2 == 0 else kr_odd
            v_ref[0, hh, :, LANES:] = ones_col
    for p in range(nheads // 2):
        lo = nope_w + p * LANES
        qr = (_dot(cq, wq_ref[:, lo:lo + LANES]) * cos
              + _dot(cq, wq_ref[:, rope_w + lo:rope_w + lo + LANES]) * sin) * qscale
        qr = qr.astype(BF16)
        q_ref[0, 2 * p, :, LANES:] = qr
        q_ref[0, 2 * p + 1, :, LANES:] = qr


def _mla_front(x2d, sc, sh, cos, sin, w_in, q_norm, w_uq, kv_norm, w_ukv, batch, seq):
    n, d = x2d.shape
    nh = MLA_HEADS
    half = MLA_ROPE // 2
    wq_lat = jnp.pad(w_in[:, :MLA_Q_RANK], ((0, 0), (0, Q_RANK_PAD - MLA_Q_RANK)))
    wkv_lat = w_in[:, MLA_Q_RANK:MLA_Q_RANK + MLA_KV_RANK]
    wkr = w_in[:, MLA_Q_RANK + MLA_KV_RANK:]
    wkr_rot = jnp.concatenate([-wkr[:, half:], wkr[:, :half]], 1)
    win = jnp.concatenate([wq_lat, wkv_lat, wkr, wkr, wkr_rot, wkr_rot], 1).astype(BF16)
    wq3 = w_uq.reshape(MLA_Q_RANK, nh, MLA_NOPE + MLA_ROPE)
    wq_nope = wq3[:, :, :MLA_NOPE].reshape(MLA_Q_RANK, nh * MLA_NOPE)
    wq_r = wq3[:, :, MLA_NOPE:]
    wq_rope = wq_r.reshape(MLA_Q_RANK, nh * MLA_ROPE)
    wq_rot = jnp.concatenate([-wq_r[:, :, half:], wq_r[:, :, :half]], -1).reshape(MLA_Q_RANK, nh * MLA_ROPE)
    wq = jnp.pad(jnp.concatenate([wq_nope, wq_rope, wq_rot], 1),
                 ((0, Q_RANK_PAD - MLA_Q_RANK), (0, 0))).astype(BF16)
    wkv = w_ukv.astype(BF16)
    qn = jnp.pad(q_norm, (0, Q_RANK_PAD - MLA_Q_RANK)).reshape(1, Q_RANK_PAD)
    kvn = kv_norm.reshape(1, MLA_KV_RANK)
    qscale = (MLA_NOPE + MLA_ROPE) ** -0.5 * math.log2(math.e)

    tm = ROW_TILE
    tiles_per_seq = seq // tm
    row = lambda i: (i, 0)
    per_b = lambda i: (i // tiles_per_seq, 0, 0)
    head_out = pl.BlockSpec((1, nh, tm, 2 * LANES), lambda i: (i // tiles_per_seq, 0, i % tiles_per_seq, 0))
    out_sds = jax.ShapeDtypeStruct((batch, nh, seq, 2 * LANES), BF16)
    return pl.pallas_call(
        functools.partial(_mla_front_kernel, qscale=qscale),
        grid=(n // tm,),
        in_specs=[pl.BlockSpec((tm, d), row),
                  pl.BlockSpec((1, 1, d), per_b), pl.BlockSpec((1, 1, d), per_b),
                  pl.BlockSpec((tm, LANES), row), pl.BlockSpec((tm, LANES), row),
                  _const_spec(win.shape), _const_spec(qn.shape), _const_spec(kvn.shape),
                  _const_spec(wq.shape), _const_spec(wkv.shape)],
        out_specs=[head_out, head_out, head_out],
        out_shape=[out_sds, out_sds, out_sds],
        compiler_params=_cparams(("arbitrary",), VMEM_LIMIT),
        name="mla_front",
    )(x2d, sc, sh, cos, sin, win, qn, kvn, wq, wkv)


def _flash_kernel(q_ref, k_ref, v_ref, o_ref, m_ref, acc_ref, *, blk, kblk, nsub):
    i = pl.program_id(2)
    m_ref[...] = jnp.full(m_ref.shape, NEG_BIG, F32)
    acc_ref[...] = jnp.zeros(acc_ref.shape, F32)
    slabs = kblk // LANES
    ratio = kblk // blk
    shift = CHUNK.bit_length() - 1
    everyone = range(nsub)

    def step(c, kb, masked):
        start = pl.multiple_of(kb * kblk, kblk)
        q = q_ref[0, 0, c * blk:(c + 1) * blk, :]
        k = k_ref[0, 0, pl.ds(start, kblk), :]
        v = v_ref[0, 0, pl.ds(start, kblk), :]
        s = _dot_nt(q, k)
        if masked:
            r = lax.shift_right_logical(lax.broadcasted_iota(I32, s.shape, 0) + (c % ratio) * blk, shift)
            cc = lax.shift_right_logical(lax.broadcasted_iota(I32, s.shape, 1), shift)
            s = jnp.where(cc <= r, s, NEG_BIG)
        m_prev = m_ref[c]
        m_new = jnp.maximum(m_prev, jnp.max(s, axis=1, keepdims=True))
        p = jnp.concatenate([jnp.exp2(s[:, j * LANES:(j + 1) * LANES] - m_new).astype(BF16)
                             for j in range(slabs)], axis=1)
        alpha = jnp.exp2(m_prev - m_new)
        pv = _dot(p, v)
        acc_ref[c, :, :LANES] = alpha * acc_ref[c, :, :LANES] + pv[:, :LANES]
        acc_ref[c, :, LANES:] = alpha * acc_ref[c, :, LANES:] + pv[:, LANES:]
        m_ref[c] = m_new

    def full_steps(kb, carry):
        for c in everyone:
            step(c, kb, False)
        return carry

    first_diag = (nsub // ratio) * i
    lax.fori_loop(0, first_diag, full_steps, 0)
    for t in range(nsub // ratio):
        for c in everyone:
            if c // ratio >= t:
                step(c, first_diag + t, masked=(c // ratio == t))
    for c in everyone:
        l = acc_ref[c, :, LANES:LANES + 1]
        o_ref[0, c * blk:(c + 1) * blk, :] = (acc_ref[c, :, :MLA_V] / l).astype(o_ref.dtype)


def _flash_attention(q, k, v):
    batch, nh, seq, dk = q.shape
    blk = ATT_BLOCK
    nsub = ATT_CHAINS
    kblk = ATT_KBLOCK
    assert kblk % blk == 0 and nsub % (kblk // blk) == 0
    qspec = pl.BlockSpec((1, 1, nsub * blk, dk), lambda b, h, i: (b, h, i, 0))
    kvspec = pl.BlockSpec((1, 1, seq, dk), lambda b, h, i: (b, h, 0, 0))
    return pl.pallas_call(
        functools.partial(_flash_kernel, blk=blk, kblk=kblk, nsub=nsub),
        grid=(batch, nh, seq // (nsub * blk)),
        in_specs=[qspec, kvspec, kvspec],
        out_specs=pl.BlockSpec((1, nsub * blk, MLA_V), lambda b, h, i: (b, i, h)),
        out_shape=jax.ShapeDtypeStruct((batch, seq, nh * MLA_V), BF16),
        scratch_shapes=[pltpu.VMEM((nsub, blk, LANES), F32), pltpu.VMEM((nsub, blk, dk), F32)],
        compiler_params=_cparams(("arbitrary", "arbitrary", "arbitrary"), VMEM_LIMIT),
        name="flash_attention",
    )(q, k, v)


def _outproj_kernel(a_ref, x_ref, g1_ref, lg_ref, lb_ref, sc_ref, sh_ref, w_ref, rwh_ref, rwl_ref, rb_ref,
                    x1_ref, h2_ref, logit_ref):
    y = _dot(a_ref[...], w_ref[...])
    z = ALPHA * x_ref[...] + (1.0 + g1_ref[0]) * y
    x1 = _layer_norm(z, lg_ref[...], lb_ref[...])
    x1_ref[...] = x1
    h2 = x1 * (1.0 + sc_ref[0]) + sh_ref[0]
    h2_ref[...] = _pack_rows(h2)
    logit_ref[...] = _dot3(h2, rwh_ref[...], rwl_ref[...]) + rb_ref[...]


def _outproj_ln_router(a2d, x2d, w_out, g1, ln_g, ln_b, sc2, sh2, w_router, b_router, seq):
    n, d = x2d.shape
    da = a2d.shape[1]
    ne = w_router.shape[1]
    w = w_out.astype(BF16)
    rw = jnp.pad(w_router, ((0, 0), (0, LANES - ne)))
    rw_hi, rw_lo = _split_bf16(rw)
    rb = jnp.pad(b_router, (0, LANES - ne), constant_values=NEG_BIG).reshape(1, LANES)
    tm = OUTPROJ_TILE
    tiles_per_seq = seq // tm
    row = lambda i: (i, 0)
    per_b = lambda i: (i // tiles_per_seq, 0, 0)
    vec = pl.BlockSpec((1, 1, d), per_b)
    return pl.pallas_call(
        _outproj_kernel,
        grid=(n // tm,),
        in_specs=[pl.BlockSpec((tm, da), row), pl.BlockSpec((tm, d), row), vec,
                  _const_spec((1, d)), _const_spec((1, d)), vec, vec,
                  _const_spec(w.shape), _const_spec(rw_hi.shape), _const_spec(rw_lo.shape),
                  _const_spec(rb.shape)],
        out_specs=[pl.BlockSpec((tm, d), row), pl.BlockSpec((tm, d // 2), row), pl.BlockSpec((tm, LANES), row)],
        out_shape=[jax.ShapeDtypeStruct((n, d), F32), jax.ShapeDtypeStruct((n, d // 2), U32),
                   jax.ShapeDtypeStruct((n, LANES), F32)],
        compiler_params=_cparams(("arbitrary",), VMEM_LIMIT),
        name="outproj_ln_router",
    )(a2d, x2d, g1, ln_g.reshape(1, d), ln_b.reshape(1, d), sc2, sh2, w, rw_hi, rw_lo, rb)


def _gate_log(g, is_input_gate):
    g = GATE_SOFTCAP * jnp.tanh(g * (1.0 / GATE_SOFTCAP))
    log_f = jnp.minimum(g, 0.0) - jnp.log(1.0 + jnp.exp(-jnp.abs(g)))
    return jnp.where(is_input_gate, g, log_f)


def _ml_inproj_kernel(x_ref, sc_ref, sh_ref, w_ref, gwh_ref, gwl_ref, gb_ref, gwth_ref, gwtl_ref, gbt_ref,
                      proj_ref, gcol_ref, grow_ref, h_scr):
    j = pl.program_id(1)

    @pl.when(j == 0)
    def _():
        h = x_ref[...] * (1.0 + sc_ref[0]) + sh_ref[0]
        h_hi, h_lo = _split_bf16(h)
        h_scr[...] = h_hi
        nh = ML_HEADS
        g = _dot(h_hi, gwh_ref[...]) + (_dot(h_lo, gwh_ref[...]) + _dot(h_hi, gwl_ref[...])) + gb_ref[...]
        lane = lax.broadcasted_iota(I32, g.shape, 1)
        gcol_ref[...] = _gate_log(g, lane < nh)
        gt = (_dot_nt(gwth_ref[...], h_hi) + (_dot_nt(gwth_ref[...], h_lo) + _dot_nt(gwtl_ref[...], h_hi))
              + gbt_ref[...])
        sub = lax.broadcasted_iota(I32, gt.shape, 0)
        grow_ref[...] = _gate_log(gt, sub < nh)

    proj_ref[...] = _dot(h_scr[...], w_ref[...]).astype(proj_ref.dtype)


def _ml_inproj(x2d, sc, sh, w_in, b_gates, seq):
    n, d = x2d.shape
    ng = 2 * ML_HEADS
    wide = w_in.shape[1] - ng
    w = w_in[:, :wide].astype(BF16)
    gw = w_in[:, wide:]
    gw_hi, gw_lo = _split_bf16(jnp.pad(gw, ((0, 0), (0, LANES - ng))))
    gwt_hi, gwt_lo = _split_bf16(gw.T)
    gb = jnp.pad(b_gates, (0, LANES - ng)).reshape(1, LANES)
    gbt = b_gates.reshape(ng, 1)
    tm, tn = 1024, 1024
    tiles_per_seq = seq // tm
    row = lambda i, j: (i, 0)
    per_b = lambda i, j: (i // tiles_per_seq, 0, 0)
    const = lambda shape: pl.BlockSpec(shape, lambda i, j: (0,) * len(shape), pipeline_mode=pl.Buffered(1))
    return pl.pallas_call(
        _ml_inproj_kernel,
        grid=(n // tm, wide // tn),
        in_specs=[pl.BlockSpec((tm, d), row),
                  pl.BlockSpec((1, 1, d), per_b), pl.BlockSpec((1, 1, d), per_b),
                  pl.BlockSpec((d, tn), lambda i, j: (0, j)),
                  const(gw_hi.shape), const(gw_lo.shape), const(gb.shape),
                  const(gwt_hi.shape), const(gwt_lo.shape), const(gbt.shape)],
        out_specs=[pl.BlockSpec((tm, tn), lambda i, j: (i, j)),
                   pl.BlockSpec((tm, LANES), row),
                   pl.BlockSpec((ng, tm), lambda i, j: (0, i))],
        out_shape=[jax.ShapeDtypeStruct((n, wide), BF16), jax.ShapeDtypeStruct((n, LANES), F32),
                   jax.ShapeDtypeStruct((ng, n), F32)],
        scratch_shapes=[pltpu.VMEM((tm, d), BF16)],
        compiler_params=_cparams(("arbitrary", "arbitrary"), VMEM_LIMIT),
        name="mlstm_inproj",
    )(x2d, sc, sh, w, gw_hi, gw_lo, gb, gwt_hi, gwt_lo, gbt)


def _split3_bf16(x):
    a = x.astype(BF16)
    r = x - a.astype(F32)
    b = r.astype(BF16)
    c = (r - b.astype(F32)).astype(BF16)
    return a, b, c


def _mlstm_kernel(q_ref, k_ref, v_ref, og_ref, gc_ref, gr_ref, hn_ref, o_ref, c_scr, m_scr, *, L, G):
    ci = pl.program_id(2)

    @pl.when(ci == 0)
    def _():
        c_scr[...] = jnp.zeros(c_scr.shape, F32)
        m_scr[...] = jnp.zeros(m_scr.shape, F32)

    row = lax.broadcasted_iota(I32, (L, L), 0)
    col = lax.broadcasted_iota(I32, (L, L), 1)
    causal = col <= row
    tri = jnp.where(causal, 1.0, 0.0).astype(BF16)
    ones_col = jnp.where(lax.broadcasted_iota(I32, (L, LANES), 1) == 0, 1.0, 0.0).astype(BF16)
    for g in range(G):
        _mlstm_head(q_ref[:, g * ML_QK:(g + 1) * ML_QK], k_ref[:, g * ML_QK:(g + 1) * ML_QK],
                    v_ref[:, g * ML_V:(g + 1) * ML_V], og_ref[:, g * ML_V:(g + 1) * ML_V],
                    gc_ref[g], gr_ref[g], hn_ref[:, g * ML_V:(g + 1) * ML_V],
                    o_ref.at[:, g * ML_V:(g + 1) * ML_V], c_scr.at[g], m_scr.at[g],
                    causal, tri, ones_col, L)


def _mlstm_head(q, k, v, og, gc, gr, head_gain, o_ref, c_scr, m_scr, causal, tri, ones_col, L):
    v_ext = jnp.concatenate([v, ones_col], axis=1)
    li_c, lf_c = gc[:, 0:1], gc[:, 1:2]
    li_r, lf_r = gr[0:1, :], gr[1:2, :]

    lfc3 = _split3_bf16(jnp.broadcast_to(lf_c, (L, LANES)))
    b_c = (_dot(tri, lfc3[0]) + (_dot(tri, lfc3[1]) + _dot(tri, lfc3[2])))[:, 0:1]
    lfr3 = _split3_bf16(jnp.broadcast_to(lf_r, (8, L)))
    b_r = (_dot_nt(lfr3[0], tri) + (_dot_nt(lfr3[1], tri) + _dot_nt(lfr3[2], tri)))[0:1, :]

    m_prev = m_scr[...]
    dm = jnp.where(causal, b_c - (b_r - li_r), NEG_BIG)
    inter = b_c + m_prev
    m_t = jnp.maximum(inter, jnp.max(dm, axis=1, keepdims=True))
    w_intra = jnp.exp(dm - m_t)
    w_inter = jnp.exp(inter - m_t)

    kscale = ML_QK ** -0.5
    qk = _dot_nt(q, k) * (w_intra * kscale)
    c_state = c_scr[...]
    num_ext = w_inter * _dot(q, c_state.astype(BF16)) + _dot(qk.astype(BF16), v_ext)
    vdim = v.shape[1]
    num = num_ext[:, :vdim]
    den = num_ext[:, vdim:vdim + 1]
    h = num / jnp.maximum(jnp.abs(den), jnp.exp(-m_t))
    hn = h * lax.rsqrt(jnp.mean(h * h, axis=-1, keepdims=True) + EPS) * head_gain
    o_ref[...] = (hn / (1.0 + jnp.exp(-og.astype(F32)))).astype(o_ref.dtype)

    b_end = b_r[:, L - 1:L]
    d_end_r = b_end - b_r + li_r
    m_new = jnp.maximum(b_end + m_prev, jnp.max(d_end_r, axis=1, keepdims=True))
    decay = jnp.exp(b_end + m_prev - m_new)
    w_s = jnp.exp(b_end - b_c + li_c - m_new) * kscale
    kw = (k.astype(F32) * w_s).astype(BF16)
    upd = lax.dot_general(kw, v_ext, (((0,), (0,)), ((), ())), preferred_element_type=F32)
    c_scr[...] = decay * c_state + upd
    m_scr[...] = m_new


def _mlstm_scan(proj, gcol, grow, head_norm, batch, seq):
    n = proj.shape[0]
    nh = ML_HEADS
    L = ML_CHUNK
    nc = seq // L
    gc = gcol[:, :2 * nh].reshape(n, 2, nh).transpose(2, 0, 1)
    gr = grow.reshape(2, nh, n).transpose(1, 0, 2)
    hn = head_norm.reshape(1, nh * ML_V)
    G = ML_HEADS_PER_STEP
    groups = nh // G
    k_off = groups
    v_off = 2 * nh * ML_QK // (G * ML_V)
    og_off = v_off + groups
    tok = lambda b, h, c: b * nc + c
    return pl.pallas_call(
        functools.partial(_mlstm_kernel, L=L, G=G),
        grid=(batch, groups, nc),
        in_specs=[pl.BlockSpec((L, G * ML_QK), lambda b, h, c: (tok(b, h, c), h)),
                  pl.BlockSpec((L, G * ML_QK), lambda b, h, c: (tok(b, h, c), k_off + h)),
                  pl.BlockSpec((L, G * ML_V), lambda b, h, c: (tok(b, h, c), v_off + h)),
                  pl.BlockSpec((L, G * ML_V), lambda b, h, c: (tok(b, h, c), og_off + h)),
                  pl.BlockSpec((G, L, 2), lambda b, h, c: (h, tok(b, h, c), 0)),
                  pl.BlockSpec((G, 2, L), lambda b, h, c: (h, 0, tok(b, h, c))),
                  pl.BlockSpec((1, G * ML_V), lambda b, h, c: (0, h))],
        out_specs=pl.BlockSpec((L, G * ML_V), lambda b, h, c: (tok(b, h, c), h)),
        out_shape=jax.ShapeDtypeStruct((n, nh * ML_V), BF16),
        scratch_shapes=[pltpu.VMEM((G, ML_QK, ML_V + LANES), F32), pltpu.VMEM((G, 1, 1), F32)],
        compiler_params=_cparams(("arbitrary", "arbitrary", "arbitrary"), VMEM_LIMIT),
        name="mlstm_scan",
    )(proj, proj, proj, proj, gc, gr, hn)


def _route_kernel(lg_ref, dest_ref, gate_ref, meta_ref, blk_ref, tri_scr, cnt_scr, run_scr, pst_scr,
                  *, T, E, K, BLK, NBP):
    ph = pl.program_id(0)
    t = pl.program_id(1)
    nt = pl.num_programs(1)

    @pl.when((ph == 0) & (t == 0))
    def _():
        r = lax.broadcasted_iota(I32, (T, T), 0)
        c = lax.broadcasted_iota(I32, (T, T), 1)
        tri_scr[...] = jnp.where(c < r, 1.0, 0.0).astype(BF16)
        cnt_scr[...] = jnp.zeros(cnt_scr.shape, F32)

    lane = lax.broadcasted_iota(I32, (T, LANES), 1)
    l = jnp.where(lane < E, lg_ref[...], -jnp.inf)
    vals, hots = [], []
    for _ in range(K):
        mx = jnp.max(l, axis=1, keepdims=True)
        idx = jnp.min(jnp.where(l == mx, lane, LANES), axis=1, keepdims=True)
        hot = lane == idx
        vals.append(mx)
        hots.append(hot)
        l = jnp.where(hot, -jnp.inf, l)
    hot_all = jnp.zeros((T, LANES), F32)
    for hot in hots:
        hot_all = hot_all + jnp.where(hot, 1.0, 0.0)
    colsum = jnp.sum(hot_all, axis=0, keepdims=True)

    @pl.when(ph == 0)
    def _():
        cnt_scr[...] = cnt_scr[...] + colsum

    @pl.when((ph == 1) & (t == 0))
    def _():
        cnt = cnt_scr[...].astype(I32)
        pc = ((cnt + (BLK - 1)) & (-BLK)).astype(F32)
        r = lax.broadcasted_iota(I32, (LANES, LANES), 0)
        c = lax.broadcasted_iota(I32, (LANES, LANES), 1)
        upper = jnp.where(r < c, 1.0, 0.0)
        pstart = jnp.dot(jnp.broadcast_to(pc, (8, LANES)), upper, precision=HIGHEST,
                         preferred_element_type=F32)[0:1, :]
        pst_scr[...] = pstart
        run_scr[...] = jnp.zeros(run_scr.shape, F32)
        pend = pstart + pc
        meta_ref[0:1, :] = cnt_scr[...].astype(I32)
        meta_ref[1:2, :] = pstart.astype(I32)
        meta_ref[2:3, :] = pend.astype(I32)
        meta_ref[3:8, :] = jnp.zeros((5, LANES), I32)
        jstart = (lax.broadcasted_iota(I32, (NBP, LANES), 0) * BLK).astype(F32)
        elane = lax.broadcasted_iota(I32, (NBP, LANES), 1)
        owned = jnp.where((pend <= jstart) & (elane < E), 1.0, 0.0)
        be = jnp.minimum(jnp.sum(owned, axis=1, keepdims=True), float(E - 1))
        blk_ref[...] = jnp.broadcast_to(be, (NBP, LANES)).astype(I32)

    @pl.when(ph == 1)
    def _():
        earlier = _dot(tri_scr[...], hot_all.astype(BF16))
        pos = earlier + (pst_scr[...] + run_scr[...])
        e0 = jnp.ones_like(vals[0])
        es = [e0] + [jnp.exp(v - vals[0]) for v in vals[1:]]
        tot = es[0]
        for e in es[1:]:
            tot = tot + e
        for kk in range(K):
            d = jnp.sum(jnp.where(hots[kk], pos, 0.0), axis=1, keepdims=True)
            dest_ref[:, kk:kk + 1] = d.astype(I32)
            gate_ref[:, kk:kk + 1] = es[kk] / tot
        run_scr[...] = run_scr[...] + colsum


def _route(logits, n_blocks):
    n = logits.shape[0]
    T = min(ROUTE_TILE, n)
    nbp = -(-n_blocks // 8) * 8
    kern = functools.partial(_route_kernel, T=T, E=N_EXPERTS, K=TOP_K, BLK=MOE_SLOT_BLOCK, NBP=nbp)
    tok_out = lambda ph, t: (t * ph, 0)
    fixed = lambda ph, t: (0, 0)
    return pl.pallas_call(
        kern,
        grid=(2, n // T),
        in_specs=[pl.BlockSpec((T, LANES), lambda ph, t: (t, 0))],
        out_specs=[pl.BlockSpec((T, TOP_K), tok_out), pl.BlockSpec((T, TOP_K), tok_out),
                   pl.BlockSpec((8, LANES), fixed), pl.BlockSpec((nbp, LANES), fixed)],
        out_shape=[jax.ShapeDtypeStruct((n, TOP_K), I32), jax.ShapeDtypeStruct((n, TOP_K), F32),
                   jax.ShapeDtypeStruct((8, LANES), I32), jax.ShapeDtypeStruct((nbp, LANES), I32)],
        scratch_shapes=[pltpu.VMEM((T, T), BF16), pltpu.VMEM((1, LANES), F32), pltpu.VMEM((1, LANES), F32),
                        pltpu.VMEM((1, LANES), F32)],
        compiler_params=_cparams(("arbitrary", "arbitrary")),
        name="moe_route",
    )(logits)


def _dispatch_kernel(meta_ref, dest_ref, h_ref, xs_ref, zero_scr, sem, zsem, *, T, K, E, BLK, NB):
    i = pl.program_id(0)
    last = pl.num_programs(0) - 1

    def row_copy(r, kk):
        d = dest_ref[r * K + kk]
        return pltpu.make_async_copy(h_ref.at[pl.ds(r, 1)], xs_ref.at[pl.ds(d, 1)], sem)

    def issue(r, carry):
        for kk in range(K):
            row_copy(r, kk).start(priority=kk % 2)
        return carry

    lax.fori_loop(0, T, issue, 0, unroll=DMA_UNROLL)

    @pl.when(i == last)
    def _():
        zero_scr[...] = jnp.zeros(zero_scr.shape, zero_scr.dtype)

        def pad_copy(slot):
            return pltpu.make_async_copy(zero_scr.at[pl.ds(0, 1)], xs_ref.at[pl.ds(slot, 1)], zsem)

        for e in range(E):
            cnt = meta_ref[0, e]
            first = meta_ref[1, e] + cnt
            npad = meta_ref[2, e] - first

            def zissue(r, carry, first=first):
                pad_copy(first + r).start()
                return carry

            def zwait(r, carry, first=first):
                pad_copy(first + r).wait()
                return carry

            lax.fori_loop(0, npad, zissue, 0)
            lax.fori_loop(0, npad, zwait, 0)

        used = meta_ref[2, E - 1] // BLK

        def tail_copy(j):
            return pltpu.make_async_copy(zero_scr, xs_ref.at[pl.ds(pl.multiple_of(j * BLK, BLK), BLK)], zsem)

        def tissue(j, carry):
            tail_copy(j).start()
            return carry

        def twait(j, carry):
            tail_copy(j).wait()
            return carry

        lax.fori_loop(used, NB, tissue, 0)
        lax.fori_loop(used, NB, twait, 0)

    def drain(r, carry):
        for kk in range(K):
            row_copy(r, kk).wait()
        return carry

    lax.fori_loop(0, T, drain, 0, unroll=DMA_UNROLL)


def _dispatch(h2, dest_flat, meta, n_blocks):
    n, d = h2.shape
    T = MOE_ROW_TILE
    blk = MOE_SLOT_BLOCK
    kern = functools.partial(_dispatch_kernel, T=T, K=TOP_K, E=N_EXPERTS, BLK=blk, NB=n_blocks)
    gs = pltpu.PrefetchScalarGridSpec(
        num_scalar_prefetch=1,
        grid=(n // T,),
        in_specs=[pl.BlockSpec((T * TOP_K,), lambda i, m: (i,), memory_space=pltpu.SMEM),
                  pl.BlockSpec((T, d), lambda i, m: (i, 0))],
        out_specs=pl.BlockSpec(memory_space=pl.ANY),
        scratch_shapes=[pltpu.VMEM((blk, d), h2.dtype), pltpu.SemaphoreType.DMA(()),
                        pltpu.SemaphoreType.DMA(())],
    )
    return pl.pallas_call(
        kern,
        grid_spec=gs,
        out_shape=jax.ShapeDtypeStruct((n_blocks * blk, d), h2.dtype),
        compiler_params=_cparams(("arbitrary",), VMEM_LIMIT),
        name="moe_dispatch",
    )(meta, dest_flat, h2)


def _expert_kernel(be_ref, used_ref, xs_ref, wgu_ref, bgu_ref, wd_ref, bd_ref, ys_ref):
    j = pl.program_id(0)

    @pl.when(j < used_ref[0])
    def _():
        x_lo, x_hi = _unpack_rows(xs_ref[...])
        x = jnp.concatenate([x_lo.astype(BF16), x_hi.astype(BF16)], axis=1)
        gu = _dot(x, wgu_ref[0]) + bgu_ref[0]
        de = gu.shape[1] // 2
        a = jnp.minimum(gu[:, :de], SWIGLU_LIMIT)
        u = jnp.clip(gu[:, de:], -SWIGLU_LIMIT, SWIGLU_LIMIT)
        act = (u + 1.0) * (a / (1.0 + jnp.exp(-SWIGLU_ALPHA * a)))
        ys_ref[...] = _pack_rows(_dot(act.astype(BF16), wd_ref[0]) + bd_ref[0])

    @pl.when(j >= used_ref[0])
    def _():
        ys_ref[...] = jnp.zeros(ys_ref.shape, ys_ref.dtype)


def _experts(xs, block_e, used, layer, w_gu_all, b_gu, w_down_all, b_down):
    p, dw = xs.shape
    d = 2 * dw
    blk = MOE_SLOT_BLOCK
    _, ne, _, de2 = w_gu_all.shape
    nb = p // blk
    last_used = lambda j, be, used: jnp.minimum(j, jnp.maximum(used[0] - 1, 0))
    gs = pltpu.PrefetchScalarGridSpec(
        num_scalar_prefetch=2,
        grid=(nb,),
        in_specs=[pl.BlockSpec((blk, dw), lambda j, be, used: (last_used(j, be, used), 0)),
                  pl.BlockSpec((None, 1, d, de2), lambda j, be, used: (layer, be[j], 0, 0)),
                  pl.BlockSpec((1, 1, de2), lambda j, be, used: (be[j], 0, 0)),
                  pl.BlockSpec((None, 1, de2 // 2, d), lambda j, be, used: (layer, be[j], 0, 0)),
                  pl.BlockSpec((1, 1, d), lambda j, be, used: (be[j], 0, 0))],
        out_specs=pl.BlockSpec((blk, dw), lambda j, be, used: (j, 0)),
    )
    return pl.pallas_call(
        _expert_kernel,
        grid_spec=gs,
        out_shape=jax.ShapeDtypeStruct((p, dw), U32),
        compiler_params=_cparams(("arbitrary",), VMEM_LIMIT),
        name="moe_experts",
    )(block_e, used, xs, w_gu_all, b_gu.reshape(ne, 1, de2), w_down_all, b_down.reshape(ne, 1, d))


def _combine_kernel(dest_ref, dest_next_ref, ys_ref, gate_ref, x_ref, g2_ref, lg_ref, lb_ref, o_ref, buf, sems,
                    *, T, K):
    i = pl.program_id(0)
    n = pl.num_programs(0)
    slot = i % 2

    def row_copy(idx_ref, dst_slot, r, kk):
        d = idx_ref[r * K + kk]
        return pltpu.make_async_copy(ys_ref.at[pl.ds(d, 1)], buf.at[dst_slot, kk, pl.ds(r, 1)],
                                     sems.at[dst_slot])

    def issue_tile(idx_ref, dst_slot):
        def issue(r, carry):
            for kk in range(K):
                row_copy(idx_ref, dst_slot, r, kk).start(priority=kk % 2)
            return carry
        lax.fori_loop(0, T, issue, 0, unroll=DMA_UNROLL)

    @pl.when(i == 0)
    def _():
        issue_tile(dest_ref, slot)

    @pl.when(i + 1 < n)
    def _():
        issue_tile(dest_next_ref, 1 - slot)

    def drain(r, carry):
        for kk in range(K):
            row_copy(dest_ref, slot, r, kk).wait()
        return carry

    lax.fori_loop(0, T, drain, 0, unroll=DMA_UNROLL)
    g = gate_ref[...]
    y_lo = y_hi = None
    for kk in range(K):
        lo, hi = _unpack_rows(buf[slot, kk])
        gk = g[:, kk:kk + 1]
        y_lo = gk * lo if y_lo is None else y_lo + gk * lo
        y_hi = gk * hi if y_hi is None else y_hi + gk * hi
    y = jnp.concatenate([y_lo, y_hi], axis=1)
    z = ALPHA * x_ref[...] + (1.0 + g2_ref[0]) * y
    o_ref[...] = _layer_norm(z, lg_ref[...], lb_ref[...])


def _combine(ys, dest_flat, gates, x1, g2, ln_g, ln_b, seq):
    n, d = x1.shape
    T = MOE_ROW_TILE
    tiles_per_seq = seq // T
    row = lambda i: (i, 0)
    last = n // T - 1
    return pl.pallas_call(
        functools.partial(_combine_kernel, T=T, K=TOP_K),
        grid=(n // T,),
        in_specs=[pl.BlockSpec((T * TOP_K,), lambda i: (i,), memory_space=pltpu.SMEM),
                  pl.BlockSpec((T * TOP_K,), lambda i: (jnp.minimum(i + 1, last),), memory_space=pltpu.SMEM),
                  pl.BlockSpec(memory_space=pl.ANY),
                  pl.BlockSpec((T, TOP_K), row), pl.BlockSpec((T, d), row),
                  pl.BlockSpec((1, 1, d), lambda i: (i // tiles_per_seq, 0, 0)),
                  _const_spec((1, d)), _const_spec((1, d))],
        out_specs=pl.BlockSpec((T, d), row),
        out_shape=jax.ShapeDtypeStruct((n, d), F32),
        scratch_shapes=[pltpu.VMEM((2, TOP_K, T, ys.shape[1]), ys.dtype), pltpu.SemaphoreType.DMA((2,))],
        compiler_params=_cparams(("arbitrary",), VMEM_LIMIT),
        name="moe_combine",
    )(dest_flat, dest_flat, ys, gates, x1, g2, ln_g.reshape(1, d), ln_b.reshape(1, d))


def _moe(h2, logits, x1, g2, ln_g, ln_b, layer, w_gu_all, b_gu, w_down_all, b_down, seq):
    n = h2.shape[0]
    blk = MOE_SLOT_BLOCK
    n_blocks = -(-(n * TOP_K + N_EXPERTS * (blk - 1)) // blk)
    dest, gates, meta, block_e = _route(logits, n_blocks)
    dest_flat = dest.reshape(n * TOP_K)
    xs = _dispatch(h2, dest_flat, meta, n_blocks)
    used = (meta[2, N_EXPERTS - 1] // blk).reshape(1)
    ys = _experts(xs, block_e[:n_blocks, 0], used, layer, w_gu_all, b_gu, w_down_all, b_down)
    return _combine(ys, dest_flat, gates, x1, g2, ln_g, ln_b, seq)


def kernel(x, c, positions, ada_w, ada_b, ln_g, ln_b, mla_w_in, mla_q_norm, mla_w_uq, mla_kv_norm, mla_w_ukv,
           mla_w_out, ml_w_in, ml_b_gates, ml_head_norm, ml_w_out, moe_w_router, moe_b_router, moe_w_gu,
           moe_b_gu, moe_w_down, moe_b_down):
    batch, seq, d = x.shape
    n = batch * seq
    mods = _mods(c, ada_w, ada_b)
    xf = x.reshape(n, d)
    w_gu_bf = moe_w_gu.astype(BF16)
    w_down_bf = moe_w_down.astype(BF16)
    for i in range(DEPTH):
        j = i // 2
        sh1, sc1, g1, sh2, sc2, g2 = [mods[i, :, s * d:(s + 1) * d].reshape(batch, 1, d) for s in range(6)]
        if i % 2 == 0:
            cos, sin = _rope_tables(positions)
            q, k, v = _mla_front(xf, sc1, sh1, cos, sin, mla_w_in[j], mla_q_norm[j], mla_w_uq[j],
                                 mla_kv_norm[j], mla_w_ukv[j], batch, seq)
            a = _flash_attention(q, k, v).reshape(n, MLA_HEADS * MLA_V)
            w_out = mla_w_out[j]
        else:
            proj, gcol, grow = _ml_inproj(xf, sc1, sh1, ml_w_in[j], ml_b_gates[j], seq)
            a = _mlstm_scan(proj, gcol, grow, ml_head_norm[j], batch, seq)
            w_out = ml_w_out[j]
        x1, h2, logits = _outproj_ln_router(a, xf, w_out, g1, ln_g[i, 0], ln_b[i, 0], sc2, sh2,
                                            moe_w_router[i], moe_b_router[i], seq)
        xf = _moe(h2, logits, x1, g2, ln_g[i, 1], ln_b[i, 1], i, w_gu_bf, moe_b_gu[i], w_down_bf,
                  moe_b_down[i], seq)
    return xf.reshape(batch, seq, d)
```

```python
import functools
import math

import jax
import jax.numpy as jnp
from jax import lax
from jax.experimental import pallas as pl
from jax.experimental.pallas import tpu as pltpu

F32 = jnp.float32
BF16 = jnp.bfloat16
I32 = jnp.int32
U32 = jnp.uint32
HIGHEST = lax.Precision.HIGHEST
HIGH_HALF = 0xFFFF0000

CHUNK = 64
MLA_HEADS = 16
MLA_NOPE = 128
MLA_ROPE = 64
MLA_V = 128
MLA_Q_RANK = 448
MLA_KV_RANK = 512
ROPE_THETA = 10000.0
ML_HEADS = 4
ML_QK = 256
ML_V = 512
GATE_SOFTCAP = 15.0
N_EXPERTS = 32
TOP_K = 4
D_EXPERT = 1024
SWIGLU_LIMIT = 7.0
SWIGLU_ALPHA = 1.702
DEPTH = 2
ALPHA = (2 * DEPTH) ** 0.25
EPS = 1e-6
NEG_BIG = -1e30

LANES = 128
VMEM_LIMIT = 56 * 1024 * 1024

Q_RANK_PAD = 512
ATT_BLOCK = 512
ATT_KBLOCK = 1024
ATT_CHAINS = 8
ML_CHUNK = 256
ML_HEADS_PER_STEP = 4
MOE_SLOT_BLOCK = 512
CAST_CHUNKS = 256
ROUTE_TILE = 1024
ROW_TILE = 256
MOE_ROW_TILE = 512
OUTPROJ_TILE = 512
DMA_UNROLL = 8


def _cparams(sem, vmem=None):
    return pltpu.CompilerParams(dimension_semantics=sem, vmem_limit_bytes=vmem)


def _const_spec(shape):
    nd = len(shape)
    return pl.BlockSpec(shape, lambda *_: (0,) * nd, pipeline_mode=pl.Buffered(1))


def _split_bf16(x):
    hi = x.astype(BF16)
    lo = (x - hi.astype(F32)).astype(BF16)
    return hi, lo


def _dot(a, b):
    return jnp.dot(a, b, preferred_element_type=F32)


def _dot_nt(a, b):
    return lax.dot_general(a, b, (((1,), (1,)), ((), ())), preferred_element_type=F32)


def _dot_split(x_hi, x_lo, w_hilo):
    p = _dot(x_hi, w_hilo)
    return p[:, :LANES] + (p[:, LANES:] + _dot(x_lo, w_hilo[:, :LANES]))


def _pack_rows(x):
    w = x.shape[1] // 2
    lo = pltpu.bitcast(x[:, :w].astype(BF16).astype(F32), U32)
    hi = pltpu.bitcast(x[:, w:].astype(BF16).astype(F32), U32)
    return (hi & jnp.uint32(HIGH_HALF)) | lax.shift_right_logical(lo, jnp.uint32(16))


def _unpack_rows(words):
    lo = pltpu.bitcast(lax.shift_left(words, jnp.uint32(16)), F32)
    hi = pltpu.bitcast(words & jnp.uint32(HIGH_HALF), F32)
    return lo, hi


def _layer_norm(z, g, b):
    mu = jnp.mean(z, axis=-1, keepdims=True)
    zc = z - mu
    var = jnp.mean(zc * zc, axis=-1, keepdims=True)
    return zc * lax.rsqrt(var + EPS) * g + b


def _mods_kernel(c_ref, w_ref, b_ref, o_ref):
    c = c_ref[...]
    ca = c / (1.0 + jnp.exp(-c))
    o_ref[0] = jnp.dot(ca, w_ref[0], precision=HIGHEST, preferred_element_type=F32) + b_ref[0]


def _mods(c, ada_w, ada_b):
    depth, d, d6 = ada_w.shape
    b = c.shape[0]
    rows = 8
    c8 = jnp.pad(c, ((0, rows - b), (0, 0)))
    tn = 1024
    out = pl.pallas_call(
        _mods_kernel,
        grid=(depth, d6 // tn),
        in_specs=[pl.BlockSpec((rows, d), lambda i, j: (0, 0)),
                  pl.BlockSpec((1, d, tn), lambda i, j: (i, 0, j)),
                  pl.BlockSpec((1, 1, tn), lambda i, j: (i, 0, j))],
        out_specs=pl.BlockSpec((1, rows, tn), lambda i, j: (i, 0, j)),
        out_shape=jax.ShapeDtypeStruct((depth, rows, d6), F32),
        compiler_params=_cparams(("arbitrary", "arbitrary"), VMEM_LIMIT),
        name="adaln_mods",
    )(c8, ada_w, ada_b.reshape(depth, 1, d6))
    return out[:, :b]


def _rope_kernel(pos_ref, inv_ref, cos_ref, sin_ref):
    ang = pos_ref[...].astype(F32) * inv_ref[...]
    cos_ref[...] = jnp.cos(ang)
    sin_ref[...] = jnp.sin(ang)


def _rope_tables(positions):
    n = positions.size
    half = MLA_ROPE // 2
    inv = ROPE_THETA ** (-jnp.arange(0, MLA_ROPE, 2, dtype=F32) / MLA_ROPE)
    inv_row = jnp.tile(inv, LANES // half).reshape(1, LANES)
    pos = jnp.broadcast_to(positions.reshape(n, 1), (n, LANES))
    tm = 1024
    spec = pl.BlockSpec((tm, LANES), lambda i: (i, 0))
    return pl.pallas_call(
        _rope_kernel,
        grid=(n // tm,),
        in_specs=[spec, pl.BlockSpec((1, LANES), lambda i: (0, 0))],
        out_specs=[spec, spec],
        out_shape=[jax.ShapeDtypeStruct((n, LANES), F32)] * 2,
        compiler_params=_cparams(("arbitrary",)),
        name="rope_tables",
    )(pos, inv_row)


def _mla_front_kernel(x_ref, sc_ref, sh_ref, cos_ref, sin_ref, win_ref, qn_ref, kvn_ref, wq_ref, wkv_ref,
                      q_ref, k_ref, v_ref, *, qscale):
    h = (x_ref[...] * (1.0 + sc_ref[0]) + sh_ref[0]).astype(BF16)
    proj = _dot(h, win_ref[...])
    cq = proj[:, :Q_RANK_PAD]
    ckv = proj[:, Q_RANK_PAD:Q_RANK_PAD + MLA_KV_RANK]
    cq = cq * lax.rsqrt(jnp.sum(cq * cq, -1, keepdims=True) * (1.0 / MLA_Q_RANK) + EPS) * qn_ref[...]
    ckv = ckv * lax.rsqrt(jnp.mean(ckv * ckv, -1, keepdims=True) + EPS) * kvn_ref[...]
    cq = cq.astype(BF16)
    ckv = ckv.astype(BF16)
    cos = cos_ref[...]
    sin = sin_ref[...]
    o = Q_RANK_PAD + MLA_KV_RANK
    kr = proj[:, o:o + LANES] * cos + proj[:, o + LANES:o + 2 * LANES] * sin
    lane = lax.broadcasted_iota(I32, kr.shape, 1)
    kr_even = jnp.where(lane < MLA_ROPE, kr, 0.0).astype(BF16)
    kr_odd = jnp.where(lane >= MLA_ROPE, kr, 0.0).astype(BF16)
    ones_col = jnp.where(lane == 0, 1.0, 0.0).astype(BF16)

    nheads = MLA_HEADS
    nope_w = nheads * MLA_NOPE
    rope_w = nheads * MLA_ROPE
    group = 4
    for g in range(nheads // group):
        qn = _dot(cq, wq_ref[:, g * group * MLA_NOPE:(g + 1) * group * MLA_NOPE]) * qscale
        kv = _dot(ckv, wkv_ref[:, g * group * 2 * LANES:(g + 1) * group * 2 * LANES])
        for j in range(group):
            hh = g * group + j
            q_ref[0, hh, :, :LANES] = qn[:, j * LANES:(j + 1) * LANES].astype(BF16)
            k_ref[0, hh, :, :LANES] = kv[:, 2 * j * LANES:(2 * j + 1) * LANES].astype(BF16)
            v_ref[0, hh, :, :LANES] = kv[:, (2 * j + 1) * LANES:(2 * j + 2) * LANES].astype(BF16)
            k_ref[0, hh, :, LANES:] = kr_even if hh % 2 == 0 else kr_odd
            v_ref[0, hh, :, LANES:] = ones_col
    for p in range(nheads // 2):
        lo = nope_w + p * LANES
        qr = (_dot(cq, wq_ref[:, lo:lo + LANES]) * cos
              + _dot(cq, wq_ref[:, rope_w + lo:rope_w + lo + LANES]) * sin) * qscale
        qr = qr.astype(BF16)
        q_ref[0, 2 * p, :, LANES:] = qr
        q_ref[0, 2 * p + 1, :, LANES:] = qr


def _mla_front(x2d, sc, sh, cos, sin, w_in, q_norm, w_uq, kv_norm, w_ukv, batch, seq):
    n, d = x2d.shape
    nh = MLA_HEADS
    half = MLA_ROPE // 2
    wq_lat = jnp.pad(w_in[:, :MLA_Q_RANK], ((0, 0), (0, Q_RANK_PAD - MLA_Q_RANK)))
    wkv_lat = w_in[:, MLA_Q_RANK:MLA_Q_RANK + MLA_KV_RANK]
    wkr = w_in[:, MLA_Q_RANK + MLA_KV_RANK:]
    wkr_rot = jnp.concatenate([-wkr[:, half:], wkr[:, :half]], 1)
    win = jnp.concatenate([wq_lat, wkv_lat, wkr, wkr, wkr_rot, wkr_rot], 1).astype(BF16)
    wq3 = w_uq.reshape(MLA_Q_RANK, nh, MLA_NOPE + MLA_ROPE)
    wq_nope = wq3[:, :, :MLA_NOPE].reshape(MLA_Q_RANK, nh * MLA_NOPE)
    wq_r = wq3[:, :, MLA_NOPE:]
    wq_rope = wq_r.reshape(MLA_Q_RANK, nh * MLA_ROPE)
    wq_rot = jnp.concatenate([-wq_r[:, :, half:], wq_r[:, :, :half]], -1).reshape(MLA_Q_RANK, nh * MLA_ROPE)
    wq = jnp.pad(jnp.concatenate([wq_nope, wq_rope, wq_rot], 1),
                 ((0, Q_RANK_PAD - MLA_Q_RANK), (0, 0))).astype(BF16)
    wkv = w_ukv.astype(BF16)
    qn = jnp.pad(q_norm, (0, Q_RANK_PAD - MLA_Q_RANK)).reshape(1, Q_RANK_PAD)
    kvn = kv_norm.reshape(1, MLA_KV_RANK)
    qscale = (MLA_NOPE + MLA_ROPE) ** -0.5 * math.log2(math.e)

    tm = ROW_TILE
    tiles_per_seq = seq // tm
    row = lambda i: (i, 0)
    per_b = lambda i: (i // tiles_per_seq, 0, 0)
    head_out = pl.BlockSpec((1, nh, tm, 2 * LANES), lambda i: (i // tiles_per_seq, 0, i % tiles_per_seq, 0))
    out_sds = jax.ShapeDtypeStruct((batch, nh, seq, 2 * LANES), BF16)
    return pl.pallas_call(
        functools.partial(_mla_front_kernel, qscale=qscale),
        grid=(n // tm,),
        in_specs=[pl.BlockSpec((tm, d), row),
                  pl.BlockSpec((1, 1, d), per_b), pl.BlockSpec((1, 1, d), per_b),
                  pl.BlockSpec((tm, LANES), row), pl.BlockSpec((tm, LANES), row),
                  _const_spec(win.shape), _const_spec(qn.shape), _const_spec(kvn.shape),
                  _const_spec(wq.shape), _const_spec(wkv.shape)],
        out_specs=[head_out, head_out, head_out],
        out_shape=[out_sds, out_sds, out_sds],
        compiler_params=_cparams(("arbitrary",), VMEM_LIMIT),
        name="mla_front",
    )(x2d, sc, sh, cos, sin, win, qn, kvn, wq, wkv)


def _flash_kernel(q_ref, k_ref, v_ref, o_ref, m_ref, acc_ref, *, blk, kblk, nsub):
    i = pl.program_id(2)
    m_ref[...] = jnp.full(m_ref.shape, NEG_BIG, F32)
    acc_ref[...] = jnp.zeros(acc_ref.shape, F32)
    slabs = kblk // LANES
    ratio = kblk // blk
    shift = CHUNK.bit_length() - 1
    everyone = range(nsub)

    def step(c, kb, masked):
        start = pl.multiple_of(kb * kblk, kblk)
        q = q_ref[0, 0, c * blk:(c + 1) * blk, :]
        k = k_ref[0, 0, pl.ds(start, kblk), :]
        v = v_ref[0, 0, pl.ds(start, kblk), :]
        s = _dot_nt(q, k)
        if masked:
            r = lax.shift_right_logical(lax.broadcasted_iota(I32, s.shape, 0) + (c % ratio) * blk, shift)
            cc = lax.shift_right_logical(lax.broadcasted_iota(I32, s.shape, 1), shift)
            s = jnp.where(cc <= r, s, NEG_BIG)
        m_prev = m_ref[c]
        m_new = jnp.maximum(m_prev, jnp.max(s, axis=1, keepdims=True))
        p = jnp.concatenate([jnp.exp2(s[:, j * LANES:(j + 1) * LANES] - m_new).astype(BF16)
                             for j in range(slabs)], axis=1)
        alpha = jnp.exp2(m_prev - m_new)
        pv = _dot(p, v)
        acc_ref[c, :, :LANES] = alpha * acc_ref[c, :, :LANES] + pv[:, :LANES]
        acc_ref[c, :, LANES:] = alpha * acc_ref[c, :, LANES:] + pv[:, LANES:]
        m_ref[c] = m_new

    def full_steps(kb, carry):
        for c in everyone:
            step(c, kb, False)
        return carry

    first_diag = (nsub // ratio) * i
    lax.fori_loop(0, first_diag, full_steps, 0)
    for t in range(nsub // ratio):
        for c in everyone:
            if c // ratio >= t:
                step(c, first_diag + t, masked=(c // ratio == t))
    for c in everyone:
        l = acc_ref[c, :, LANES:LANES + 1]
        o_ref[0, c * blk:(c + 1) * blk, :] = (acc_ref[c, :, :MLA_V] / l).astype(o_ref.dtype)


def _flash_attention(q, k, v):
    batch, nh, seq, dk = q.shape
    blk = ATT_BLOCK
    nsub = ATT_CHAINS
    kblk = ATT_KBLOCK
    assert kblk % blk == 0 and nsub % (kblk // blk) == 0
    qspec = pl.BlockSpec((1, 1, nsub * blk, dk), lambda b, h, i: (b, h, i, 0))
    kvspec = pl.BlockSpec((1, 1, seq, dk), lambda b, h, i: (b, h, 0, 0))
    return pl.pallas_call(
        functools.partial(_flash_kernel, blk=blk, kblk=kblk, nsub=nsub),
        grid=(batch, nh, seq // (nsub * blk)),
        in_specs=[qspec, kvspec, kvspec],
        out_specs=pl.BlockSpec((1, nsub * blk, MLA_V), lambda b, h, i: (b, i, h)),
        out_shape=jax.ShapeDtypeStruct((batch, seq, nh * MLA_V), BF16),
        scratch_shapes=[pltpu.VMEM((nsub, blk, LANES), F32), pltpu.VMEM((nsub, blk, dk), F32)],
        compiler_params=_cparams(("arbitrary", "arbitrary", "arbitrary"), VMEM_LIMIT),
        name="flash_attention",
    )(q, k, v)


def _outproj_kernel(a_ref, x_ref, g1_ref, lg_ref, lb_ref, sc_ref, sh_ref, w_ref, rw_ref, rb_ref,
                    x1_ref, h2_ref, logit_ref):
    y = _dot(a_ref[...], w_ref[...])
    z = ALPHA * x_ref[...] + (1.0 + g1_ref[0]) * y
    x1 = _layer_norm(z, lg_ref[...], lb_ref[...])
    x1_ref[...] = x1
    h2 = x1 * (1.0 + sc_ref[0]) + sh_ref[0]
    h2_ref[...] = _pack_rows(h2)
    h2_hi, h2_lo = _split_bf16(h2)
    logit_ref[...] = _dot_split(h2_hi, h2_lo, rw_ref[...]) + rb_ref[...]


def _outproj_ln_router(a2d, x2d, w_out, g1, ln_g, ln_b, sc2, sh2, w_router, b_router, seq):
    n, d = x2d.shape
    da = a2d.shape[1]
    ne = w_router.shape[1]
    w = w_out.astype(BF16)
    rw = jnp.pad(w_router, ((0, 0), (0, LANES - ne)))
    rw_hilo = jnp.concatenate(_split_bf16(rw), axis=1)
    rb = jnp.pad(b_router, (0, LANES - ne), constant_values=NEG_BIG).reshape(1, LANES)
    tm = OUTPROJ_TILE
    tiles_per_seq = seq // tm
    row = lambda i: (i, 0)
    per_b = lambda i: (i // tiles_per_seq, 0, 0)
    vec = pl.BlockSpec((1, 1, d), per_b)
    return pl.pallas_call(
        _outproj_kernel,
        grid=(n // tm,),
        in_specs=[pl.BlockSpec((tm, da), row), pl.BlockSpec((tm, d), row), vec,
                  _const_spec((1, d)), _const_spec((1, d)), vec, vec,
                  _const_spec(w.shape), _const_spec(rw_hilo.shape), _const_spec(rb.shape)],
        out_specs=[pl.BlockSpec((tm, d), row), pl.BlockSpec((tm, d // 2), row), pl.BlockSpec((tm, LANES), row)],
        out_shape=[jax.ShapeDtypeStruct((n, d), F32), jax.ShapeDtypeStruct((n, d // 2), U32),
                   jax.ShapeDtypeStruct((n, LANES), F32)],
        compiler_params=_cparams(("arbitrary",), VMEM_LIMIT),
        name="outproj_ln_router",
    )(a2d, x2d, g1, ln_g.reshape(1, d), ln_b.reshape(1, d), sc2, sh2, w, rw_hilo, rb)


def _gate_log(g, is_input_gate):
    g = GATE_SOFTCAP * jnp.tanh(g * (1.0 / GATE_SOFTCAP))
    log_f = jnp.minimum(g, 0.0) - jnp.log(1.0 + jnp.exp(-jnp.abs(g)))
    return jnp.where(is_input_gate, g, log_f)


def _ml_inproj_kernel(x_ref, sc_ref, sh_ref, w_ref, gw_ref, gb_ref, gwth_ref, gwtl_ref, gbt_ref,
                      proj_ref, gcol_ref, grow_ref, h_scr):
    j = pl.program_id(1)

    @pl.when(j == 0)
    def _():
        h = x_ref[...] * (1.0 + sc_ref[0]) + sh_ref[0]
        h_hi, h_lo = _split_bf16(h)
        h_scr[...] = h_hi
        nh = ML_HEADS
        g = _dot_split(h_hi, h_lo, gw_ref[...]) + gb_ref[...]
        lane = lax.broadcasted_iota(I32, g.shape, 1)
        gcol_ref[...] = _gate_log(g, lane < nh)
        gt = (_dot_nt(gwth_ref[...], h_hi) + (_dot_nt(gwth_ref[...], h_lo) + _dot_nt(gwtl_ref[...], h_hi))
              + gbt_ref[...])
        sub = lax.broadcasted_iota(I32, gt.shape, 0)
        grow_ref[...] = _gate_log(gt, sub < nh)

    proj_ref[...] = _dot(h_scr[...], w_ref[...]).astype(proj_ref.dtype)


def _ml_inproj(x2d, sc, sh, w_in, b_gates, seq):
    n, d = x2d.shape
    ng = 2 * ML_HEADS
    wide = w_in.shape[1] - ng
    w = w_in[:, :wide].astype(BF16)
    gw = w_in[:, wide:]
    gw_hilo = jnp.concatenate(_split_bf16(jnp.pad(gw, ((0, 0), (0, LANES - ng)))), axis=1)
    gwt_hi, gwt_lo = _split_bf16(gw.T)
    gb = jnp.pad(b_gates, (0, LANES - ng)).reshape(1, LANES)
    gbt = b_gates.reshape(ng, 1)
    tm, tn = 1024, 1024
    tiles_per_seq = seq // tm
    row = lambda i, j: (i, 0)
    per_b = lambda i, j: (i // tiles_per_seq, 0, 0)
    const = lambda shape: pl.BlockSpec(shape, lambda i, j: (0,) * len(shape), pipeline_mode=pl.Buffered(1))
    return pl.pallas_call(
        _ml_inproj_kernel,
        grid=(n // tm, wide // tn),
        in_specs=[pl.BlockSpec((tm, d), row),
                  pl.BlockSpec((1, 1, d), per_b), pl.BlockSpec((1, 1, d), per_b),
                  pl.BlockSpec((d, tn), lambda i, j: (0, j)),
                  const(gw_hilo.shape), const(gb.shape),
                  const(gwt_hi.shape), const(gwt_lo.shape), const(gbt.shape)],
        out_specs=[pl.BlockSpec((tm, tn), lambda i, j: (i, j)),
                   pl.BlockSpec((tm, LANES), row),
                   pl.BlockSpec((ng, tm), lambda i, j: (0, i))],
        out_shape=[jax.ShapeDtypeStruct((n, wide), BF16), jax.ShapeDtypeStruct((n, LANES), F32),
                   jax.ShapeDtypeStruct((ng, n), F32)],
        scratch_shapes=[pltpu.VMEM((tm, d), BF16)],
        compiler_params=_cparams(("arbitrary", "arbitrary"), VMEM_LIMIT),
        name="mlstm_inproj",
    )(x2d, sc, sh, w, gw_hilo, gb, gwt_hi, gwt_lo, gbt)


def _split3_bf16(x):
    a = x.astype(BF16)
    r = x - a.astype(F32)
    b = r.astype(BF16)
    c = (r - b.astype(F32)).astype(BF16)
    return a, b, c


def _mlstm_kernel(q_ref, k_ref, v_ref, og_ref, gc_ref, gr_ref, hn_ref, o_ref, c_scr, m_scr, *, L, G):
    ci = pl.program_id(2)

    @pl.when(ci == 0)
    def _():
        c_scr[...] = jnp.zeros(c_scr.shape, F32)
        m_scr[...] = jnp.zeros(m_scr.shape, F32)

    row = lax.broadcasted_iota(I32, (L, L), 0)
    col = lax.broadcasted_iota(I32, (L, L), 1)
    causal = col <= row
    tri = jnp.where(causal, 1.0, 0.0).astype(BF16)
    ones_col = jnp.where(lax.broadcasted_iota(I32, (L, LANES), 1) == 0, 1.0, 0.0).astype(BF16)
    for g in range(G):
        _mlstm_head(q_ref[:, g * ML_QK:(g + 1) * ML_QK], k_ref[:, g * ML_QK:(g + 1) * ML_QK],
                    v_ref[:, g * ML_V:(g + 1) * ML_V], og_ref[:, g * ML_V:(g + 1) * ML_V],
                    gc_ref[g], gr_ref[g], hn_ref[:, g * ML_V:(g + 1) * ML_V],
                    o_ref.at[:, g * ML_V:(g + 1) * ML_V], c_scr.at[g], m_scr.at[g],
                    causal, tri, ones_col, L)


def _mlstm_head(q, k, v, og, gc, gr, head_gain, o_ref, c_scr, m_scr, causal, tri, ones_col, L):
    v_ext = jnp.concatenate([v, ones_col], axis=1)
    li_c, lf_c = gc[:, 0:1], gc[:, 1:2]
    li_r, lf_r = gr[0:1, :], gr[1:2, :]

    lfc3 = _split3_bf16(jnp.broadcast_to(lf_c, (L, LANES)))
    b_c = (_dot(tri, lfc3[0]) + (_dot(tri, lfc3[1]) + _dot(tri, lfc3[2])))[:, 0:1]
    lfr3 = _split3_bf16(jnp.broadcast_to(lf_r, (8, L)))
    b_r = (_dot_nt(lfr3[0], tri) + (_dot_nt(lfr3[1], tri) + _dot_nt(lfr3[2], tri)))[0:1, :]

    m_prev = m_scr[...]
    dm = jnp.where(causal, b_c - (b_r - li_r), NEG_BIG)
    inter = b_c + m_prev
    m_t = jnp.maximum(inter, jnp.max(dm, axis=1, keepdims=True))
    w_intra = jnp.exp(dm - m_t)
    w_inter = jnp.exp(inter - m_t)

    kscale = ML_QK ** -0.5
    qk = _dot_nt(q, k) * (w_intra * kscale)
    c_state = c_scr[...]
    num_ext = w_inter * _dot(q, c_state.astype(BF16)) + _dot(qk.astype(BF16), v_ext)
    vdim = v.shape[1]
    num = num_ext[:, :vdim]
    den = num_ext[:, vdim:vdim + 1]
    h = num / jnp.maximum(jnp.abs(den), jnp.exp(-m_t))
    hn = h * lax.rsqrt(jnp.mean(h * h, axis=-1, keepdims=True) + EPS) * head_gain
    o_ref[...] = (hn / (1.0 + jnp.exp(-og.astype(F32)))).astype(o_ref.dtype)

    b_end = b_r[:, L - 1:L]
    d_end_r = b_end - b_r + li_r
    m_new = jnp.maximum(b_end + m_prev, jnp.max(d_end_r, axis=1, keepdims=True))
    decay = jnp.exp(b_end + m_prev - m_new)
    w_s = jnp.exp(b_end - b_c + li_c - m_new) * kscale
    kw = (k.astype(F32) * w_s).astype(BF16)
    upd = lax.dot_general(kw, v_ext, (((0,), (0,)), ((), ())), preferred_element_type=F32)
    c_scr[...] = decay * c_state + upd
    m_scr[...] = m_new


def _mlstm_scan(proj, gcol, grow, head_norm, batch, seq):
    n = proj.shape[0]
    nh = ML_HEADS
    L = ML_CHUNK
    nc = seq // L
    gc = gcol[:, :2 * nh].reshape(n, 2, nh).transpose(2, 0, 1)
    gr = grow.reshape(2, nh, n).transpose(1, 0, 2)
    hn = head_norm.reshape(1, nh * ML_V)
    G = ML_HEADS_PER_STEP
    groups = nh // G
    k_off = groups
    v_off = 2 * nh * ML_QK // (G * ML_V)
    og_off = v_off + groups
    tok = lambda b, h, c: b * nc + c
    return pl.pallas_call(
        functools.partial(_mlstm_kernel, L=L, G=G),
        grid=(batch, groups, nc),
        in_specs=[pl.BlockSpec((L, G * ML_QK), lambda b, h, c: (tok(b, h, c), h)),
                  pl.BlockSpec((L, G * ML_QK), lambda b, h, c: (tok(b, h, c), k_off + h)),
                  pl.BlockSpec((L, G * ML_V), lambda b, h, c: (tok(b, h, c), v_off + h)),
                  pl.BlockSpec((L, G * ML_V), lambda b, h, c: (tok(b, h, c), og_off + h)),
                  pl.BlockSpec((G, L, 2), lambda b, h, c: (h, tok(b, h, c), 0)),
                  pl.BlockSpec((G, 2, L), lambda b, h, c: (h, 0, tok(b, h, c))),
                  pl.BlockSpec((1, G * ML_V), lambda b, h, c: (0, h))],
        out_specs=pl.BlockSpec((L, G * ML_V), lambda b, h, c: (tok(b, h, c), h)),
        out_shape=jax.ShapeDtypeStruct((n, nh * ML_V), BF16),
        scratch_shapes=[pltpu.VMEM((G, ML_QK, ML_V + LANES), F32), pltpu.VMEM((G, 1, 1), F32)],
        compiler_params=_cparams(("arbitrary", "arbitrary", "arbitrary"), VMEM_LIMIT),
        name="mlstm_scan",
    )(proj, proj, proj, proj, gc, gr, hn)


def _route_kernel(lg_ref, dest_ref, gate_ref, meta_ref, blk_ref, tri_scr, cnt_scr, run_scr, pst_scr,
                  *, T, E, K, BLK, NBP):
    ph = pl.program_id(0)
    t = pl.program_id(1)
    nt = pl.num_programs(1)

    @pl.when((ph == 0) & (t == 0))
    def _():
        r = lax.broadcasted_iota(I32, (T, T), 0)
        c = lax.broadcasted_iota(I32, (T, T), 1)
        tri_scr[...] = jnp.where(c < r, 1.0, 0.0).astype(BF16)
        cnt_scr[...] = jnp.zeros(cnt_scr.shape, F32)

    lane = lax.broadcasted_iota(I32, (T, LANES), 1)
    l = jnp.where(lane < E, lg_ref[...], -jnp.inf)
    vals, hots = [], []
    for _ in range(K):
        mx = jnp.max(l, axis=1, keepdims=True)
        idx = jnp.min(jnp.where(l == mx, lane, LANES), axis=1, keepdims=True)
        hot = lane == idx
        vals.append(mx)
        hots.append(hot)
        l = jnp.where(hot, -jnp.inf, l)
    hot_all = jnp.zeros((T, LANES), F32)
    for hot in hots:
        hot_all = hot_all + jnp.where(hot, 1.0, 0.0)
    colsum = jnp.sum(hot_all, axis=0, keepdims=True)

    @pl.when(ph == 0)
    def _():
        cnt_scr[...] = cnt_scr[...] + colsum

    @pl.when((ph == 1) & (t == 0))
    def _():
        cnt = cnt_scr[...].astype(I32)
        pc = ((cnt + (BLK - 1)) & (-BLK)).astype(F32)
        r = lax.broadcasted_iota(I32, (LANES, LANES), 0)
        c = lax.broadcasted_iota(I32, (LANES, LANES), 1)
        upper = jnp.where(r < c, 1.0, 0.0)
        pstart = jnp.dot(jnp.broadcast_to(pc, (8, LANES)), upper, precision=HIGHEST,
                         preferred_element_type=F32)[0:1, :]
        pst_scr[...] = pstart
        run_scr[...] = jnp.zeros(run_scr.shape, F32)
        pend = pstart + pc
        meta_ref[0:1, :] = cnt_scr[...].astype(I32)
        meta_ref[1:2, :] = pstart.astype(I32)
        meta_ref[2:3, :] = pend.astype(I32)
        meta_ref[3:8, :] = jnp.zeros((5, LANES), I32)
        jstart = (lax.broadcasted_iota(I32, (NBP, LANES), 0) * BLK).astype(F32)
        elane = lax.broadcasted_iota(I32, (NBP, LANES), 1)
        owned = jnp.where((pend <= jstart) & (elane < E), 1.0, 0.0)
        be = jnp.minimum(jnp.sum(owned, axis=1, keepdims=True), float(E - 1))
        blk_ref[...] = jnp.broadcast_to(be, (NBP, LANES)).astype(I32)

    @pl.when(ph == 1)
    def _():
        earlier = _dot(tri_scr[...], hot_all.astype(BF16))
        pos = earlier + (pst_scr[...] + run_scr[...])
        e0 = jnp.ones_like(vals[0])
        es = [e0] + [jnp.exp(v - vals[0]) for v in vals[1:]]
        tot = es[0]
        for e in es[1:]:
            tot = tot + e
        for kk in range(K):
            d = jnp.sum(jnp.where(hots[kk], pos, 0.0), axis=1, keepdims=True)
            dest_ref[:, kk:kk + 1] = d.astype(I32)
            gate_ref[:, kk:kk + 1] = es[kk] / tot
        run_scr[...] = run_scr[...] + colsum


def _route(logits, n_blocks):
    n = logits.shape[0]
    T = min(ROUTE_TILE, n)
    nbp = -(-n_blocks // 8) * 8
    kern = functools.partial(_route_kernel, T=T, E=N_EXPERTS, K=TOP_K, BLK=MOE_SLOT_BLOCK, NBP=nbp)
    tok_out = lambda ph, t: (t * ph, 0)
    fixed = lambda ph, t: (0, 0)
    return pl.pallas_call(
        kern,
        grid=(2, n // T),
        in_specs=[pl.BlockSpec((T, LANES), lambda ph, t: (t, 0))],
        out_specs=[pl.BlockSpec((T, TOP_K), tok_out), pl.BlockSpec((T, TOP_K), tok_out),
                   pl.BlockSpec((8, LANES), fixed), pl.BlockSpec((nbp, LANES), fixed)],
        out_shape=[jax.ShapeDtypeStruct((n, TOP_K), I32), jax.ShapeDtypeStruct((n, TOP_K), F32),
                   jax.ShapeDtypeStruct((8, LANES), I32), jax.ShapeDtypeStruct((nbp, LANES), I32)],
        scratch_shapes=[pltpu.VMEM((T, T), BF16), pltpu.VMEM((1, LANES), F32), pltpu.VMEM((1, LANES), F32),
                        pltpu.VMEM((1, LANES), F32)],
        compiler_params=_cparams(("arbitrary", "arbitrary")),
        name="moe_route",
    )(logits)


def _dispatch_kernel(meta_ref, dest_ref, h_ref, xs_ref, zero_scr, sem, zsem, *, T, K, E, BLK, NB):
    i = pl.program_id(0)
    last = pl.num_programs(0) - 1

    def row_copy(r, kk):
        d = dest_ref[r * K + kk]
        return pltpu.make_async_copy(h_ref.at[pl.ds(r, 1)], xs_ref.at[pl.ds(d, 1)], sem)

    def issue(r, carry):
        for kk in range(K):
            row_copy(r, kk).start(priority=kk % 2)
        return carry

    lax.fori_loop(0, T, issue, 0, unroll=DMA_UNROLL)

    @pl.when(i == last)
    def _():
        zero_scr[...] = jnp.zeros(zero_scr.shape, zero_scr.dtype)

        def pad_copy(slot):
            return pltpu.make_async_copy(zero_scr.at[pl.ds(0, 1)], xs_ref.at[pl.ds(slot, 1)], zsem)

        for e in range(E):
            cnt = meta_ref[0, e]
            first = meta_ref[1, e] + cnt
            npad = meta_ref[2, e] - first

            def zissue(r, carry, first=first):
                pad_copy(first + r).start()
                return carry

            def zwait(r, carry, first=first):
                pad_copy(first + r).wait()
                return carry

            lax.fori_loop(0, npad, zissue, 0)
            lax.fori_loop(0, npad, zwait, 0)

        used = meta_ref[2, E - 1] // BLK

        def tail_copy(j):
            return pltpu.make_async_copy(zero_scr, xs_ref.at[pl.ds(pl.multiple_of(j * BLK, BLK), BLK)], zsem)

        def tissue(j, carry):
            tail_copy(j).start()
            return carry

        def twait(j, carry):
            tail_copy(j).wait()
            return carry

        lax.fori_loop(used, NB, tissue, 0)
        lax.fori_loop(used, NB, twait, 0)

    def drain(r, carry):
        for kk in range(K):
            row_copy(r, kk).wait()
        return carry

    lax.fori_loop(0, T, drain, 0, unroll=DMA_UNROLL)


def _dispatch(h2, dest_flat, meta, n_blocks):
    n, d = h2.shape
    T = MOE_ROW_TILE
    blk = MOE_SLOT_BLOCK
    kern = functools.partial(_dispatch_kernel, T=T, K=TOP_K, E=N_EXPERTS, BLK=blk, NB=n_blocks)
    gs = pltpu.PrefetchScalarGridSpec(
        num_scalar_prefetch=1,
        grid=(n // T,),
        in_specs=[pl.BlockSpec((T * TOP_K,), lambda i, m: (i,), memory_space=pltpu.SMEM),
                  pl.BlockSpec((T, d), lambda i, m: (i, 0))],
        out_specs=pl.BlockSpec(memory_space=pl.ANY),
        scratch_shapes=[pltpu.VMEM((blk, d), h2.dtype), pltpu.SemaphoreType.DMA(()),
                        pltpu.SemaphoreType.DMA(())],
    )
    return pl.pallas_call(
        kern,
        grid_spec=gs,
        out_shape=jax.ShapeDtypeStruct((n_blocks * blk, d), h2.dtype),
        compiler_params=_cparams(("arbitrary",), VMEM_LIMIT),
        name="moe_dispatch",
    )(meta, dest_flat, h2)


def _expert_kernel(be_ref, used_ref, xs_ref, wgu_ref, bgu_ref, wd_ref, bd_ref, *rest):
    j = pl.program_id(0)
    if len(rest) == 1:
        (ys_ref,) = rest
    else:
        ngu_ref, ndn_ref, ys_ref, ogu_ref, odn_ref = rest
        ogu_ref[...] = ngu_ref[...].astype(BF16)
        odn_ref[...] = ndn_ref[...].astype(BF16)

    @pl.when(j < used_ref[0])
    def _():
        x_lo, x_hi = _unpack_rows(xs_ref[...])
        x = jnp.concatenate([x_lo.astype(BF16), x_hi.astype(BF16)], axis=1)
        gu = _dot(x, wgu_ref[0]) + bgu_ref[0]
        de = gu.shape[1] // 2
        a = jnp.minimum(gu[:, :de], SWIGLU_LIMIT)
        u = jnp.clip(gu[:, de:], -SWIGLU_LIMIT, SWIGLU_LIMIT)
        act = (u + 1.0) * (a / (1.0 + jnp.exp(-SWIGLU_ALPHA * a)))
        ys_ref[...] = _pack_rows(_dot(act.astype(BF16), wd_ref[0]) + bd_ref[0])

    @pl.when(j >= used_ref[0])
    def _():
        ys_ref[...] = jnp.zeros(ys_ref.shape, ys_ref.dtype)


def _experts(xs, block_e, used, w_gu, b_gu, w_down, b_down, cast_next=None):
    p, dw = xs.shape
    d = 2 * dw
    blk = MOE_SLOT_BLOCK
    ne, _, de2 = w_gu.shape
    nb = p // blk
    last_used = lambda j, be, used: jnp.minimum(j, jnp.maximum(used[0] - 1, 0))
    in_specs = [pl.BlockSpec((blk, dw), lambda j, be, used: (last_used(j, be, used), 0)),
                pl.BlockSpec((1, d, de2), lambda j, be, used: (be[j], 0, 0)),
                pl.BlockSpec((1, 1, de2), lambda j, be, used: (be[j], 0, 0)),
                pl.BlockSpec((1, de2 // 2, d), lambda j, be, used: (be[j], 0, 0)),
                pl.BlockSpec((1, 1, d), lambda j, be, used: (be[j], 0, 0))]
    out_specs = [pl.BlockSpec((blk, dw), lambda j, be, used: (j, 0))]
    out_shape = [jax.ShapeDtypeStruct((p, dw), U32)]
    operands = [block_e, used, xs, w_gu, b_gu.reshape(ne, 1, de2), w_down, b_down.reshape(ne, 1, d)]
    if cast_next is not None:
        gu_all, down_all, layer = cast_next
        chunks = 1 << (min(nb, CAST_CHUNKS).bit_length() - 1)
        for w_all in (gu_all, down_all):
            rows = w_all.shape[1] * w_all.shape[2]
            cols = w_all.shape[3]
            chunk = rows // chunks
            first = layer * chunks
            in_specs.append(pl.BlockSpec(
                (chunk, cols), lambda j, be, used, first=first: (first + jnp.minimum(j, chunks - 1), 0)))
            out_specs.append(pl.BlockSpec(
                (chunk, cols), lambda j, be, used: (jnp.minimum(j, chunks - 1), 0)))
            out_shape.append(jax.ShapeDtypeStruct((rows, cols), BF16))
            operands.append(w_all.reshape(w_all.shape[0] * rows, cols))
    gs = pltpu.PrefetchScalarGridSpec(num_scalar_prefetch=2, grid=(nb,), in_specs=in_specs, out_specs=out_specs)
    outs = pl.pallas_call(
        _expert_kernel,
        grid_spec=gs,
        out_shape=out_shape,
        compiler_params=_cparams(("arbitrary",), VMEM_LIMIT),
        name="moe_experts",
    )(*operands)
    if cast_next is None:
        return outs[0], None
    gu_all, down_all, _ = cast_next
    return outs[0], (outs[1].reshape(gu_all.shape[1:]), outs[2].reshape(down_all.shape[1:]))


def _combine_kernel(dest_ref, dest_next_ref, ys_ref, gate_ref, x_ref, g2_ref, lg_ref, lb_ref, o_ref, buf, sems,
                    *, T, K):
    i = pl.program_id(0)
    n = pl.num_programs(0)
    slot = i % 2

    def row_copy(idx_ref, dst_slot, r, kk):
        d = idx_ref[r * K + kk]
        return pltpu.make_async_copy(ys_ref.at[pl.ds(d, 1)], buf.at[dst_slot, kk, pl.ds(r, 1)],
                                     sems.at[dst_slot])

    def issue_tile(idx_ref, dst_slot):
        def issue(r, carry):
            for kk in range(K):
                row_copy(idx_ref, dst_slot, r, kk).start(priority=kk % 2)
            return carry
        lax.fori_loop(0, T, issue, 0, unroll=DMA_UNROLL)

    @pl.when(i == 0)
    def _():
        issue_tile(dest_ref, slot)

    @pl.when(i + 1 < n)
    def _():
        issue_tile(dest_next_ref, 1 - slot)

    def drain(r, carry):
        for kk in range(K):
            row_copy(dest_ref, slot, r, kk).wait()
        return carry

    lax.fori_loop(0, T, drain, 0, unroll=DMA_UNROLL)
    g = gate_ref[...]
    y_lo = y_hi = None
    for kk in range(K):
        lo, hi = _unpack_rows(buf[slot, kk])
        gk = g[:, kk:kk + 1]
        y_lo = gk * lo if y_lo is None else y_lo + gk * lo
        y_hi = gk * hi if y_hi is None else y_hi + gk * hi
    y = jnp.concatenate([y_lo, y_hi], axis=1)
    z = ALPHA * x_ref[...] + (1.0 + g2_ref[0]) * y
    o_ref[...] = _layer_norm(z, lg_ref[...], lb_ref[...])


def _combine(ys, dest_flat, gates, x1, g2, ln_g, ln_b, seq):
    n, d = x1.shape
    T = MOE_ROW_TILE
    tiles_per_seq = seq // T
    row = lambda i: (i, 0)
    last = n // T - 1
    return pl.pallas_call(
        functools.partial(_combine_kernel, T=T, K=TOP_K),
        grid=(n // T,),
        in_specs=[pl.BlockSpec((T * TOP_K,), lambda i: (i,), memory_space=pltpu.SMEM),
                  pl.BlockSpec((T * TOP_K,), lambda i: (jnp.minimum(i + 1, last),), memory_space=pltpu.SMEM),
                  pl.BlockSpec(memory_space=pl.ANY),
                  pl.BlockSpec((T, TOP_K), row), pl.BlockSpec((T, d), row),
                  pl.BlockSpec((1, 1, d), lambda i: (i // tiles_per_seq, 0, 0)),
                  _const_spec((1, d)), _const_spec((1, d))],
        out_specs=pl.BlockSpec((T, d), row),
        out_shape=jax.ShapeDtypeStruct((n, d), F32),
        scratch_shapes=[pltpu.VMEM((2, TOP_K, T, ys.shape[1]), ys.dtype), pltpu.SemaphoreType.DMA((2,))],
        compiler_params=_cparams(("arbitrary",), VMEM_LIMIT),
        name="moe_combine",
    )(dest_flat, dest_flat, ys, gates, x1, g2, ln_g.reshape(1, d), ln_b.reshape(1, d))


def _moe(h2, logits, x1, g2, ln_g, ln_b, w_gu, b_gu, w_down, b_down, seq, cast_next=None):
    n = h2.shape[0]
    blk = MOE_SLOT_BLOCK
    n_blocks = -(-(n * TOP_K + N_EXPERTS * (blk - 1)) // blk)
    dest, gates, meta, block_e = _route(logits, n_blocks)
    dest_flat = dest.reshape(n * TOP_K)
    xs = _dispatch(h2, dest_flat, meta, n_blocks)
    used = (meta[2, N_EXPERTS - 1] // blk).reshape(1)
    ys, next_weights = _experts(xs, block_e[:n_blocks, 0], used, w_gu, b_gu, w_down, b_down, cast_next)
    return _combine(ys, dest_flat, gates, x1, g2, ln_g, ln_b, seq), next_weights


def kernel(x, c, positions, ada_w, ada_b, ln_g, ln_b, mla_w_in, mla_q_norm, mla_w_uq, mla_kv_norm, mla_w_ukv,
           mla_w_out, ml_w_in, ml_b_gates, ml_head_norm, ml_w_out, moe_w_router, moe_b_router, moe_w_gu,
           moe_b_gu, moe_w_down, moe_b_down):
    batch, seq, d = x.shape
    n = batch * seq
    mods = _mods(c, ada_w, ada_b)
    xf = x.reshape(n, d)
    expert_w = (moe_w_gu[0].astype(BF16), moe_w_down[0].astype(BF16))
    for i in range(DEPTH):
        j = i // 2
        sh1, sc1, g1, sh2, sc2, g2 = [mods[i, :, s * d:(s + 1) * d].reshape(batch, 1, d) for s in range(6)]
        if i % 2 == 0:
            cos, sin = _rope_tables(positions)
            q, k, v = _mla_front(xf, sc1, sh1, cos, sin, mla_w_in[j], mla_q_norm[j], mla_w_uq[j],
                                 mla_kv_norm[j], mla_w_ukv[j], batch, seq)
            a = _flash_attention(q, k, v).reshape(n, MLA_HEADS * MLA_V)
            w_out = mla_w_out[j]
        else:
            proj, gcol, grow = _ml_inproj(xf, sc1, sh1, ml_w_in[j], ml_b_gates[j], seq)
            a = _mlstm_scan(proj, gcol, grow, ml_head_norm[j], batch, seq)
            w_out = ml_w_out[j]
        x1, h2, logits = _outproj_ln_router(a, xf, w_out, g1, ln_g[i, 0], ln_b[i, 0], sc2, sh2,
                                            moe_w_router[i], moe_b_router[i], seq)
        cast_next = (moe_w_gu, moe_w_down, i + 1) if i + 1 < DEPTH else None
        xf, expert_w = _moe(h2, logits, x1, g2, ln_g[i, 1], ln_b[i, 1], expert_w[0], moe_b_gu[i], expert_w[1],
                            moe_b_down[i], seq, cast_next)
    return xf.reshape(batch, seq, d)
```

```python
import functools
import math

import jax
import jax.numpy as jnp
from jax import lax
from jax.experimental import pallas as pl
from jax.experimental.pallas import tpu as pltpu

F32 = jnp.float32
BF16 = jnp.bfloat16
I32 = jnp.int32
U32 = jnp.uint32
HIGHEST = lax.Precision.HIGHEST
HIGH_HALF = 0xFFFF0000

CHUNK = 64
MLA_HEADS = 16
MLA_NOPE = 128
MLA_ROPE = 64
MLA_V = 128
MLA_Q_RANK = 448
MLA_KV_RANK = 512
ROPE_THETA = 10000.0
ML_HEADS = 4
ML_QK = 256
ML_V = 512
GATE_SOFTCAP = 15.0
N_EXPERTS = 32
TOP_K = 4
D_EXPERT = 1024
SWIGLU_LIMIT = 7.0
SWIGLU_ALPHA = 1.702
DEPTH = 2
ALPHA = (2 * DEPTH) ** 0.25
EPS = 1e-6
NEG_BIG = -1e30

LANES = 128
VMEM_LIMIT = 56 * 1024 * 1024

Q_RANK_PAD = 512
ATT_BLOCK = 512
ATT_KBLOCK = 1024
ATT_CHAINS = 8
ML_CHUNK = 256
ML_HEADS_PER_STEP = 4
MOE_SLOT_BLOCK = 512
CAST_CHUNKS = 256
ROUTE_TILE = 1024
ROW_TILE = 256
MOE_ROW_TILE = 512
OUTPROJ_TILE = 512
DMA_UNROLL = 8


def _cparams(sem, vmem=None):
    return pltpu.CompilerParams(dimension_semantics=sem, vmem_limit_bytes=vmem)


def _const_spec(shape):
    nd = len(shape)
    return pl.BlockSpec(shape, lambda *_: (0,) * nd, pipeline_mode=pl.Buffered(1))


def _split_bf16(x):
    hi = x.astype(BF16)
    lo = (x - hi.astype(F32)).astype(BF16)
    return hi, lo


def _dot(a, b):
    return jnp.dot(a, b, preferred_element_type=F32)


def _dot_nt(a, b):
    return lax.dot_general(a, b, (((1,), (1,)), ((), ())), preferred_element_type=F32)


def _dot_split(x_hi, x_lo, w_hilo):
    p = _dot(x_hi, w_hilo)
    return p[:, :LANES] + (p[:, LANES:] + _dot(x_lo, w_hilo[:, :LANES]))


def _pack_rows(x):
    w = x.shape[1] // 2
    lo = pltpu.bitcast(x[:, :w].astype(BF16).astype(F32), U32)
    hi = pltpu.bitcast(x[:, w:].astype(BF16).astype(F32), U32)
    return (hi & jnp.uint32(HIGH_HALF)) | lax.shift_right_logical(lo, jnp.uint32(16))


def _unpack_rows(words):
    lo = pltpu.bitcast(lax.shift_left(words, jnp.uint32(16)), F32)
    hi = pltpu.bitcast(words & jnp.uint32(HIGH_HALF), F32)
    return lo, hi


def _layer_norm(z, g, b):
    mu = jnp.mean(z, axis=-1, keepdims=True)
    zc = z - mu
    var = jnp.mean(zc * zc, axis=-1, keepdims=True)
    return zc * lax.rsqrt(var + EPS) * g + b


def _mods_kernel(c_ref, w_ref, b_ref, o_ref):
    c = c_ref[...]
    ca = c / (1.0 + jnp.exp(-c))
    o_ref[0] = jnp.dot(ca, w_ref[0], precision=HIGHEST, preferred_element_type=F32) + b_ref[0]


def _mods(c, ada_w, ada_b):
    depth, d, d6 = ada_w.shape
    b = c.shape[0]
    rows = 8
    c8 = jnp.pad(c, ((0, rows - b), (0, 0)))
    tn = 1024
    out = pl.pallas_call(
        _mods_kernel,
        grid=(depth, d6 // tn),
        in_specs=[pl.BlockSpec((rows, d), lambda i, j: (0, 0)),
                  pl.BlockSpec((1, d, tn), lambda i, j: (i, 0, j)),
                  pl.BlockSpec((1, 1, tn), lambda i, j: (i, 0, j))],
        out_specs=pl.BlockSpec((1, rows, tn), lambda i, j: (i, 0, j)),
        out_shape=jax.ShapeDtypeStruct((depth, rows, d6), F32),
        compiler_params=_cparams(("arbitrary", "arbitrary"), VMEM_LIMIT),
        name="adaln_mods",
    )(c8, ada_w, ada_b.reshape(depth, 1, d6))
    return out[:, :b]


def _rope_kernel(pos_ref, inv_ref, cos_ref, sin_ref):
    ang = pos_ref[...].astype(F32) * inv_ref[...]
    cos_ref[...] = jnp.cos(ang)
    sin_ref[...] = jnp.sin(ang)


def _rope_tables(positions):
    n = positions.size
    half = MLA_ROPE // 2
    inv = ROPE_THETA ** (-jnp.arange(0, MLA_ROPE, 2, dtype=F32) / MLA_ROPE)
    inv_row = jnp.tile(inv, LANES // half).reshape(1, LANES)
    pos = jnp.broadcast_to(positions.reshape(n, 1), (n, LANES))
    tm = 1024
    spec = pl.BlockSpec((tm, LANES), lambda i: (i, 0))
    return pl.pallas_call(
        _rope_kernel,
        grid=(n // tm,),
        in_specs=[spec, pl.BlockSpec((1, LANES), lambda i: (0, 0))],
        out_specs=[spec, spec],
        out_shape=[jax.ShapeDtypeStruct((n, LANES), F32)] * 2,
        compiler_params=_cparams(("arbitrary",)),
        name="rope_tables",
    )(pos, inv_row)


def _mla_front_kernel(x_ref, sc_ref, sh_ref, cos_ref, sin_ref, win_ref, qn_ref, kvn_ref, wq_ref, wkv_ref,
                      q_ref, k_ref, v_ref, *, qscale):
    h = (x_ref[...] * (1.0 + sc_ref[0]) + sh_ref[0]).astype(BF16)
    proj = _dot(h, win_ref[...])
    cq = proj[:, :Q_RANK_PAD]
    ckv = proj[:, Q_RANK_PAD:Q_RANK_PAD + MLA_KV_RANK]
    cq = cq * lax.rsqrt(jnp.sum(cq * cq, -1, keepdims=True) * (1.0 / MLA_Q_RANK) + EPS) * qn_ref[...]
    ckv = ckv * lax.rsqrt(jnp.mean(ckv * ckv, -1, keepdims=True) + EPS) * kvn_ref[...]
    cq = cq.astype(BF16)
    ckv = ckv.astype(BF16)
    cos = cos_ref[...]
    sin = sin_ref[...]
    o = Q_RANK_PAD + MLA_KV_RANK
    kr = proj[:, o:o + LANES] * cos + proj[:, o + LANES:o + 2 * LANES] * sin
    lane = lax.broadcasted_iota(I32, kr.shape, 1)
    kr_even = jnp.where(lane < MLA_ROPE, kr, 0.0).astype(BF16)
    kr_odd = jnp.where(lane >= MLA_ROPE, kr, 0.0).astype(BF16)
    ones_col = jnp.where(lane == 0, 1.0, 0.0).astype(BF16)

    nheads = MLA_HEADS
    nope_w = nheads * MLA_NOPE
    rope_w = nheads * MLA_ROPE
    group = 4
    for g in range(nheads // group):
        qn = _dot(cq, wq_ref[:, g * group * MLA_NOPE:(g + 1) * group * MLA_NOPE]) * qscale
        kv = _dot(ckv, wkv_ref[:, g * group * 2 * LANES:(g + 1) * group * 2 * LANES])
        for j in range(group):
            hh = g * group + j
            q_ref[0, hh, :, :LANES] = qn[:, j * LANES:(j + 1) * LANES].astype(BF16)
            k_ref[0, hh, :, :LANES] = kv[:, 2 * j * LANES:(2 * j + 1) * LANES].astype(BF16)
            v_ref[0, hh, :, :LANES] = kv[:, (2 * j + 1) * LANES:(2 * j + 2) * LANES].astype(BF16)
            k_ref[0, hh, :, LANES:] = kr_even if hh % 2 == 0 else kr_odd
            v_ref[0, hh, :, LANES:] = ones_col
    for p in range(nheads // 2):
        lo = nope_w + p * LANES
        qr = (_dot(cq, wq_ref[:, lo:lo + LANES]) * cos
              + _dot(cq, wq_ref[:, rope_w + lo:rope_w + lo + LANES]) * sin) * qscale
        qr = qr.astype(BF16)
        q_ref[0, 2 * p, :, LANES:] = qr
        q_ref[0, 2 * p + 1, :, LANES:] = qr


def _mla_front(x2d, sc, sh, cos, sin, w_in, q_norm, w_uq, kv_norm, w_ukv, batch, seq):
    n, d = x2d.shape
    nh = MLA_HEADS
    half = MLA_ROPE // 2
    wq_lat = jnp.pad(w_in[:, :MLA_Q_RANK], ((0, 0), (0, Q_RANK_PAD - MLA_Q_RANK)))
    wkv_lat = w_in[:, MLA_Q_RANK:MLA_Q_RANK + MLA_KV_RANK]
    wkr = w_in[:, MLA_Q_RANK + MLA_KV_RANK:]
    wkr_rot = jnp.concatenate([-wkr[:, half:], wkr[:, :half]], 1)
    win = jnp.concatenate([wq_lat, wkv_lat, wkr, wkr, wkr_rot, wkr_rot], 1).astype(BF16)
    wq3 = w_uq.reshape(MLA_Q_RANK, nh, MLA_NOPE + MLA_ROPE)
    wq_nope = wq3[:, :, :MLA_NOPE].reshape(MLA_Q_RANK, nh * MLA_NOPE)
    wq_r = wq3[:, :, MLA_NOPE:]
    wq_rope = wq_r.reshape(MLA_Q_RANK, nh * MLA_ROPE)
    wq_rot = jnp.concatenate([-wq_r[:, :, half:], wq_r[:, :, :half]], -1).reshape(MLA_Q_RANK, nh * MLA_ROPE)
    wq = jnp.pad(jnp.concatenate([wq_nope, wq_rope, wq_rot], 1),
                 ((0, Q_RANK_PAD - MLA_Q_RANK), (0, 0))).astype(BF16)
    wkv = w_ukv.astype(BF16)
    qn = jnp.pad(q_norm, (0, Q_RANK_PAD - MLA_Q_RANK)).reshape(1, Q_RANK_PAD)
    kvn = kv_norm.reshape(1, MLA_KV_RANK)
    qscale = (MLA_NOPE + MLA_ROPE) ** -0.5 * math.log2(math.e)

    tm = ROW_TILE
    tiles_per_seq = seq // tm
    row = lambda i: (i, 0)
    per_b = lambda i: (i // tiles_per_seq, 0, 0)
    head_out = pl.BlockSpec((1, nh, tm, 2 * LANES), lambda i: (i // tiles_per_seq, 0, i % tiles_per_seq, 0))
    out_sds = jax.ShapeDtypeStruct((batch, nh, seq, 2 * LANES), BF16)
    return pl.pallas_call(
        functools.partial(_mla_front_kernel, qscale=qscale),
        grid=(n // tm,),
        in_specs=[pl.BlockSpec((tm, d), row),
                  pl.BlockSpec((1, 1, d), per_b), pl.BlockSpec((1, 1, d), per_b),
                  pl.BlockSpec((tm, LANES), row), pl.BlockSpec((tm, LANES), row),
                  _const_spec(win.shape), _const_spec(qn.shape), _const_spec(kvn.shape),
                  _const_spec(wq.shape), _const_spec(wkv.shape)],
        out_specs=[head_out, head_out, head_out],
        out_shape=[out_sds, out_sds, out_sds],
        compiler_params=_cparams(("arbitrary",), VMEM_LIMIT),
        name="mla_front",
    )(x2d, sc, sh, cos, sin, win, qn, kvn, wq, wkv)


def _flash_kernel(q_ref, k_ref, v_ref, o_ref, m_ref, acc_ref, *, blk, kblk, nsub):
    i = pl.program_id(2)
    m_ref[...] = jnp.full(m_ref.shape, NEG_BIG, F32)
    acc_ref[...] = jnp.zeros(acc_ref.shape, F32)
    slabs = kblk // LANES
    ratio = kblk // blk
    shift = CHUNK.bit_length() - 1
    everyone = range(nsub)

    def step(c, kb, masked):
        start = pl.multiple_of(kb * kblk, kblk)
        q = q_ref[0, 0, c * blk:(c + 1) * blk, :]
        k = k_ref[0, 0, pl.ds(start, kblk), :]
        v = v_ref[0, 0, pl.ds(start, kblk), :]
        s = _dot_nt(q, k)
        if masked:
            r = lax.shift_right_logical(lax.broadcasted_iota(I32, s.shape, 0) + (c % ratio) * blk, shift)
            cc = lax.shift_right_logical(lax.broadcasted_iota(I32, s.shape, 1), shift)
            s = jnp.where(cc <= r, s, NEG_BIG)
        m_prev = m_ref[c]
        m_new = jnp.maximum(m_prev, jnp.max(s, axis=1, keepdims=True))
        p = jnp.concatenate([jnp.exp2(s[:, j * LANES:(j + 1) * LANES] - m_new).astype(BF16)
                             for j in range(slabs)], axis=1)
        alpha = jnp.exp2(m_prev - m_new)
        pv = _dot(p, v)
        acc_ref[c, :, :LANES] = alpha * acc_ref[c, :, :LANES] + pv[:, :LANES]
        acc_ref[c, :, LANES:] = alpha * acc_ref[c, :, LANES:] + pv[:, LANES:]
        m_ref[c] = m_new

    def full_steps(kb, carry):
        for c in everyone:
            step(c, kb, False)
        return carry

    first_diag = (nsub // ratio) * i
    lax.fori_loop(0, first_diag, full_steps, 0)
    for t in range(nsub // ratio):
        for c in everyone:
            if c // ratio >= t:
                step(c, first_diag + t, masked=(c // ratio == t))
    for c in everyone:
        l = acc_ref[c, :, LANES:LANES + 1]
        o_ref[0, c * blk:(c + 1) * blk, :] = (acc_ref[c, :, :MLA_V] / l).astype(o_ref.dtype)


def _flash_attention(q, k, v):
    batch, nh, seq, dk = q.shape
    blk = ATT_BLOCK
    nsub = ATT_CHAINS
    kblk = ATT_KBLOCK
    assert kblk % blk == 0 and nsub % (kblk // blk) == 0
    qspec = pl.BlockSpec((1, 1, nsub * blk, dk), lambda b, h, i: (b, h, i, 0))
    kvspec = pl.BlockSpec((1, 1, seq, dk), lambda b, h, i: (b, h, 0, 0))
    return pl.pallas_call(
        functools.partial(_flash_kernel, blk=blk, kblk=kblk, nsub=nsub),
        grid=(batch, nh, seq // (nsub * blk)),
        in_specs=[qspec, kvspec, kvspec],
        out_specs=pl.BlockSpec((1, nsub * blk, MLA_V), lambda b, h, i: (b, i, h)),
        out_shape=jax.ShapeDtypeStruct((batch, seq, nh * MLA_V), BF16),
        scratch_shapes=[pltpu.VMEM((nsub, blk, LANES), F32), pltpu.VMEM((nsub, blk, dk), F32)],
        compiler_params=_cparams(("arbitrary", "arbitrary", "arbitrary"), VMEM_LIMIT),
        name="flash_attention",
    )(q, k, v)


def _outproj_kernel(a_ref, x_ref, g1_ref, lg_ref, lb_ref, sc_ref, sh_ref, w_ref, rw_ref, rb_ref,
                    x1_ref, h2_ref, logit_ref):
    y = _dot(a_ref[...], w_ref[...])
    z = ALPHA * x_ref[...] + (1.0 + g1_ref[0]) * y
    x1 = _layer_norm(z, lg_ref[...], lb_ref[...])
    x1_ref[...] = x1
    h2 = x1 * (1.0 + sc_ref[0]) + sh_ref[0]
    h2_ref[...] = _pack_rows(h2)
    h2_hi, h2_lo = _split_bf16(h2)
    logit_ref[...] = _dot_split(h2_hi, h2_lo, rw_ref[...]) + rb_ref[...]


def _outproj_ln_router(a2d, x2d, w_out, g1, ln_g, ln_b, sc2, sh2, w_router, b_router, seq):
    n, d = x2d.shape
    da = a2d.shape[1]
    ne = w_router.shape[1]
    w = w_out.astype(BF16)
    rw = jnp.pad(w_router, ((0, 0), (0, LANES - ne)))
    rw_hilo = jnp.concatenate(_split_bf16(rw), axis=1)
    rb = jnp.pad(b_router, (0, LANES - ne), constant_values=NEG_BIG).reshape(1, LANES)
    tm = OUTPROJ_TILE
    tiles_per_seq = seq // tm
    row = lambda i: (i, 0)
    per_b = lambda i: (i // tiles_per_seq, 0, 0)
    vec = pl.BlockSpec((1, 1, d), per_b)
    return pl.pallas_call(
        _outproj_kernel,
        grid=(n // tm,),
        in_specs=[pl.BlockSpec((tm, da), row), pl.BlockSpec((tm, d), row), vec,
                  _const_spec((1, d)), _const_spec((1, d)), vec, vec,
                  _const_spec(w.shape), _const_spec(rw_hilo.shape), _const_spec(rb.shape)],
        out_specs=[pl.BlockSpec((tm, d), row), pl.BlockSpec((tm, d // 2), row), pl.BlockSpec((tm, LANES), row)],
        out_shape=[jax.ShapeDtypeStruct((n, d), F32), jax.ShapeDtypeStruct((n, d // 2), U32),
                   jax.ShapeDtypeStruct((n, LANES), F32)],
        compiler_params=_cparams(("arbitrary",), VMEM_LIMIT),
        name="outproj_ln_router",
    )(a2d, x2d, g1, ln_g.reshape(1, d), ln_b.reshape(1, d), sc2, sh2, w, rw_hilo, rb)


def _gate_log(g, is_input_gate):
    g = GATE_SOFTCAP * jnp.tanh(g * (1.0 / GATE_SOFTCAP))
    log_f = jnp.minimum(g, 0.0) - jnp.log(1.0 + jnp.exp(-jnp.abs(g)))
    return jnp.where(is_input_gate, g, log_f)


def _ml_inproj_kernel(x_ref, sc_ref, sh_ref, w_ref, gw_ref, gb_ref, gwth_ref, gwtl_ref, gbt_ref,
                      proj_ref, gcol_ref, grow_ref, h_scr):
    j = pl.program_id(1)

    @pl.when(j == 0)
    def _():
        h = x_ref[...] * (1.0 + sc_ref[0]) + sh_ref[0]
        h_hi, h_lo = _split_bf16(h)
        h_scr[...] = h_hi
        nh = ML_HEADS
        g = _dot_split(h_hi, h_lo, gw_ref[...]) + gb_ref[...]
        lane = lax.broadcasted_iota(I32, g.shape, 1)
        gcol_ref[...] = _gate_log(g, lane < nh)
        gt = (_dot_nt(gwth_ref[...], h_hi) + (_dot_nt(gwth_ref[...], h_lo) + _dot_nt(gwtl_ref[...], h_hi))
              + gbt_ref[...])
        sub = lax.broadcasted_iota(I32, gt.shape, 0)
        grow_ref[...] = _gate_log(gt, sub < nh)

    proj_ref[...] = _dot(h_scr[...], w_ref[...]).astype(proj_ref.dtype)


def _ml_inproj(x2d, sc, sh, w_in, b_gates, seq):
    n, d = x2d.shape
    ng = 2 * ML_HEADS
    wide = w_in.shape[1] - ng
    w = w_in[:, :wide].astype(BF16)
    gw = w_in[:, wide:]
    gw_hilo = jnp.concatenate(_split_bf16(jnp.pad(gw, ((0, 0), (0, LANES - ng)))), axis=1)
    gwt_hi, gwt_lo = _split_bf16(gw.T)
    gb = jnp.pad(b_gates, (0, LANES - ng)).reshape(1, LANES)
    gbt = b_gates.reshape(ng, 1)
    tm, tn = 1024, 1024
    tiles_per_seq = seq // tm
    row = lambda i, j: (i, 0)
    per_b = lambda i, j: (i // tiles_per_seq, 0, 0)
    const = lambda shape: pl.BlockSpec(shape, lambda i, j: (0,) * len(shape), pipeline_mode=pl.Buffered(1))
    return pl.pallas_call(
        _ml_inproj_kernel,
        grid=(n // tm, wide // tn),
        in_specs=[pl.BlockSpec((tm, d), row),
                  pl.BlockSpec((1, 1, d), per_b), pl.BlockSpec((1, 1, d), per_b),
                  pl.BlockSpec((d, tn), lambda i, j: (0, j)),
                  const(gw_hilo.shape), const(gb.shape),
                  const(gwt_hi.shape), const(gwt_lo.shape), const(gbt.shape)],
        out_specs=[pl.BlockSpec((tm, tn), lambda i, j: (i, j)),
                   pl.BlockSpec((tm, LANES), row),
                   pl.BlockSpec((ng, tm), lambda i, j: (0, i))],
        out_shape=[jax.ShapeDtypeStruct((n, wide), BF16), jax.ShapeDtypeStruct((n, LANES), F32),
                   jax.ShapeDtypeStruct((ng, n), F32)],
        scratch_shapes=[pltpu.VMEM((tm, d), BF16)],
        compiler_params=_cparams(("arbitrary", "arbitrary"), VMEM_LIMIT),
        name="mlstm_inproj",
    )(x2d, sc, sh, w, gw_hilo, gb, gwt_hi, gwt_lo, gbt)


def _split3_bf16(x):
    a = x.astype(BF16)
    r = x - a.astype(F32)
    b = r.astype(BF16)
    c = (r - b.astype(F32)).astype(BF16)
    return a, b, c


def _mlstm_kernel(q_ref, k_ref, v_ref, og_ref, gc_ref, gr_ref, hn_ref, o_ref, c_scr, m_scr, *, L, G):
    ci = pl.program_id(2)

    @pl.when(ci == 0)
    def _():
        c_scr[...] = jnp.zeros(c_scr.shape, F32)
        m_scr[...] = jnp.zeros(m_scr.shape, F32)

    row = lax.broadcasted_iota(I32, (L, L), 0)
    col = lax.broadcasted_iota(I32, (L, L), 1)
    causal = col <= row
    tri = jnp.where(causal, 1.0, 0.0).astype(BF16)
    ones_col = jnp.where(lax.broadcasted_iota(I32, (L, LANES), 1) == 0, 1.0, 0.0).astype(BF16)
    for g in range(G):
        _mlstm_head(q_ref[:, g * ML_QK:(g + 1) * ML_QK], k_ref[:, g * ML_QK:(g + 1) * ML_QK],
                    v_ref[:, g * ML_V:(g + 1) * ML_V], og_ref[:, g * ML_V:(g + 1) * ML_V],
                    gc_ref[g], gr_ref[g], hn_ref[:, g * ML_V:(g + 1) * ML_V],
                    o_ref.at[:, g * ML_V:(g + 1) * ML_V], c_scr.at[g], m_scr.at[g],
                    causal, tri, ones_col, L)


def _mlstm_head(q, k, v, og, gc, gr, head_gain, o_ref, c_scr, m_scr, causal, tri, ones_col, L):
    v_ext = jnp.concatenate([v, ones_col], axis=1)
    li_c, lf_c = gc[:, 0:1], gc[:, 1:2]
    li_r, lf_r = gr[0:1, :], gr[1:2, :]

    lfc3 = _split3_bf16(jnp.broadcast_to(lf_c, (L, LANES)))
    b_c = (_dot(tri, lfc3[0]) + (_dot(tri, lfc3[1]) + _dot(tri, lfc3[2])))[:, 0:1]
    lfr3 = _split3_bf16(jnp.broadcast_to(lf_r, (8, L)))
    b_r = (_dot_nt(lfr3[0], tri) + (_dot_nt(lfr3[1], tri) + _dot_nt(lfr3[2], tri)))[0:1, :]

    m_prev = m_scr[...]
    dm = jnp.where(causal, b_c - (b_r - li_r), NEG_BIG)
    inter = b_c + m_prev
    m_t = jnp.maximum(inter, jnp.max(dm, axis=1, keepdims=True))
    w_intra = jnp.exp(dm - m_t)
    w_inter = jnp.exp(inter - m_t)

    kscale = ML_QK ** -0.5
    qk = _dot_nt(q, k) * (w_intra * kscale)
    c_state = c_scr[...]
    num_ext = w_inter * _dot(q, c_state.astype(BF16)) + _dot(qk.astype(BF16), v_ext)
    vdim = v.shape[1]
    num = num_ext[:, :vdim]
    den = num_ext[:, vdim:vdim + 1]
    h = num / jnp.maximum(jnp.abs(den), jnp.exp(-m_t))
    hn = h * lax.rsqrt(jnp.mean(h * h, axis=-1, keepdims=True) + EPS) * head_gain
    o_ref[...] = (hn / (1.0 + jnp.exp(-og.astype(F32)))).astype(o_ref.dtype)

    b_end = b_r[:, L - 1:L]
    d_end_r = b_end - b_r + li_r
    m_new = jnp.maximum(b_end + m_prev, jnp.max(d_end_r, axis=1, keepdims=True))
    decay = jnp.exp(b_end + m_prev - m_new)
    w_s = jnp.exp(b_end - b_c + li_c - m_new) * kscale
    kw = (k.astype(F32) * w_s).astype(BF16)
    upd = lax.dot_general(kw, v_ext, (((0,), (0,)), ((), ())), preferred_element_type=F32)
    c_scr[...] = decay * c_state + upd
    m_scr[...] = m_new


def _mlstm_scan(proj, gcol, grow, head_norm, batch, seq):
    n = proj.shape[0]
    nh = ML_HEADS
    L = ML_CHUNK
    nc = seq // L
    gc = gcol[:, :2 * nh].reshape(n, 2, nh).transpose(2, 0, 1)
    gr = grow.reshape(2, nh, n).transpose(1, 0, 2)
    hn = head_norm.reshape(1, nh * ML_V)
    G = ML_HEADS_PER_STEP
    groups = nh // G
    k_off = groups
    v_off = 2 * nh * ML_QK // (G * ML_V)
    og_off = v_off + groups
    tok = lambda b, h, c: b * nc + c
    return pl.pallas_call(
        functools.partial(_mlstm_kernel, L=L, G=G),
        grid=(batch, groups, nc),
        in_specs=[pl.BlockSpec((L, G * ML_QK), lambda b, h, c: (tok(b, h, c), h)),
                  pl.BlockSpec((L, G * ML_QK), lambda b, h, c: (tok(b, h, c), k_off + h)),
                  pl.BlockSpec((L, G * ML_V), lambda b, h, c: (tok(b, h, c), v_off + h)),
                  pl.BlockSpec((L, G * ML_V), lambda b, h, c: (tok(b, h, c), og_off + h)),
                  pl.BlockSpec((G, L, 2), lambda b, h, c: (h, tok(b, h, c), 0)),
                  pl.BlockSpec((G, 2, L), lambda b, h, c: (h, 0, tok(b, h, c))),
                  pl.BlockSpec((1, G * ML_V), lambda b, h, c: (0, h))],
        out_specs=pl.BlockSpec((L, G * ML_V), lambda b, h, c: (tok(b, h, c), h)),
        out_shape=jax.ShapeDtypeStruct((n, nh * ML_V), BF16),
        scratch_shapes=[pltpu.VMEM((G, ML_QK, ML_V + LANES), F32), pltpu.VMEM((G, 1, 1), F32)],
        compiler_params=_cparams(("arbitrary", "arbitrary", "arbitrary"), VMEM_LIMIT),
        name="mlstm_scan",
    )(proj, proj, proj, proj, gc, gr, hn)


def _route_kernel(lg_ref, dest_ref, gate_ref, meta_ref, blk_ref, tri_scr, cnt_scr, run_scr, pst_scr,
                  *, T, E, K, BLK, NBP):
    ph = pl.program_id(0)
    t = pl.program_id(1)
    nt = pl.num_programs(1)

    @pl.when((ph == 0) & (t == 0))
    def _():
        r = lax.broadcasted_iota(I32, (T, T), 0)
        c = lax.broadcasted_iota(I32, (T, T), 1)
        tri_scr[...] = jnp.where(c < r, 1.0, 0.0).astype(BF16)
        cnt_scr[...] = jnp.zeros(cnt_scr.shape, F32)

    lane = lax.broadcasted_iota(I32, (T, LANES), 1)
    l = jnp.where(lane < E, lg_ref[...], -jnp.inf)
    vals, hots = [], []
    for _ in range(K):
        mx = jnp.max(l, axis=1, keepdims=True)
        idx = jnp.min(jnp.where(l == mx, lane, LANES), axis=1, keepdims=True)
        hot = lane == idx
        vals.append(mx)
        hots.append(hot)
        l = jnp.where(hot, -jnp.inf, l)
    hot_all = jnp.zeros((T, LANES), F32)
    for hot in hots:
        hot_all = hot_all + jnp.where(hot, 1.0, 0.0)
    colsum = jnp.sum(hot_all, axis=0, keepdims=True)

    @pl.when(ph == 0)
    def _():
        cnt_scr[...] = cnt_scr[...] + colsum

    @pl.when((ph == 1) & (t == 0))
    def _():
        cnt = cnt_scr[...].astype(I32)
        pc = ((cnt + (BLK - 1)) & (-BLK)).astype(F32)
        r = lax.broadcasted_iota(I32, (LANES, LANES), 0)
        c = lax.broadcasted_iota(I32, (LANES, LANES), 1)
        upper = jnp.where(r < c, 1.0, 0.0)
        pstart = jnp.dot(jnp.broadcast_to(pc, (8, LANES)), upper, precision=HIGHEST,
                         preferred_element_type=F32)[0:1, :]
        pst_scr[...] = pstart
        run_scr[...] = jnp.zeros(run_scr.shape, F32)
        pend = pstart + pc
        meta_ref[0:1, :] = cnt_scr[...].astype(I32)
        meta_ref[1:2, :] = pstart.astype(I32)
        meta_ref[2:3, :] = pend.astype(I32)
        meta_ref[3:8, :] = jnp.zeros((5, LANES), I32)
        jstart = (lax.broadcasted_iota(I32, (NBP, LANES), 0) * BLK).astype(F32)
        elane = lax.broadcasted_iota(I32, (NBP, LANES), 1)
        owned = jnp.where((pend <= jstart) & (elane < E), 1.0, 0.0)
        be = jnp.minimum(jnp.sum(owned, axis=1, keepdims=True), float(E - 1))
        blk_ref[...] = jnp.broadcast_to(be, (NBP, LANES)).astype(I32)

    @pl.when(ph == 1)
    def _():
        earlier = _dot(tri_scr[...], hot_all.astype(BF16))
        pos = earlier + (pst_scr[...] + run_scr[...])
        e0 = jnp.ones_like(vals[0])
        es = [e0] + [jnp.exp(v - vals[0]) for v in vals[1:]]
        tot = es[0]
        for e in es[1:]:
            tot = tot + e
        for kk in range(K):
            d = jnp.sum(jnp.where(hots[kk], pos, 0.0), axis=1, keepdims=True)
            dest_ref[:, kk:kk + 1] = d.astype(I32)
            gate_ref[:, kk:kk + 1] = es[kk] / tot
        run_scr[...] = run_scr[...] + colsum


def _route(logits, n_blocks):
    n = logits.shape[0]
    T = min(ROUTE_TILE, n)
    nbp = -(-n_blocks // 8) * 8
    kern = functools.partial(_route_kernel, T=T, E=N_EXPERTS, K=TOP_K, BLK=MOE_SLOT_BLOCK, NBP=nbp)
    tok_out = lambda ph, t: (t * ph, 0)
    fixed = lambda ph, t: (0, 0)
    return pl.pallas_call(
        kern,
        grid=(2, n // T),
        in_specs=[pl.BlockSpec((T, LANES), lambda ph, t: (t, 0))],
        out_specs=[pl.BlockSpec((T, TOP_K), tok_out), pl.BlockSpec((T, TOP_K), tok_out),
                   pl.BlockSpec((8, LANES), fixed), pl.BlockSpec((nbp, LANES), fixed)],
        out_shape=[jax.ShapeDtypeStruct((n, TOP_K), I32), jax.ShapeDtypeStruct((n, TOP_K), F32),
                   jax.ShapeDtypeStruct((8, LANES), I32), jax.ShapeDtypeStruct((nbp, LANES), I32)],
        scratch_shapes=[pltpu.VMEM((T, T), BF16), pltpu.VMEM((1, LANES), F32), pltpu.VMEM((1, LANES), F32),
                        pltpu.VMEM((1, LANES), F32)],
        compiler_params=_cparams(("arbitrary", "arbitrary")),
        name="moe_route",
    )(logits)


def _dispatch_kernel(meta_ref, dest_ref, h_ref, *rest, T, K, E, BLK, NB):
    if len(rest) == 4:
        xs_ref, zero_scr, sem, zsem = rest
    else:
        gu_ref, dn_ref, xs_ref, ogu_ref, odn_ref, zero_scr, sem, zsem = rest
        ogu_ref[...] = gu_ref[...].astype(BF16)
        odn_ref[...] = dn_ref[...].astype(BF16)
    i = pl.program_id(0)
    last = pl.num_programs(0) - 1

    def row_copy(r, kk):
        d = dest_ref[r * K + kk]
        return pltpu.make_async_copy(h_ref.at[pl.ds(r, 1)], xs_ref.at[pl.ds(d, 1)], sem)

    def issue(r, carry):
        for kk in range(K):
            row_copy(r, kk).start(priority=kk % 2)
        return carry

    lax.fori_loop(0, T, issue, 0, unroll=DMA_UNROLL)

    @pl.when(i == last)
    def _():
        zero_scr[...] = jnp.zeros(zero_scr.shape, zero_scr.dtype)

        def pad_copy(slot):
            return pltpu.make_async_copy(zero_scr.at[pl.ds(0, 1)], xs_ref.at[pl.ds(slot, 1)], zsem)

        for e in range(E):
            cnt = meta_ref[0, e]
            first = meta_ref[1, e] + cnt
            npad = meta_ref[2, e] - first

            def zissue(r, carry, first=first):
                pad_copy(first + r).start()
                return carry

            def zwait(r, carry, first=first):
                pad_copy(first + r).wait()
                return carry

            lax.fori_loop(0, npad, zissue, 0)
            lax.fori_loop(0, npad, zwait, 0)

        used = meta_ref[2, E - 1] // BLK

        def tail_copy(j):
            return pltpu.make_async_copy(zero_scr, xs_ref.at[pl.ds(pl.multiple_of(j * BLK, BLK), BLK)], zsem)

        def tissue(j, carry):
            tail_copy(j).start()
            return carry

        def twait(j, carry):
            tail_copy(j).wait()
            return carry

        lax.fori_loop(used, NB, tissue, 0)
        lax.fori_loop(used, NB, twait, 0)

    def drain(r, carry):
        for kk in range(K):
            row_copy(r, kk).wait()
        return carry

    lax.fori_loop(0, T, drain, 0, unroll=DMA_UNROLL)


def _cast_chunk_specs(w_all, layer, chunks, step_of):
    rows = w_all.shape[1] * w_all.shape[2]
    cols = w_all.shape[3]
    chunk = rows // chunks
    first = layer * chunks
    in_spec = pl.BlockSpec((chunk, cols), lambda *a: (first + jnp.minimum(step_of(*a), chunks - 1), 0))
    out_spec = pl.BlockSpec((chunk, cols), lambda *a: (jnp.minimum(step_of(*a), chunks - 1), 0))
    return (in_spec, out_spec, jax.ShapeDtypeStruct((rows, cols), BF16),
            w_all.reshape(w_all.shape[0] * rows, cols))


def _dispatch(h2, dest_flat, meta, n_blocks, cast=None):
    n, d = h2.shape
    T = MOE_ROW_TILE
    blk = MOE_SLOT_BLOCK
    steps = n // T
    kern = functools.partial(_dispatch_kernel, T=T, K=TOP_K, E=N_EXPERTS, BLK=blk, NB=n_blocks)
    in_specs = [pl.BlockSpec((T * TOP_K,), lambda i, m: (i,), memory_space=pltpu.SMEM),
                pl.BlockSpec((T, d), lambda i, m: (i, 0))]
    out_specs = [pl.BlockSpec(memory_space=pl.ANY)]
    out_shape = [jax.ShapeDtypeStruct((n_blocks * blk, d), h2.dtype)]
    operands = [meta, dest_flat, h2]
    if cast is not None:
        chunks = 1 << (steps.bit_length() - 1)
        for w_all in cast[:2]:
            i_spec, o_spec, o_shape, flat = _cast_chunk_specs(w_all, cast[2], chunks, lambda i, m: i)
            in_specs.append(i_spec)
            out_specs.append(o_spec)
            out_shape.append(o_shape)
            operands.append(flat)
    gs = pltpu.PrefetchScalarGridSpec(
        num_scalar_prefetch=1,
        grid=(steps,),
        in_specs=in_specs,
        out_specs=out_specs,
        scratch_shapes=[pltpu.VMEM((blk, d), h2.dtype), pltpu.SemaphoreType.DMA(()),
                        pltpu.SemaphoreType.DMA(())],
    )
    outs = pl.pallas_call(
        kern,
        grid_spec=gs,
        out_shape=out_shape,
        compiler_params=_cparams(("arbitrary",), VMEM_LIMIT),
        name="moe_dispatch",
    )(*operands)
    if cast is None:
        return outs[0], None
    return outs[0], (outs[1].reshape(cast[0].shape[1:]), outs[2].reshape(cast[1].shape[1:]))


def _expert_kernel(be_ref, used_ref, xs_ref, wgu_ref, bgu_ref, wd_ref, bd_ref, *rest):
    j = pl.program_id(0)
    if len(rest) == 1:
        (ys_ref,) = rest
    else:
        ngu_ref, ndn_ref, ys_ref, ogu_ref, odn_ref = rest
        ogu_ref[...] = ngu_ref[...].astype(BF16)
        odn_ref[...] = ndn_ref[...].astype(BF16)

    @pl.when(j < used_ref[0])
    def _():
        x_lo, x_hi = _unpack_rows(xs_ref[...])
        x = jnp.concatenate([x_lo.astype(BF16), x_hi.astype(BF16)], axis=1)
        gu = _dot(x, wgu_ref[0]) + bgu_ref[0]
        de = gu.shape[1] // 2
        a = jnp.minimum(gu[:, :de], SWIGLU_LIMIT)
        u = jnp.clip(gu[:, de:], -SWIGLU_LIMIT, SWIGLU_LIMIT)
        act = (u + 1.0) * (a / (1.0 + jnp.exp(-SWIGLU_ALPHA * a)))
        ys_ref[...] = _pack_rows(_dot(act.astype(BF16), wd_ref[0]) + bd_ref[0])

    @pl.when(j >= used_ref[0])
    def _():
        ys_ref[...] = jnp.zeros(ys_ref.shape, ys_ref.dtype)


def _experts(xs, block_e, used, w_gu, b_gu, w_down, b_down, cast_next=None):
    p, dw = xs.shape
    d = 2 * dw
    blk = MOE_SLOT_BLOCK
    ne, _, de2 = w_gu.shape
    nb = p // blk
    last_used = lambda j, be, used: jnp.minimum(j, jnp.maximum(used[0] - 1, 0))
    in_specs = [pl.BlockSpec((blk, dw), lambda j, be, used: (last_used(j, be, used), 0)),
                pl.BlockSpec((1, d, de2), lambda j, be, used: (be[j], 0, 0)),
                pl.BlockSpec((1, 1, de2), lambda j, be, used: (be[j], 0, 0)),
                pl.BlockSpec((1, de2 // 2, d), lambda j, be, used: (be[j], 0, 0)),
                pl.BlockSpec((1, 1, d), lambda j, be, used: (be[j], 0, 0))]
    out_specs = [pl.BlockSpec((blk, dw), lambda j, be, used: (j, 0))]
    out_shape = [jax.ShapeDtypeStruct((p, dw), U32)]
    operands = [block_e, used, xs, w_gu, b_gu.reshape(ne, 1, de2), w_down, b_down.reshape(ne, 1, d)]
    if cast_next is not None:
        chunks = 1 << (min(nb, CAST_CHUNKS).bit_length() - 1)
        for w_all in cast_next[:2]:
            i_spec, o_spec, o_shape, flat = _cast_chunk_specs(w_all, cast_next[2], chunks,
                                                              lambda j, be, used: j)
            in_specs.append(i_spec)
            out_specs.append(o_spec)
            out_shape.append(o_shape)
            operands.append(flat)
    gs = pltpu.PrefetchScalarGridSpec(num_scalar_prefetch=2, grid=(nb,), in_specs=in_specs, out_specs=out_specs)
    outs = pl.pallas_call(
        _expert_kernel,
        grid_spec=gs,
        out_shape=out_shape,
        compiler_params=_cparams(("arbitrary",), VMEM_LIMIT),
        name="moe_experts",
    )(*operands)
    if cast_next is None:
        return outs[0], None
    gu_all, down_all, _ = cast_next
    return outs[0], (outs[1].reshape(gu_all.shape[1:]), outs[2].reshape(down_all.shape[1:]))


def _combine_kernel(dest_ref, dest_next_ref, ys_ref, gate_ref, x_ref, g2_ref, lg_ref, lb_ref, o_ref, buf, sems,
                    *, T, K):
    i = pl.program_id(0)
    n = pl.num_programs(0)
    slot = i % 2

    def row_copy(idx_ref, dst_slot, r, kk):
        d = idx_ref[r * K + kk]
        return pltpu.make_async_copy(ys_ref.at[pl.ds(d, 1)], buf.at[dst_slot, kk, pl.ds(r, 1)],
                                     sems.at[dst_slot])

    def issue_tile(idx_ref, dst_slot):
        def issue(r, carry):
            for kk in range(K):
                row_copy(idx_ref, dst_slot, r, kk).start(priority=kk % 2)
            return carry
        lax.fori_loop(0, T, issue, 0, unroll=DMA_UNROLL)

    @pl.when(i == 0)
    def _():
        issue_tile(dest_ref, slot)

    @pl.when(i + 1 < n)
    def _():
        issue_tile(dest_next_ref, 1 - slot)

    def drain(r, carry):
        for kk in range(K):
            row_copy(dest_ref, slot, r, kk).wait()
        return carry

    lax.fori_loop(0, T, drain, 0, unroll=DMA_UNROLL)
    g = gate_ref[...]
    y_lo = y_hi = None
    for kk in range(K):
        lo, hi = _unpack_rows(buf[slot, kk])
        gk = g[:, kk:kk + 1]
        y_lo = gk * lo if y_lo is None else y_lo + gk * lo
        y_hi = gk * hi if y_hi is None else y_hi + gk * hi
    y = jnp.concatenate([y_lo, y_hi], axis=1)
    z = ALPHA * x_ref[...] + (1.0 + g2_ref[0]) * y
    o_ref[...] = _layer_norm(z, lg_ref[...], lb_ref[...])


def _combine(ys, dest_flat, gates, x1, g2, ln_g, ln_b, seq):
    n, d = x1.shape
    T = MOE_ROW_TILE
    tiles_per_seq = seq // T
    row = lambda i: (i, 0)
    last = n // T - 1
    return pl.pallas_call(
        functools.partial(_combine_kernel, T=T, K=TOP_K),
        grid=(n // T,),
        in_specs=[pl.BlockSpec((T * TOP_K,), lambda i: (i,), memory_space=pltpu.SMEM),
                  pl.BlockSpec((T * TOP_K,), lambda i: (jnp.minimum(i + 1, last),), memory_space=pltpu.SMEM),
                  pl.BlockSpec(memory_space=pl.ANY),
                  pl.BlockSpec((T, TOP_K), row), pl.BlockSpec((T, d), row),
                  pl.BlockSpec((1, 1, d), lambda i: (i // tiles_per_seq, 0, 0)),
                  _const_spec((1, d)), _const_spec((1, d))],
        out_specs=pl.BlockSpec((T, d), row),
        out_shape=jax.ShapeDtypeStruct((n, d), F32),
        scratch_shapes=[pltpu.VMEM((2, TOP_K, T, ys.shape[1]), ys.dtype), pltpu.SemaphoreType.DMA((2,))],
        compiler_params=_cparams(("arbitrary",), VMEM_LIMIT),
        name="moe_combine",
    )(dest_flat, dest_flat, ys, gates, x1, g2, ln_g.reshape(1, d), ln_b.reshape(1, d))


def _moe(h2, logits, x1, g2, ln_g, ln_b, layer, weights, w_gu_f32, b_gu, w_down_f32, b_down, seq):
    n = h2.shape[0]
    blk = MOE_SLOT_BLOCK
    n_blocks = -(-(n * TOP_K + N_EXPERTS * (blk - 1)) // blk)
    dest, gates, meta, block_e = _route(logits, n_blocks)
    dest_flat = dest.reshape(n * TOP_K)
    cast_here = (w_gu_f32, w_down_f32, layer) if weights is None else None
    xs, cast_weights = _dispatch(h2, dest_flat, meta, n_blocks, cast_here)
    w_gu, w_down = cast_weights if weights is None else weights
    used = (meta[2, N_EXPERTS - 1] // blk).reshape(1)
    cast_next = (w_gu_f32, w_down_f32, layer + 1) if layer + 1 < w_gu_f32.shape[0] else None
    ys, next_weights = _experts(xs, block_e[:n_blocks, 0], used, w_gu, b_gu[layer], w_down, b_down[layer],
                                cast_next)
    return _combine(ys, dest_flat, gates, x1, g2, ln_g, ln_b, seq), next_weights


def kernel(x, c, positions, ada_w, ada_b, ln_g, ln_b, mla_w_in, mla_q_norm, mla_w_uq, mla_kv_norm, mla_w_ukv,
           mla_w_out, ml_w_in, ml_b_gates, ml_head_norm, ml_w_out, moe_w_router, moe_b_router, moe_w_gu,
           moe_b_gu, moe_w_down, moe_b_down):
    batch, seq, d = x.shape
    n = batch * seq
    mods = _mods(c, ada_w, ada_b)
    xf = x.reshape(n, d)
    expert_w = None
    for i in range(DEPTH):
        j = i // 2
        sh1, sc1, g1, sh2, sc2, g2 = [mods[i, :, s * d:(s + 1) * d].reshape(batch, 1, d) for s in range(6)]
        if i % 2 == 0:
            cos, sin = _rope_tables(positions)
            q, k, v = _mla_front(xf, sc1, sh1, cos, sin, mla_w_in[j], mla_q_norm[j], mla_w_uq[j],
                                 mla_kv_norm[j], mla_w_ukv[j], batch, seq)
            a = _flash_attention(q, k, v).reshape(n, MLA_HEADS * MLA_V)
            w_out = mla_w_out[j]
        else:
            proj, gcol, grow = _ml_inproj(xf, sc1, sh1, ml_w_in[j], ml_b_gates[j], seq)
            a = _mlstm_scan(proj, gcol, grow, ml_head_norm[j], batch, seq)
            w_out = ml_w_out[j]
        x1, h2, logits = _outproj_ln_router(a, xf, w_out, g1, ln_g[i, 0], ln_b[i, 0], sc2, sh2,
                                            moe_w_router[i], moe_b_router[i], seq)
        xf, expert_w = _moe(h2, logits, x1, g2, ln_g[i, 1], ln_b[i, 1], i, expert_w, moe_w_gu, moe_b_gu,
                            moe_w_down, moe_b_down, seq)
    return xf.reshape(batch, seq, d)
```

```python
import functools
import math

import jax
import jax.numpy as jnp
from jax import lax
from jax.experimental import pallas as pl
from jax.experimental.pallas import tpu as pltpu

F32 = jnp.float32
BF16 = jnp.bfloat16
I32 = jnp.int32
U32 = jnp.uint32
HIGHEST = lax.Precision.HIGHEST
HIGH_HALF = 0xFFFF0000

CHUNK = 64
MLA_HEADS = 16
MLA_NOPE = 128
MLA_ROPE = 64
MLA_V = 128
MLA_Q_RANK = 448
MLA_KV_RANK = 512
ROPE_THETA = 10000.0
ML_HEADS = 4
ML_QK = 256
ML_V = 512
GATE_SOFTCAP = 15.0
N_EXPERTS = 32
TOP_K = 4
D_EXPERT = 1024
SWIGLU_LIMIT = 7.0
SWIGLU_ALPHA = 1.702
DEPTH = 2
ALPHA = (2 * DEPTH) ** 0.25
EPS = 1e-6
NEG_BIG = -1e30

LANES = 128
VMEM_LIMIT = 56 * 1024 * 1024

Q_RANK_PAD = 512
ATT_BLOCK = 512
ATT_KBLOCK = 1024
ATT_CHAINS = 8
ML_CHUNK = 256
ML_HEADS_PER_STEP = 4
MOE_SLOT_BLOCK = 512
CAST_CHUNKS = 256
ROUTE_TILE = 1024
ROW_TILE = 256
MOE_ROW_TILE = 512
OUTPROJ_TILE = 512
DMA_UNROLL = 8


def _cparams(sem, vmem=None):
    return pltpu.CompilerParams(dimension_semantics=sem, vmem_limit_bytes=vmem)


def _const_spec(shape):
    nd = len(shape)
    return pl.BlockSpec(shape, lambda *_: (0,) * nd, pipeline_mode=pl.Buffered(1))


def _split_bf16(x):
    hi = x.astype(BF16)
    lo = (x - hi.astype(F32)).astype(BF16)
    return hi, lo


def _dot(a, b):
    return jnp.dot(a, b, preferred_element_type=F32)


def _dot_nt(a, b):
    return lax.dot_general(a, b, (((1,), (1,)), ((), ())), preferred_element_type=F32)


def _dot_split(x_hi, x_lo, w_hilo):
    p = _dot(x_hi, w_hilo)
    return p[:, :LANES] + (p[:, LANES:] + _dot(x_lo, w_hilo[:, :LANES]))


def _pack_rows(x):
    w = x.shape[1] // 2
    lo = pltpu.bitcast(x[:, :w].astype(BF16).astype(F32), U32)
    hi = pltpu.bitcast(x[:, w:].astype(BF16).astype(F32), U32)
    return (hi & jnp.uint32(HIGH_HALF)) | lax.shift_right_logical(lo, jnp.uint32(16))


def _unpack_rows(words):
    lo = pltpu.bitcast(lax.shift_left(words, jnp.uint32(16)), F32)
    hi = pltpu.bitcast(words & jnp.uint32(HIGH_HALF), F32)
    return lo, hi


def _layer_norm(z, g, b):
    mu = jnp.mean(z, axis=-1, keepdims=True)
    zc = z - mu
    var = jnp.mean(zc * zc, axis=-1, keepdims=True)
    return zc * lax.rsqrt(var + EPS) * g + b


def _mods_kernel(c_ref, w_ref, b_ref, o_ref):
    c = c_ref[...]
    ca = c / (1.0 + jnp.exp(-c))
    o_ref[0] = jnp.dot(ca, w_ref[0], precision=HIGHEST, preferred_element_type=F32) + b_ref[0]


def _mods(c, ada_w, ada_b):
    depth, d, d6 = ada_w.shape
    b = c.shape[0]
    rows = 8
    c8 = jnp.pad(c, ((0, rows - b), (0, 0)))
    tn = 1024
    out = pl.pallas_call(
        _mods_kernel,
        grid=(depth, d6 // tn),
        in_specs=[pl.BlockSpec((rows, d), lambda i, j: (0, 0)),
                  pl.BlockSpec((1, d, tn), lambda i, j: (i, 0, j)),
                  pl.BlockSpec((1, 1, tn), lambda i, j: (i, 0, j))],
        out_specs=pl.BlockSpec((1, rows, tn), lambda i, j: (i, 0, j)),
        out_shape=jax.ShapeDtypeStruct((depth, rows, d6), F32),
        compiler_params=_cparams(("arbitrary", "arbitrary"), VMEM_LIMIT),
        name="adaln_mods",
    )(c8, ada_w, ada_b.reshape(depth, 1, d6))
    return out[:, :b]


def _rope_kernel(pos_ref, inv_ref, cos_ref, sin_ref):
    ang = pos_ref[...].astype(F32) * inv_ref[...]
    cos_ref[...] = jnp.cos(ang)
    sin_ref[...] = jnp.sin(ang)


def _rope_tables(positions):
    n = positions.size
    half = MLA_ROPE // 2
    inv = ROPE_THETA ** (-jnp.arange(0, MLA_ROPE, 2, dtype=F32) / MLA_ROPE)
    inv_row = jnp.tile(inv, LANES // half).reshape(1, LANES)
    pos = jnp.broadcast_to(positions.reshape(n, 1), (n, LANES))
    tm = 1024
    spec = pl.BlockSpec((tm, LANES), lambda i: (i, 0))
    return pl.pallas_call(
        _rope_kernel,
        grid=(n // tm,),
        in_specs=[spec, pl.BlockSpec((1, LANES), lambda i: (0, 0))],
        out_specs=[spec, spec],
        out_shape=[jax.ShapeDtypeStruct((n, LANES), F32)] * 2,
        compiler_params=_cparams(("arbitrary",)),
        name="rope_tables",
    )(pos, inv_row)


def _mla_front_kernel(x_ref, sc_ref, sh_ref, cos_ref, sin_ref, win_ref, qn_ref, kvn_ref, wq_ref, wkv_ref,
                      q_ref, k_ref, v_ref, *, qscale):
    h = (x_ref[...] * (1.0 + sc_ref[0]) + sh_ref[0]).astype(BF16)
    proj = _dot(h, win_ref[...])
    cq = proj[:, :Q_RANK_PAD]
    ckv = proj[:, Q_RANK_PAD:Q_RANK_PAD + MLA_KV_RANK]
    cq = cq * lax.rsqrt(jnp.sum(cq * cq, -1, keepdims=True) * (1.0 / MLA_Q_RANK) + EPS) * qn_ref[...]
    ckv = ckv * lax.rsqrt(jnp.mean(ckv * ckv, -1, keepdims=True) + EPS) * kvn_ref[...]
    cq = cq.astype(BF16)
    ckv = ckv.astype(BF16)
    cos = cos_ref[...]
    sin = sin_ref[...]
    o = Q_RANK_PAD + MLA_KV_RANK
    kr = proj[:, o:o + LANES] * cos + proj[:, o + LANES:o + 2 * LANES] * sin
    lane = lax.broadcasted_iota(I32, kr.shape, 1)
    kr_even = jnp.where(lane < MLA_ROPE, kr, 0.0).astype(BF16)
    kr_odd = jnp.where(lane >= MLA_ROPE, kr, 0.0).astype(BF16)
    ones_col = jnp.where(lane == 0, 1.0, 0.0).astype(BF16)

    nheads = MLA_HEADS
    nope_w = nheads * MLA_NOPE
    rope_w = nheads * MLA_ROPE
    group = 4
    for g in range(nheads // group):
        qn = _dot(cq, wq_ref[:, g * group * MLA_NOPE:(g + 1) * group * MLA_NOPE]) * qscale
        kv = _dot(ckv, wkv_ref[:, g * group * 2 * LANES:(g + 1) * group * 2 * LANES])
        for j in range(group):
            hh = g * group + j
            q_ref[0, hh, :, :LANES] = qn[:, j * LANES:(j + 1) * LANES].astype(BF16)
            k_ref[0, hh, :, :LANES] = kv[:, 2 * j * LANES:(2 * j + 1) * LANES].astype(BF16)
            v_ref[0, hh, :, :LANES] = kv[:, (2 * j + 1) * LANES:(2 * j + 2) * LANES].astype(BF16)
            k_ref[0, hh, :, LANES:] = kr_even if hh % 2 == 0 else kr_odd
            v_ref[0, hh, :, LANES:] = ones_col
    for p in range(nheads // 2):
        lo = nope_w + p * LANES
        qr = (_dot(cq, wq_ref[:, lo:lo + LANES]) * cos
              + _dot(cq, wq_ref[:, rope_w + lo:rope_w + lo + LANES]) * sin) * qscale
        qr = qr.astype(BF16)
        q_ref[0, 2 * p, :, LANES:] = qr
        q_ref[0, 2 * p + 1, :, LANES:] = qr


def _mla_front(x2d, sc, sh, cos, sin, w_in, q_norm, w_uq, kv_norm, w_ukv, batch, seq):
    n, d = x2d.shape
    nh = MLA_HEADS
    half = MLA_ROPE // 2
    wq_lat = jnp.pad(w_in[:, :MLA_Q_RANK], ((0, 0), (0, Q_RANK_PAD - MLA_Q_RANK)))
    wkv_lat = w_in[:, MLA_Q_RANK:MLA_Q_RANK + MLA_KV_RANK]
    wkr = w_in[:, MLA_Q_RANK + MLA_KV_RANK:]
    wkr_rot = jnp.concatenate([-wkr[:, half:], wkr[:, :half]], 1)
    win = jnp.concatenate([wq_lat, wkv_lat, wkr, wkr, wkr_rot, wkr_rot], 1).astype(BF16)
    wq3 = w_uq.reshape(MLA_Q_RANK, nh, MLA_NOPE + MLA_ROPE)
    wq_nope = wq3[:, :, :MLA_NOPE].reshape(MLA_Q_RANK, nh * MLA_NOPE)
    wq_r = wq3[:, :, MLA_NOPE:]
    wq_rope = wq_r.reshape(MLA_Q_RANK, nh * MLA_ROPE)
    wq_rot = jnp.concatenate([-wq_r[:, :, half:], wq_r[:, :, :half]], -1).reshape(MLA_Q_RANK, nh * MLA_ROPE)
    wq = jnp.pad(jnp.concatenate([wq_nope, wq_rope, wq_rot], 1),
                 ((0, Q_RANK_PAD - MLA_Q_RANK), (0, 0))).astype(BF16)
    wkv = w_ukv.astype(BF16)
    qn = jnp.pad(q_norm, (0, Q_RANK_PAD - MLA_Q_RANK)).reshape(1, Q_RANK_PAD)
    kvn = kv_norm.reshape(1, MLA_KV_RANK)
    qscale = (MLA_NOPE + MLA_ROPE) ** -0.5 * math.log2(math.e)

    tm = ROW_TILE
    tiles_per_seq = seq // tm
    row = lambda i: (i, 0)
    per_b = lambda i: (i // tiles_per_seq, 0, 0)
    head_out = pl.BlockSpec((1, nh, tm, 2 * LANES), lambda i: (i // tiles_per_seq, 0, i % tiles_per_seq, 0))
    out_sds = jax.ShapeDtypeStruct((batch, nh, seq, 2 * LANES), BF16)
    return pl.pallas_call(
        functools.partial(_mla_front_kernel, qscale=qscale),
        grid=(n // tm,),
        in_specs=[pl.BlockSpec((tm, d), row),
                  pl.BlockSpec((1, 1, d), per_b), pl.BlockSpec((1, 1, d), per_b),
                  pl.BlockSpec((tm, LANES), row), pl.BlockSpec((tm, LANES), row),
                  _const_spec(win.shape), _const_spec(qn.shape), _const_spec(kvn.shape),
                  _const_spec(wq.shape), _const_spec(wkv.shape)],
        out_specs=[head_out, head_out, head_out],
        out_shape=[out_sds, out_sds, out_sds],
        compiler_params=_cparams(("arbitrary",), VMEM_LIMIT),
        name="mla_front",
    )(x2d, sc, sh, cos, sin, win, qn, kvn, wq, wkv)


def _flash_kernel(q_ref, k_ref, v_ref, o_ref, m_ref, acc_ref, p_last, a_last, *, blk, kblk, nsub):
    i = pl.program_id(2)
    m_ref[...] = jnp.full(m_ref.shape, NEG_BIG, F32)
    acc_ref[...] = jnp.zeros(acc_ref.shape, F32)
    p_last[...] = jnp.zeros(p_last.shape, BF16)
    a_last[...] = jnp.ones(a_last.shape, F32)
    slabs = kblk // LANES
    ratio = kblk // blk
    shift = CHUNK.bit_length() - 1
    everyone = range(nsub)
    last = nsub - 1

    def accumulate(c, alpha, p, kb):
        v = v_ref[0, 0, pl.ds(pl.multiple_of(kb * kblk, kblk), kblk), :]
        pv = _dot(p, v)
        acc_ref[c, :, :LANES] = alpha * acc_ref[c, :, :LANES] + pv[:, :LANES]
        acc_ref[c, :, LANES:] = alpha * acc_ref[c, :, LANES:] + pv[:, LANES:]

    def step(c, kb, masked, defer=False):
        start = pl.multiple_of(kb * kblk, kblk)
        q = q_ref[0, 0, c * blk:(c + 1) * blk, :]
        k = k_ref[0, 0, pl.ds(start, kblk), :]
        s = _dot_nt(q, k)
        if masked:
            r = lax.shift_right_logical(lax.broadcasted_iota(I32, s.shape, 0) + (c % ratio) * blk, shift)
            cc = lax.shift_right_logical(lax.broadcasted_iota(I32, s.shape, 1), shift)
            s = jnp.where(cc <= r, s, NEG_BIG)
        m_prev = m_ref[c]
        m_new = jnp.maximum(m_prev, jnp.max(s, axis=1, keepdims=True))
        p = jnp.concatenate([jnp.exp2(s[:, j * LANES:(j + 1) * LANES] - m_new).astype(BF16)
                             for j in range(slabs)], axis=1)
        alpha = jnp.exp2(m_prev - m_new)
        m_ref[c] = m_new
        if defer:
            p_last[...] = p
            a_last[...] = alpha
        else:
            accumulate(c, alpha, p, kb)

    def flush_last(kb):
        accumulate(last, a_last[...], p_last[...], jnp.maximum(kb, 0))

    def full_steps(kb, carry):
        flush_last(kb - 1)
        for c in everyone:
            step(c, kb, False, defer=(c == last))
        return carry

    first_diag = (nsub // ratio) * i
    lax.fori_loop(0, first_diag, full_steps, 0)
    flush_last(first_diag - 1)
    for t in range(nsub // ratio):
        for c in everyone:
            if c // ratio >= t:
                step(c, first_diag + t, masked=(c // ratio == t))
    for c in everyone:
        l = acc_ref[c, :, LANES:LANES + 1]
        o_ref[0, c * blk:(c + 1) * blk, :] = (acc_ref[c, :, :MLA_V] / l).astype(o_ref.dtype)


def _flash_attention(q, k, v):
    batch, nh, seq, dk = q.shape
    blk = ATT_BLOCK
    nsub = ATT_CHAINS
    kblk = ATT_KBLOCK
    assert kblk % blk == 0 and nsub % (kblk // blk) == 0
    qspec = pl.BlockSpec((1, 1, nsub * blk, dk), lambda b, h, i: (b, h, i, 0))
    kvspec = pl.BlockSpec((1, 1, seq, dk), lambda b, h, i: (b, h, 0, 0))
    return pl.pallas_call(
        functools.partial(_flash_kernel, blk=blk, kblk=kblk, nsub=nsub),
        grid=(batch, nh, seq // (nsub * blk)),
        in_specs=[qspec, kvspec, kvspec],
        out_specs=pl.BlockSpec((1, nsub * blk, MLA_V), lambda b, h, i: (b, i, h)),
        out_shape=jax.ShapeDtypeStruct((batch, seq, nh * MLA_V), BF16),
        scratch_shapes=[pltpu.VMEM((nsub, blk, LANES), F32), pltpu.VMEM((nsub, blk, dk), F32),
                        pltpu.VMEM((blk, kblk), BF16), pltpu.VMEM((blk, LANES), F32)],
        compiler_params=_cparams(("arbitrary", "arbitrary", "arbitrary"), VMEM_LIMIT),
        name="flash_attention",
    )(q, k, v)


def _outproj_kernel(a_ref, x_ref, g1_ref, lg_ref, lb_ref, sc_ref, sh_ref, w_ref, rw_ref, rb_ref,
                    x1_ref, h2_ref, logit_ref):
    y = _dot(a_ref[...], w_ref[...])
    z = ALPHA * x_ref[...] + (1.0 + g1_ref[0]) * y
    x1 = _layer_norm(z, lg_ref[...], lb_ref[...])
    x1_ref[...] = x1
    h2 = x1 * (1.0 + sc_ref[0]) + sh_ref[0]
    h2_ref[...] = _pack_rows(h2)
    h2_hi, h2_lo = _split_bf16(h2)
    logit_ref[...] = _dot_split(h2_hi, h2_lo, rw_ref[...]) + rb_ref[...]


def _outproj_ln_router(a2d, x2d, w_out, g1, ln_g, ln_b, sc2, sh2, w_router, b_router, seq):
    n, d = x2d.shape
    da = a2d.shape[1]
    ne = w_router.shape[1]
    w = w_out.astype(BF16)
    rw = jnp.pad(w_router, ((0, 0), (0, LANES - ne)))
    rw_hilo = jnp.concatenate(_split_bf16(rw), axis=1)
    rb = jnp.pad(b_router, (0, LANES - ne), constant_values=NEG_BIG).reshape(1, LANES)
    tm = OUTPROJ_TILE
    tiles_per_seq = seq // tm
    row = lambda i: (i, 0)
    per_b = lambda i: (i // tiles_per_seq, 0, 0)
    vec = pl.BlockSpec((1, 1, d), per_b)
    return pl.pallas_call(
        _outproj_kernel,
        grid=(n // tm,),
        in_specs=[pl.BlockSpec((tm, da), row), pl.BlockSpec((tm, d), row), vec,
                  _const_spec((1, d)), _const_spec((1, d)), vec, vec,
                  _const_spec(w.shape), _const_spec(rw_hilo.shape), _const_spec(rb.shape)],
        out_specs=[pl.BlockSpec((tm, d), row), pl.BlockSpec((tm, d // 2), row), pl.BlockSpec((tm, LANES), row)],
        out_shape=[jax.ShapeDtypeStruct((n, d), F32), jax.ShapeDtypeStruct((n, d // 2), U32),
                   jax.ShapeDtypeStruct((n, LANES), F32)],
        compiler_params=_cparams(("arbitrary",), VMEM_LIMIT),
        name="outproj_ln_router",
    )(a2d, x2d, g1, ln_g.reshape(1, d), ln_b.reshape(1, d), sc2, sh2, w, rw_hilo, rb)


def _gate_log(g, is_input_gate):
    g = GATE_SOFTCAP * jnp.tanh(g * (1.0 / GATE_SOFTCAP))
    log_f = jnp.minimum(g, 0.0) - jnp.log(1.0 + jnp.exp(-jnp.abs(g)))
    return jnp.where(is_input_gate, g, log_f)


def _ml_inproj_kernel(x_ref, sc_ref, sh_ref, w_ref, gw_ref, gb_ref, gwth_ref, gwtl_ref, gbt_ref,
                      proj_ref, gcol_ref, grow_ref, h_scr):
    j = pl.program_id(1)

    @pl.when(j == 0)
    def _():
        h = x_ref[...] * (1.0 + sc_ref[0]) + sh_ref[0]
        h_hi, h_lo = _split_bf16(h)
        h_scr[...] = h_hi
        nh = ML_HEADS
        g = _dot_split(h_hi, h_lo, gw_ref[...]) + gb_ref[...]
        lane = lax.broadcasted_iota(I32, g.shape, 1)
        gcol_ref[...] = _gate_log(g, lane < nh)
        gt = (_dot_nt(gwth_ref[...], h_hi) + (_dot_nt(gwth_ref[...], h_lo) + _dot_nt(gwtl_ref[...], h_hi))
              + gbt_ref[...])
        sub = lax.broadcasted_iota(I32, gt.shape, 0)
        grow_ref[...] = _gate_log(gt, sub < nh)

    proj_ref[...] = _dot(h_scr[...], w_ref[...]).astype(proj_ref.dtype)


def _ml_inproj(x2d, sc, sh, w_in, b_gates, seq):
    n, d = x2d.shape
    ng = 2 * ML_HEADS
    wide = w_in.shape[1] - ng
    w = w_in[:, :wide].astype(BF16)
    gw = w_in[:, wide:]
    gw_hilo = jnp.concatenate(_split_bf16(jnp.pad(gw, ((0, 0), (0, LANES - ng)))), axis=1)
    gwt_hi, gwt_lo = _split_bf16(gw.T)
    gb = jnp.pad(b_gates, (0, LANES - ng)).reshape(1, LANES)
    gbt = b_gates.reshape(ng, 1)
    tm, tn = 1024, 1024
    tiles_per_seq = seq // tm
    row = lambda i, j: (i, 0)
    per_b = lambda i, j: (i // tiles_per_seq, 0, 0)
    const = lambda shape: pl.BlockSpec(shape, lambda i, j: (0,) * len(shape), pipeline_mode=pl.Buffered(1))
    return pl.pallas_call(
        _ml_inproj_kernel,
        grid=(n // tm, wide // tn),
        in_specs=[pl.BlockSpec((tm, d), row),
                  pl.BlockSpec((1, 1, d), per_b), pl.BlockSpec((1, 1, d), per_b),
                  pl.BlockSpec((d, tn), lambda i, j: (0, j)),
                  const(gw_hilo.shape), const(gb.shape),
                  const(gwt_hi.shape), const(gwt_lo.shape), const(gbt.shape)],
        out_specs=[pl.BlockSpec((tm, tn), lambda i, j: (i, j)),
                   pl.BlockSpec((tm, LANES), row),
                   pl.BlockSpec((ng, tm), lambda i, j: (0, i))],
        out_shape=[jax.ShapeDtypeStruct((n, wide), BF16), jax.ShapeDtypeStruct((n, LANES), F32),
                   jax.ShapeDtypeStruct((ng, n), F32)],
        scratch_shapes=[pltpu.VMEM((tm, d), BF16)],
        compiler_params=_cparams(("arbitrary", "arbitrary"), VMEM_LIMIT),
        name="mlstm_inproj",
    )(x2d, sc, sh, w, gw_hilo, gb, gwt_hi, gwt_lo, gbt)


def _split3_bf16(x):
    a = x.astype(BF16)
    r = x - a.astype(F32)
    b = r.astype(BF16)
    c = (r - b.astype(F32)).astype(BF16)
    return a, b, c


def _mlstm_kernel(q_ref, k_ref, v_ref, og_ref, gc_ref, gr_ref, hn_ref, o_ref, c_scr, m_scr, *, L, G):
    ci = pl.program_id(2)

    @pl.when(ci == 0)
    def _():
        c_scr[...] = jnp.zeros(c_scr.shape, F32)
        m_scr[...] = jnp.zeros(m_scr.shape, F32)

    row = lax.broadcasted_iota(I32, (L, L), 0)
    col = lax.broadcasted_iota(I32, (L, L), 1)
    causal = col <= row
    tri = jnp.where(causal, 1.0, 0.0).astype(BF16)
    ones_col = jnp.where(lax.broadcasted_iota(I32, (L, LANES), 1) == 0, 1.0, 0.0).astype(BF16)
    for g in range(G):
        _mlstm_head(q_ref[:, g * ML_QK:(g + 1) * ML_QK], k_ref[:, g * ML_QK:(g + 1) * ML_QK],
                    v_ref[:, g * ML_V:(g + 1) * ML_V], og_ref[:, g * ML_V:(g + 1) * ML_V],
                    gc_ref[g], gr_ref[g], hn_ref[:, g * ML_V:(g + 1) * ML_V],
                    o_ref.at[:, g * ML_V:(g + 1) * ML_V], c_scr.at[g], m_scr.at[g],
                    causal, tri, ones_col, L)


def _mlstm_head(q, k, v, og, gc, gr, head_gain, o_ref, c_scr, m_scr, causal, tri, ones_col, L):
    v_ext = jnp.concatenate([v, ones_col], axis=1)
    li_c, lf_c = gc[:, 0:1], gc[:, 1:2]
    li_r, lf_r = gr[0:1, :], gr[1:2, :]

    lfc3 = _split3_bf16(jnp.broadcast_to(lf_c, (L, LANES)))
    b_c = (_dot(tri, lfc3[0]) + (_dot(tri, lfc3[1]) + _dot(tri, lfc3[2])))[:, 0:1]
    lfr3 = _split3_bf16(jnp.broadcast_to(lf_r, (8, L)))
    b_r = (_dot_nt(lfr3[0], tri) + (_dot_nt(lfr3[1], tri) + _dot_nt(lfr3[2], tri)))[0:1, :]

    m_prev = m_scr[...]
    dm = jnp.where(causal, b_c - (b_r - li_r), NEG_BIG)
    inter = b_c + m_prev
    m_t = jnp.maximum(inter, jnp.max(dm, axis=1, keepdims=True))
    w_intra = jnp.exp(dm - m_t)
    w_inter = jnp.exp(inter - m_t)

    kscale = ML_QK ** -0.5
    qk = _dot_nt(q, k) * (w_intra * kscale)
    c_state = c_scr[...]
    num_ext = w_inter * _dot(q, c_state.astype(BF16)) + _dot(qk.astype(BF16), v_ext)
    vdim = v.shape[1]
    num = num_ext[:, :vdim]
    den = num_ext[:, vdim:vdim + 1]
    h = num / jnp.maximum(jnp.abs(den), jnp.exp(-m_t))
    hn = h * lax.rsqrt(jnp.mean(h * h, axis=-1, keepdims=True) + EPS) * head_gain
    o_ref[...] = (hn / (1.0 + jnp.exp(-og.astype(F32)))).astype(o_ref.dtype)

    b_end = b_r[:, L - 1:L]
    d_end_r = b_end - b_r + li_r
    m_new = jnp.maximum(b_end + m_prev, jnp.max(d_end_r, axis=1, keepdims=True))
    decay = jnp.exp(b_end + m_prev - m_new)
    w_s = jnp.exp(b_end - b_c + li_c - m_new) * kscale
    kw = (k.astype(F32) * w_s).astype(BF16)
    upd = lax.dot_general(kw, v_ext, (((0,), (0,)), ((), ())), preferred_element_type=F32)
    c_scr[...] = decay * c_state + upd
    m_scr[...] = m_new


def _mlstm_scan(proj, gcol, grow, head_norm, batch, seq):
    n = proj.shape[0]
    nh = ML_HEADS
    L = ML_CHUNK
    nc = seq // L
    gc = gcol[:, :2 * nh].reshape(n, 2, nh).transpose(2, 0, 1)
    gr = grow.reshape(2, nh, n).transpose(1, 0, 2)
    hn = head_norm.reshape(1, nh * ML_V)
    G = ML_HEADS_PER_STEP
    groups = nh // G
    k_off = groups
    v_off = 2 * nh * ML_QK // (G * ML_V)
    og_off = v_off + groups
    tok = lambda b, h, c: b * nc + c
    return pl.pallas_call(
        functools.partial(_mlstm_kernel, L=L, G=G),
        grid=(batch, groups, nc),
        in_specs=[pl.BlockSpec((L, G * ML_QK), lambda b, h, c: (tok(b, h, c), h)),
                  pl.BlockSpec((L, G * ML_QK), lambda b, h, c: (tok(b, h, c), k_off + h)),
                  pl.BlockSpec((L, G * ML_V), lambda b, h, c: (tok(b, h, c), v_off + h)),
                  pl.BlockSpec((L, G * ML_V), lambda b, h, c: (tok(b, h, c), og_off + h)),
                  pl.BlockSpec((G, L, 2), lambda b, h, c: (h, tok(b, h, c), 0)),
                  pl.BlockSpec((G, 2, L), lambda b, h, c: (h, 0, tok(b, h, c))),
                  pl.BlockSpec((1, G * ML_V), lambda b, h, c: (0, h))],
        out_specs=pl.BlockSpec((L, G * ML_V), lambda b, h, c: (tok(b, h, c), h)),
        out_shape=jax.ShapeDtypeStruct((n, nh * ML_V), BF16),
        scratch_shapes=[pltpu.VMEM((G, ML_QK, ML_V + LANES), F32), pltpu.VMEM((G, 1, 1), F32)],
        compiler_params=_cparams(("arbitrary", "arbitrary", "arbitrary"), VMEM_LIMIT),
        name="mlstm_scan",
    )(proj, proj, proj, proj, gc, gr, hn)


def _route_kernel(lg_ref, dest_ref, gate_ref, meta_ref, blk_ref, tri_scr, cnt_scr, run_scr, pst_scr,
                  *, T, E, K, BLK, NBP):
    ph = pl.program_id(0)
    t = pl.program_id(1)
    nt = pl.num_programs(1)

    @pl.when((ph == 0) & (t == 0))
    def _():
        r = lax.broadcasted_iota(I32, (T, T), 0)
        c = lax.broadcasted_iota(I32, (T, T), 1)
        tri_scr[...] = jnp.where(c < r, 1.0, 0.0).astype(BF16)
        cnt_scr[...] = jnp.zeros(cnt_scr.shape, F32)

    lane = lax.broadcasted_iota(I32, (T, LANES), 1)
    l = jnp.where(lane < E, lg_ref[...], -jnp.inf)
    vals, hots = [], []
    for _ in range(K):
        mx = jnp.max(l, axis=1, keepdims=True)
        idx = jnp.min(jnp.where(l == mx, lane, LANES), axis=1, keepdims=True)
        hot = lane == idx
        vals.append(mx)
        hots.append(hot)
        l = jnp.where(hot, -jnp.inf, l)
    hot_all = jnp.zeros((T, LANES), F32)
    for hot in hots:
        hot_all = hot_all + jnp.where(hot, 1.0, 0.0)
    colsum = jnp.sum(hot_all, axis=0, keepdims=True)

    @pl.when(ph == 0)
    def _():
        cnt_scr[...] = cnt_scr[...] + colsum

    @pl.when((ph == 1) & (t == 0))
    def _():
        cnt = cnt_scr[...].astype(I32)
        pc = ((cnt + (BLK - 1)) & (-BLK)).astype(F32)
        r = lax.broadcasted_iota(I32, (LANES, LANES), 0)
        c = lax.broadcasted_iota(I32, (LANES, LANES), 1)
        upper = jnp.where(r < c, 1.0, 0.0)
        pstart = jnp.dot(jnp.broadcast_to(pc, (8, LANES)), upper, precision=HIGHEST,
                         preferred_element_type=F32)[0:1, :]
        pst_scr[...] = pstart
        run_scr[...] = jnp.zeros(run_scr.shape, F32)
        pend = pstart + pc
        meta_ref[0:1, :] = cnt_scr[...].astype(I32)
        meta_ref[1:2, :] = pstart.astype(I32)
        meta_ref[2:3, :] = pend.astype(I32)
        meta_ref[3:8, :] = jnp.zeros((5, LANES), I32)
        jstart = (lax.broadcasted_iota(I32, (NBP, LANES), 0) * BLK).astype(F32)
        elane = lax.broadcasted_iota(I32, (NBP, LANES), 1)
        owned = jnp.where((pend <= jstart) & (elane < E), 1.0, 0.0)
        be = jnp.minimum(jnp.sum(owned, axis=1, keepdims=True), float(E - 1))
        blk_ref[...] = jnp.broadcast_to(be, (NBP, LANES)).astype(I32)

    @pl.when(ph == 1)
    def _():
        earlier = _dot(tri_scr[...], hot_all.astype(BF16))
        pos = earlier + (pst_scr[...] + run_scr[...])
        e0 = jnp.ones_like(vals[0])
        es = [e0] + [jnp.exp(v - vals[0]) for v in vals[1:]]
        tot = es[0]
        for e in es[1:]:
            tot = tot + e
        for kk in range(K):
            d = jnp.sum(jnp.where(hots[kk], pos, 0.0), axis=1, keepdims=True)
            dest_ref[:, kk:kk + 1] = d.astype(I32)
            gate_ref[:, kk:kk + 1] = es[kk] / tot
        run_scr[...] = run_scr[...] + colsum


def _route(logits, n_blocks):
    n = logits.shape[0]
    T = min(ROUTE_TILE, n)
    nbp = -(-n_blocks // 8) * 8
    kern = functools.partial(_route_kernel, T=T, E=N_EXPERTS, K=TOP_K, BLK=MOE_SLOT_BLOCK, NBP=nbp)
    tok_out = lambda ph, t: (t * ph, 0)
    fixed = lambda ph, t: (0, 0)
    return pl.pallas_call(
        kern,
        grid=(2, n // T),
        in_specs=[pl.BlockSpec((T, LANES), lambda ph, t: (t, 0))],
        out_specs=[pl.BlockSpec((T, TOP_K), tok_out), pl.BlockSpec((T, TOP_K), tok_out),
                   pl.BlockSpec((8, LANES), fixed), pl.BlockSpec((nbp, LANES), fixed)],
        out_shape=[jax.ShapeDtypeStruct((n, TOP_K), I32), jax.ShapeDtypeStruct((n, TOP_K), F32),
                   jax.ShapeDtypeStruct((8, LANES), I32), jax.ShapeDtypeStruct((nbp, LANES), I32)],
        scratch_shapes=[pltpu.VMEM((T, T), BF16), pltpu.VMEM((1, LANES), F32), pltpu.VMEM((1, LANES), F32),
                        pltpu.VMEM((1, LANES), F32)],
        compiler_params=_cparams(("arbitrary", "arbitrary")),
        name="moe_route",
    )(logits)


def _dispatch_kernel(meta_ref, dest_ref, h_ref, *rest, T, K, E, BLK, NB):
    if len(rest) == 4:
        xs_ref, zero_scr, sem, zsem = rest
    else:
        gu_ref, dn_ref, xs_ref, ogu_ref, odn_ref, zero_scr, sem, zsem = rest
        ogu_ref[...] = gu_ref[...].astype(BF16)
        odn_ref[...] = dn_ref[...].astype(BF16)
    i = pl.program_id(0)
    last = pl.num_programs(0) - 1

    def row_copy(r, kk):
        d = dest_ref[r * K + kk]
        return pltpu.make_async_copy(h_ref.at[pl.ds(r, 1)], xs_ref.at[pl.ds(d, 1)], sem)

    def issue(r, carry):
        for kk in range(K):
            row_copy(r, kk).start(priority=kk % 2)
        return carry

    lax.fori_loop(0, T, issue, 0, unroll=DMA_UNROLL)

    @pl.when(i == last)
    def _():
        zero_scr[...] = jnp.zeros(zero_scr.shape, zero_scr.dtype)

        def pad_copy(slot):
            return pltpu.make_async_copy(zero_scr.at[pl.ds(0, 1)], xs_ref.at[pl.ds(slot, 1)], zsem)

        for e in range(E):
            cnt = meta_ref[0, e]
            first = meta_ref[1, e] + cnt
            npad = meta_ref[2, e] - first

            def zissue(r, carry, first=first):
                pad_copy(first + r).start()
                return carry

            def zwait(r, carry, first=first):
                pad_copy(first + r).wait()
                return carry

            lax.fori_loop(0, npad, zissue, 0)
            lax.fori_loop(0, npad, zwait, 0)

        used = meta_ref[2, E - 1] // BLK

        def tail_copy(j):
            return pltpu.make_async_copy(zero_scr, xs_ref.at[pl.ds(pl.multiple_of(j * BLK, BLK), BLK)], zsem)

        def tissue(j, carry):
            tail_copy(j).start()
            return carry

        def twait(j, carry):
            tail_copy(j).wait()
            return carry

        lax.fori_loop(used, NB, tissue, 0)
        lax.fori_loop(used, NB, twait, 0)

    def drain(r, carry):
        for kk in range(K):
            row_copy(r, kk).wait()
        return carry

    lax.fori_loop(0, T, drain, 0, unroll=DMA_UNROLL)


def _cast_chunk_specs(w_all, layer, chunks, step_of):
    rows = w_all.shape[1] * w_all.shape[2]
    cols = w_all.shape[3]
    chunk = rows // chunks
    first = layer * chunks
    in_spec = pl.BlockSpec((chunk, cols), lambda *a: (first + jnp.minimum(step_of(*a), chunks - 1), 0))
    out_spec = pl.BlockSpec((chunk, cols), lambda *a: (jnp.minimum(step_of(*a), chunks - 1), 0))
    return (in_spec, out_spec, jax.ShapeDtypeStruct((rows, cols), BF16),
            w_all.reshape(w_all.shape[0] * rows, cols))


def _dispatch(h2, dest_flat, meta, n_blocks, cast=None):
    n, d = h2.shape
    T = MOE_ROW_TILE
    blk = MOE_SLOT_BLOCK
    steps = n // T
    kern = functools.partial(_dispatch_kernel, T=T, K=TOP_K, E=N_EXPERTS, BLK=blk, NB=n_blocks)
    in_specs = [pl.BlockSpec((T * TOP_K,), lambda i, m: (i,), memory_space=pltpu.SMEM),
                pl.BlockSpec((T, d), lambda i, m: (i, 0))]
    out_specs = [pl.BlockSpec(memory_space=pl.ANY)]
    out_shape = [jax.ShapeDtypeStruct((n_blocks * blk, d), h2.dtype)]
    operands = [meta, dest_flat, h2]
    if cast is not None:
        chunks = 1 << (steps.bit_length() - 1)
        for w_all in cast[:2]:
            i_spec, o_spec, o_shape, flat = _cast_chunk_specs(w_all, cast[2], chunks, lambda i, m: i)
            in_specs.append(i_spec)
            out_specs.append(o_spec)
            out_shape.append(o_shape)
            operands.append(flat)
    gs = pltpu.PrefetchScalarGridSpec(
        num_scalar_prefetch=1,
        grid=(steps,),
        in_specs=in_specs,
        out_specs=out_specs,
        scratch_shapes=[pltpu.VMEM((blk, d), h2.dtype), pltpu.SemaphoreType.DMA(()),
                        pltpu.SemaphoreType.DMA(())],
    )
    outs = pl.pallas_call(
        kern,
        grid_spec=gs,
        out_shape=out_shape,
        compiler_params=_cparams(("arbitrary",), VMEM_LIMIT),
        name="moe_dispatch",
    )(*operands)
    if cast is None:
        return outs[0], None
    return outs[0], (outs[1].reshape(cast[0].shape[1:]), outs[2].reshape(cast[1].shape[1:]))


def _expert_kernel(be_ref, used_ref, xs_ref, wgu_ref, bgu_ref, wd_ref, bd_ref, *rest):
    j = pl.program_id(0)
    if len(rest) == 1:
        (ys_ref,) = rest
    else:
        ngu_ref, ndn_ref, ys_ref, ogu_ref, odn_ref = rest
        ogu_ref[...] = ngu_ref[...].astype(BF16)
        odn_ref[...] = ndn_ref[...].astype(BF16)

    @pl.when(j < used_ref[0])
    def _():
        x_lo, x_hi = _unpack_rows(xs_ref[...])
        x = jnp.concatenate([x_lo.astype(BF16), x_hi.astype(BF16)], axis=1)
        gu = _dot(x, wgu_ref[0]) + bgu_ref[0]
        de = gu.shape[1] // 2
        a = jnp.minimum(gu[:, :de], SWIGLU_LIMIT)
        u = jnp.clip(gu[:, de:], -SWIGLU_LIMIT, SWIGLU_LIMIT)
        act = (u + 1.0) * (a / (1.0 + jnp.exp(-SWIGLU_ALPHA * a)))
        ys_ref[...] = _pack_rows(_dot(act.astype(BF16), wd_ref[0]) + bd_ref[0])

    @pl.when(j >= used_ref[0])
    def _():
        ys_ref[...] = jnp.zeros(ys_ref.shape, ys_ref.dtype)


def _experts(xs, block_e, used, w_gu, b_gu, w_down, b_down, cast_next=None):
    p, dw = xs.shape
    d = 2 * dw
    blk = MOE_SLOT_BLOCK
    ne, _, de2 = w_gu.shape
    nb = p // blk
    last_used = lambda j, be, used: jnp.minimum(j, jnp.maximum(used[0] - 1, 0))
    in_specs = [pl.BlockSpec((blk, dw), lambda j, be, used: (last_used(j, be, used), 0)),
                pl.BlockSpec((1, d, de2), lambda j, be, used: (be[j], 0, 0)),
                pl.BlockSpec((1, 1, de2), lambda j, be, used: (be[j], 0, 0)),
                pl.BlockSpec((1, de2 // 2, d), lambda j, be, used: (be[j], 0, 0)),
                pl.BlockSpec((1, 1, d), lambda j, be, used: (be[j], 0, 0))]
    out_specs = [pl.BlockSpec((blk, dw), lambda j, be, used: (j, 0))]
    out_shape = [jax.ShapeDtypeStruct((p, dw), U32)]
    operands = [block_e, used, xs, w_gu, b_gu.reshape(ne, 1, de2), w_down, b_down.reshape(ne, 1, d)]
    if cast_next is not None:
        chunks = 1 << (min(nb, CAST_CHUNKS).bit_length() - 1)
        for w_all in cast_next[:2]:
            i_spec, o_spec, o_shape, flat = _cast_chunk_specs(w_all, cast_next[2], chunks,
                                                              lambda j, be, used: j)
            in_specs.append(i_spec)
            out_specs.append(o_spec)
            out_shape.append(o_shape)
            operands.append(flat)
    gs = pltpu.PrefetchScalarGridSpec(num_scalar_prefetch=2, grid=(nb,), in_specs=in_specs, out_specs=out_specs)
    outs = pl.pallas_call(
        _expert_kernel,
        grid_spec=gs,
        out_shape=out_shape,
        compiler_params=_cparams(("arbitrary",), VMEM_LIMIT),
        name="moe_experts",
    )(*operands)
    if cast_next is None:
        return outs[0], None
    gu_all, down_all, _ = cast_next
    return outs[0], (outs[1].reshape(gu_all.shape[1:]), outs[2].reshape(down_all.shape[1:]))


def _combine_kernel(dest_ref, dest_next_ref, ys_ref, gate_ref, x_ref, g2_ref, lg_ref, lb_ref, o_ref, buf, sems,
                    *, T, K):
    i = pl.program_id(0)
    n = pl.num_programs(0)
    slot = i % 2

    def row_copy(idx_ref, dst_slot, r, kk):
        d = idx_ref[r * K + kk]
        return pltpu.make_async_copy(ys_ref.at[pl.ds(d, 1)], buf.at[dst_slot, kk, pl.ds(r, 1)],
                                     sems.at[dst_slot])

    def issue_tile(idx_ref, dst_slot):
        def issue(r, carry):
            for kk in range(K):
                row_copy(idx_ref, dst_slot, r, kk).start(priority=kk % 2)
            return carry
        lax.fori_loop(0, T, issue, 0, unroll=DMA_UNROLL)

    @pl.when(i == 0)
    def _():
        issue_tile(dest_ref, slot)

    @pl.when(i + 1 < n)
    def _():
        issue_tile(dest_next_ref, 1 - slot)

    def drain(r, carry):
        for kk in range(K):
            row_copy(dest_ref, slot, r, kk).wait()
        return carry

    lax.fori_loop(0, T, drain, 0, unroll=DMA_UNROLL)
    g = gate_ref[...]
    y_lo = y_hi = None
    for kk in range(K):
        lo, hi = _unpack_rows(buf[slot, kk])
        gk = g[:, kk:kk + 1]
        y_lo = gk * lo if y_lo is None else y_lo + gk * lo
        y_hi = gk * hi if y_hi is None else y_hi + gk * hi
    y = jnp.concatenate([y_lo, y_hi], axis=1)
    z = ALPHA * x_ref[...] + (1.0 + g2_ref[0]) * y
    o_ref[...] = _layer_norm(z, lg_ref[...], lb_ref[...])


def _combine(ys, dest_flat, gates, x1, g2, ln_g, ln_b, seq):
    n, d = x1.shape
    T = MOE_ROW_TILE
    tiles_per_seq = seq // T
    row = lambda i: (i, 0)
    last = n // T - 1
    return pl.pallas_call(
        functools.partial(_combine_kernel, T=T, K=TOP_K),
        grid=(n // T,),
        in_specs=[pl.BlockSpec((T * TOP_K,), lambda i: (i,), memory_space=pltpu.SMEM),
                  pl.BlockSpec((T * TOP_K,), lambda i: (jnp.minimum(i + 1, last),), memory_space=pltpu.SMEM),
                  pl.BlockSpec(memory_space=pl.ANY),
                  pl.BlockSpec((T, TOP_K), row), pl.BlockSpec((T, d), row),
                  pl.BlockSpec((1, 1, d), lambda i: (i // tiles_per_seq, 0, 0)),
                  _const_spec((1, d)), _const_spec((1, d))],
        out_specs=pl.BlockSpec((T, d), row),
        out_shape=jax.ShapeDtypeStruct((n, d), F32),
        scratch_shapes=[pltpu.VMEM((2, TOP_K, T, ys.shape[1]), ys.dtype), pltpu.SemaphoreType.DMA((2,))],
        compiler_params=_cparams(("arbitrary",), VMEM_LIMIT),
        name="moe_combine",
    )(dest_flat, dest_flat, ys, gates, x1, g2, ln_g.reshape(1, d), ln_b.reshape(1, d))


def _moe(h2, logits, x1, g2, ln_g, ln_b, layer, weights, w_gu_f32, b_gu, w_down_f32, b_down, seq):
    n = h2.shape[0]
    blk = MOE_SLOT_BLOCK
    n_blocks = -(-(n * TOP_K + N_EXPERTS * (blk - 1)) // blk)
    dest, gates, meta, block_e = _route(logits, n_blocks)
    dest_flat = dest.reshape(n * TOP_K)
    cast_here = (w_gu_f32, w_down_f32, layer) if weights is None else None
    xs, cast_weights = _dispatch(h2, dest_flat, meta, n_blocks, cast_here)
    w_gu, w_down = cast_weights if weights is None else weights
    used = (meta[2, N_EXPERTS - 1] // blk).reshape(1)
    cast_next = (w_gu_f32, w_down_f32, layer + 1) if layer + 1 < w_gu_f32.shape[0] else None
    ys, next_weights = _experts(xs, block_e[:n_blocks, 0], used, w_gu, b_gu[layer], w_down, b_down[layer],
                                cast_next)
    return _combine(ys, dest_flat, gates, x1, g2, ln_g, ln_b, seq), next_weights


def kernel(x, c, positions, ada_w, ada_b, ln_g, ln_b, mla_w_in, mla_q_norm, mla_w_uq, mla_kv_norm, mla_w_ukv,
           mla_w_out, ml_w_in, ml_b_gates, ml_head_norm, ml_w_out, moe_w_router, moe_b_router, moe_w_gu,
           moe_b_gu, moe_w_down, moe_b_down):
    batch, seq, d = x.shape
    n = batch * seq
    mods = _mods(c, ada_w, ada_b)
    xf = x.reshape(n, d)
    expert_w = None
    for i in range(DEPTH):
        j = i // 2
        sh1, sc1, g1, sh2, sc2, g2 = [mods[i, :, s * d:(s + 1) * d].reshape(batch, 1, d) for s in range(6)]
        if i % 2 == 0:
            cos, sin = _rope_tables(positions)
            q, k, v = _mla_front(xf, sc1, sh1, cos, sin, mla_w_in[j], mla_q_norm[j], mla_w_uq[j],
                                 mla_kv_norm[j], mla_w_ukv[j], batch, seq)
            a = _flash_attention(q, k, v).reshape(n, MLA_HEADS * MLA_V)
            w_out = mla_w_out[j]
        else:
            proj, gcol, grow = _ml_inproj(xf, sc1, sh1, ml_w_in[j], ml_b_gates[j], seq)
            a = _mlstm_scan(proj, gcol, grow, ml_head_norm[j], batch, seq)
            w_out = ml_w_out[j]
        x1, h2, logits = _outproj_ln_router(a, xf, w_out, g1, ln_g[i, 0], ln_b[i, 0], sc2, sh2,
                                            moe_w_router[i], moe_b_router[i], seq)
        xf, expert_w = _moe(h2, logits, x1, g2, ln_g[i, 1], ln_b[i, 1], i, expert_w, moe_w_gu, moe_b_gu,
                            moe_w_down, moe_b_down, seq)
    return xf.reshape(batch, seq, d)
```

```python
import functools
import math

import jax
import jax.numpy as jnp
from jax import lax
from jax.experimental import pallas as pl
from jax.experimental.pallas import tpu as pltpu

F32 = jnp.float32
BF16 = jnp.bfloat16
I32 = jnp.int32
U32 = jnp.uint32
HIGHEST = lax.Precision.HIGHEST
HIGH_HALF = 0xFFFF0000

CHUNK = 64
MLA_HEADS = 16
MLA_NOPE = 128
MLA_ROPE = 64
MLA_V = 128
MLA_Q_RANK = 448
MLA_KV_RANK = 512
ROPE_THETA = 10000.0
ML_HEADS = 4
ML_QK = 256
ML_V = 512
GATE_SOFTCAP = 15.0
N_EXPERTS = 32
TOP_K = 4
D_EXPERT = 1024
SWIGLU_LIMIT = 7.0
SWIGLU_ALPHA = 1.702
DEPTH = 2
ALPHA = (2 * DEPTH) ** 0.25
EPS = 1e-6
NEG_BIG = -1e30

LANES = 128
VMEM_LIMIT = 56 * 1024 * 1024

Q_RANK_PAD = 512
ATT_BLOCK = 512
ATT_KBLOCK = 1024
ATT_CHAINS = 8
ML_CHUNK = 256
ML_HEADS_PER_STEP = 4
MOE_SLOT_BLOCK = 512
CAST_CHUNKS = 256
ROUTE_TILE = 1024
ROW_TILE = 256
MOE_ROW_TILE = 512
OUTPROJ_TILE = 512
DMA_UNROLL = 8


def _cparams(sem, vmem=None):
    return pltpu.CompilerParams(dimension_semantics=sem, vmem_limit_bytes=vmem)


def _const_spec(shape):
    nd = len(shape)
    return pl.BlockSpec(shape, lambda *_: (0,) * nd, pipeline_mode=pl.Buffered(1))


def _split_bf16(x):
    hi = x.astype(BF16)
    lo = (x - hi.astype(F32)).astype(BF16)
    return hi, lo


def _dot(a, b):
    return jnp.dot(a, b, preferred_element_type=F32)


def _dot_nt(a, b):
    return lax.dot_general(a, b, (((1,), (1,)), ((), ())), preferred_element_type=F32)


def _dot_split(x_hi, x_lo, w_hilo):
    p = _dot(x_hi, w_hilo)
    return p[:, :LANES] + (p[:, LANES:] + _dot(x_lo, w_hilo[:, :LANES]))


def _pack_rows(x):
    w = x.shape[1] // 2
    lo = pltpu.bitcast(x[:, :w].astype(BF16).astype(F32), U32)
    hi = pltpu.bitcast(x[:, w:].astype(BF16).astype(F32), U32)
    return (hi & jnp.uint32(HIGH_HALF)) | lax.shift_right_logical(lo, jnp.uint32(16))


def _unpack_rows(words):
    lo = pltpu.bitcast(lax.shift_left(words, jnp.uint32(16)), F32)
    hi = pltpu.bitcast(words & jnp.uint32(HIGH_HALF), F32)
    return lo, hi


def _layer_norm(z, g, b):
    mu = jnp.mean(z, axis=-1, keepdims=True)
    zc = z - mu
    var = jnp.mean(zc * zc, axis=-1, keepdims=True)
    return zc * lax.rsqrt(var + EPS) * g + b


def _mods_kernel(c_ref, w_ref, b_ref, o_ref):
    c = c_ref[...]
    ca = c / (1.0 + jnp.exp(-c))
    o_ref[0] = jnp.dot(ca, w_ref[0], precision=HIGHEST, preferred_element_type=F32) + b_ref[0]


def _mods(c, ada_w, ada_b):
    depth, d, d6 = ada_w.shape
    b = c.shape[0]
    rows = 8
    c8 = jnp.pad(c, ((0, rows - b), (0, 0)))
    tn = 1024
    out = pl.pallas_call(
        _mods_kernel,
        grid=(depth, d6 // tn),
        in_specs=[pl.BlockSpec((rows, d), lambda i, j: (0, 0)),
                  pl.BlockSpec((1, d, tn), lambda i, j: (i, 0, j)),
                  pl.BlockSpec((1, 1, tn), lambda i, j: (i, 0, j))],
        out_specs=pl.BlockSpec((1, rows, tn), lambda i, j: (i, 0, j)),
        out_shape=jax.ShapeDtypeStruct((depth, rows, d6), F32),
        compiler_params=_cparams(("arbitrary", "arbitrary"), VMEM_LIMIT),
        name="adaln_mods",
    )(c8, ada_w, ada_b.reshape(depth, 1, d6))
    return out[:, :b]


def _rope_kernel(pos_ref, inv_ref, cos_ref, sin_ref):
    ang = pos_ref[...].astype(F32) * inv_ref[...]
    cos_ref[...] = jnp.cos(ang)
    sin_ref[...] = jnp.sin(ang)


def _rope_tables(positions):
    n = positions.size
    half = MLA_ROPE // 2
    inv = ROPE_THETA ** (-jnp.arange(0, MLA_ROPE, 2, dtype=F32) / MLA_ROPE)
    inv_row = jnp.tile(inv, LANES // half).reshape(1, LANES)
    pos = jnp.broadcast_to(positions.reshape(n, 1), (n, LANES))
    tm = 1024
    spec = pl.BlockSpec((tm, LANES), lambda i: (i, 0))
    return pl.pallas_call(
        _rope_kernel,
        grid=(n // tm,),
        in_specs=[spec, pl.BlockSpec((1, LANES), lambda i: (0, 0))],
        out_specs=[spec, spec],
        out_shape=[jax.ShapeDtypeStruct((n, LANES), F32)] * 2,
        compiler_params=_cparams(("arbitrary",)),
        name="rope_tables",
    )(pos, inv_row)


def _mla_front_kernel(x_ref, sc_ref, sh_ref, cos_ref, sin_ref, win_ref, qn_ref, kvn_ref, wq_ref, wkv_ref,
                      q_ref, k_ref, v_ref, *, qscale):
    h = (x_ref[...] * (1.0 + sc_ref[0]) + sh_ref[0]).astype(BF16)
    proj = _dot(h, win_ref[...])
    cq = proj[:, :Q_RANK_PAD]
    ckv = proj[:, Q_RANK_PAD:Q_RANK_PAD + MLA_KV_RANK]
    cq = cq * lax.rsqrt(jnp.sum(cq * cq, -1, keepdims=True) * (1.0 / MLA_Q_RANK) + EPS) * qn_ref[...]
    ckv = ckv * lax.rsqrt(jnp.mean(ckv * ckv, -1, keepdims=True) + EPS) * kvn_ref[...]
    cq = cq.astype(BF16)
    ckv = ckv.astype(BF16)
    cos = cos_ref[...]
    sin = sin_ref[...]
    o = Q_RANK_PAD + MLA_KV_RANK
    kr = proj[:, o:o + LANES] * cos + proj[:, o + LANES:o + 2 * LANES] * sin
    lane = lax.broadcasted_iota(I32, kr.shape, 1)
    kr_even = jnp.where(lane < MLA_ROPE, kr, 0.0).astype(BF16)
    kr_odd = jnp.where(lane >= MLA_ROPE, kr, 0.0).astype(BF16)
    ones_col = jnp.where(lane == 0, 1.0, 0.0).astype(BF16)

    nheads = MLA_HEADS
    nope_w = nheads * MLA_NOPE
    rope_w = nheads * MLA_ROPE
    group = 4
    for g in range(nheads // group):
        qn = _dot(cq, wq_ref[:, g * group * MLA_NOPE:(g + 1) * group * MLA_NOPE]) * qscale
        kv = _dot(ckv, wkv_ref[:, g * group * 2 * LANES:(g + 1) * group * 2 * LANES])
        for j in range(group):
            hh = g * group + j
            q_ref[0, hh, :, :LANES] = qn[:, j * LANES:(j + 1) * LANES].astype(BF16)
            k_ref[0, hh, :, :LANES] = kv[:, 2 * j * LANES:(2 * j + 1) * LANES].astype(BF16)
            v_ref[0, hh, :, :LANES] = kv[:, (2 * j + 1) * LANES:(2 * j + 2) * LANES].astype(BF16)
            k_ref[0, hh, :, LANES:] = kr_even if hh % 2 == 0 else kr_odd
            v_ref[0, hh, :, LANES:] = ones_col
    cos2 = jnp.concatenate([cos, cos], axis=1)
    sin2 = jnp.concatenate([sin, sin], axis=1)
    for p2 in range(nheads // 4):
        lo = nope_w + p2 * 2 * LANES
        qr = (_dot(cq, wq_ref[:, lo:lo + 2 * LANES]) * cos2
              + _dot(cq, wq_ref[:, rope_w + lo:rope_w + lo + 2 * LANES]) * sin2) * qscale
        qr = qr.astype(BF16)
        for half_slab in range(2):
            pair = 2 * p2 + half_slab
            slab = qr[:, half_slab * LANES:(half_slab + 1) * LANES]
            q_ref[0, 2 * pair, :, LANES:] = slab
            q_ref[0, 2 * pair + 1, :, LANES:] = slab


def _mla_front(x2d, sc, sh, cos, sin, w_in, q_norm, w_uq, kv_norm, w_ukv, batch, seq):
    n, d = x2d.shape
    nh = MLA_HEADS
    half = MLA_ROPE // 2
    wq_lat = jnp.pad(w_in[:, :MLA_Q_RANK], ((0, 0), (0, Q_RANK_PAD - MLA_Q_RANK)))
    wkv_lat = w_in[:, MLA_Q_RANK:MLA_Q_RANK + MLA_KV_RANK]
    wkr = w_in[:, MLA_Q_RANK + MLA_KV_RANK:]
    wkr_rot = jnp.concatenate([-wkr[:, half:], wkr[:, :half]], 1)
    win = jnp.concatenate([wq_lat, wkv_lat, wkr, wkr, wkr_rot, wkr_rot], 1).astype(BF16)
    wq3 = w_uq.reshape(MLA_Q_RANK, nh, MLA_NOPE + MLA_ROPE)
    wq_nope = wq3[:, :, :MLA_NOPE].reshape(MLA_Q_RANK, nh * MLA_NOPE)
    wq_r = wq3[:, :, MLA_NOPE:]
    wq_rope = wq_r.reshape(MLA_Q_RANK, nh * MLA_ROPE)
    wq_rot = jnp.concatenate([-wq_r[:, :, half:], wq_r[:, :, :half]], -1).reshape(MLA_Q_RANK, nh * MLA_ROPE)
    wq = jnp.pad(jnp.concatenate([wq_nope, wq_rope, wq_rot], 1),
                 ((0, Q_RANK_PAD - MLA_Q_RANK), (0, 0))).astype(BF16)
    wkv = w_ukv.astype(BF16)
    qn = jnp.pad(q_norm, (0, Q_RANK_PAD - MLA_Q_RANK)).reshape(1, Q_RANK_PAD)
    kvn = kv_norm.reshape(1, MLA_KV_RANK)
    qscale = (MLA_NOPE + MLA_ROPE) ** -0.5 * math.log2(math.e)

    tm = ROW_TILE
    tiles_per_seq = seq // tm
    row = lambda i: (i, 0)
    per_b = lambda i: (i // tiles_per_seq, 0, 0)
    head_out = pl.BlockSpec((1, nh, tm, 2 * LANES), lambda i: (i // tiles_per_seq, 0, i % tiles_per_seq, 0))
    out_sds = jax.ShapeDtypeStruct((batch, nh, seq, 2 * LANES), BF16)
    return pl.pallas_call(
        functools.partial(_mla_front_kernel, qscale=qscale),
        grid=(n // tm,),
        in_specs=[pl.BlockSpec((tm, d), row),
                  pl.BlockSpec((1, 1, d), per_b), pl.BlockSpec((1, 1, d), per_b),
                  pl.BlockSpec((tm, LANES), row), pl.BlockSpec((tm, LANES), row),
                  _const_spec(win.shape), _const_spec(qn.shape), _const_spec(kvn.shape),
                  _const_spec(wq.shape), _const_spec(wkv.shape)],
        out_specs=[head_out, head_out, head_out],
        out_shape=[out_sds, out_sds, out_sds],
        compiler_params=_cparams(("arbitrary",), VMEM_LIMIT),
        name="mla_front",
    )(x2d, sc, sh, cos, sin, win, qn, kvn, wq, wkv)


def _flash_kernel(q_ref, k_ref, v_ref, o_ref, m_ref, acc_ref, p_last, a_last, *, blk, kblk, nsub):
    i = pl.program_id(2)
    m_ref[...] = jnp.full(m_ref.shape, NEG_BIG, F32)
    acc_ref[...] = jnp.zeros(acc_ref.shape, F32)
    p_last[...] = jnp.zeros(p_last.shape, BF16)
    a_last[...] = jnp.ones(a_last.shape, F32)
    slabs = kblk // LANES
    ratio = kblk // blk
    shift = CHUNK.bit_length() - 1
    everyone = range(nsub)
    last = nsub - 1

    def accumulate(c, alpha, p, kb):
        v = v_ref[0, 0, pl.ds(pl.multiple_of(kb * kblk, kblk), kblk), :]
        pv = _dot(p, v)
        acc_ref[c, :, :LANES] = alpha * acc_ref[c, :, :LANES] + pv[:, :LANES]
        acc_ref[c, :, LANES:] = alpha * acc_ref[c, :, LANES:] + pv[:, LANES:]

    def step(c, kb, masked, defer=False):
        start = pl.multiple_of(kb * kblk, kblk)
        q = q_ref[0, 0, c * blk:(c + 1) * blk, :]
        k = k_ref[0, 0, pl.ds(start, kblk), :]
        s = _dot_nt(q, k)
        if masked:
            r = lax.shift_right_logical(lax.broadcasted_iota(I32, s.shape, 0) + (c % ratio) * blk, shift)
            cc = lax.shift_right_logical(lax.broadcasted_iota(I32, s.shape, 1), shift)
            s = jnp.where(cc <= r, s, NEG_BIG)
        m_prev = m_ref[c]
        m_new = jnp.maximum(m_prev, jnp.max(s, axis=1, keepdims=True))
        p = jnp.concatenate([jnp.exp2(s[:, j * LANES:(j + 1) * LANES] - m_new).astype(BF16)
                             for j in range(slabs)], axis=1)
        alpha = jnp.exp2(m_prev - m_new)
        m_ref[c] = m_new
        if defer:
            p_last[...] = p
            a_last[...] = alpha
        else:
            accumulate(c, alpha, p, kb)

    def flush_last(kb):
        accumulate(last, a_last[...], p_last[...], jnp.maximum(kb, 0))

    def full_steps(kb, carry):
        flush_last(kb - 1)
        for c in everyone:
            step(c, kb, False, defer=(c == last))
        return carry

    first_diag = (nsub // ratio) * i
    lax.fori_loop(0, first_diag, full_steps, 0)
    flush_last(first_diag - 1)
    for t in range(nsub // ratio):
        for c in everyone:
            if c // ratio >= t:
                step(c, first_diag + t, masked=(c // ratio == t))
    for c in everyone:
        l = acc_ref[c, :, LANES:LANES + 1]
        o_ref[0, c * blk:(c + 1) * blk, :] = (acc_ref[c, :, :MLA_V] / l).astype(o_ref.dtype)


def _flash_attention(q, k, v):
    batch, nh, seq, dk = q.shape
    blk = ATT_BLOCK
    nsub = ATT_CHAINS
    kblk = ATT_KBLOCK
    assert kblk % blk == 0 and nsub % (kblk // blk) == 0
    qspec = pl.BlockSpec((1, 1, nsub * blk, dk), lambda b, h, i: (b, h, i, 0))
    kvspec = pl.BlockSpec((1, 1, seq, dk), lambda b, h, i: (b, h, 0, 0))
    return pl.pallas_call(
        functools.partial(_flash_kernel, blk=blk, kblk=kblk, nsub=nsub),
        grid=(batch, nh, seq // (nsub * blk)),
        in_specs=[qspec, kvspec, kvspec],
        out_specs=pl.BlockSpec((1, nsub * blk, MLA_V), lambda b, h, i: (b, i, h)),
        out_shape=jax.ShapeDtypeStruct((batch, seq, nh * MLA_V), BF16),
        scratch_shapes=[pltpu.VMEM((nsub, blk, LANES), F32), pltpu.VMEM((nsub, blk, dk), F32),
                        pltpu.VMEM((blk, kblk), BF16), pltpu.VMEM((blk, LANES), F32)],
        compiler_params=_cparams(("arbitrary", "arbitrary", "arbitrary"), VMEM_LIMIT),
        name="flash_attention",
    )(q, k, v)


def _outproj_kernel(a_ref, x_ref, g1_ref, lg_ref, lb_ref, sc_ref, sh_ref, w_ref, rw_ref, rb_ref,
                    x1_ref, h2_ref, logit_ref):
    y = _dot(a_ref[...], w_ref[...])
    z = ALPHA * x_ref[...] + (1.0 + g1_ref[0]) * y
    x1 = _layer_norm(z, lg_ref[...], lb_ref[...])
    x1_ref[...] = x1
    h2 = x1 * (1.0 + sc_ref[0]) + sh_ref[0]
    h2_ref[...] = _pack_rows(h2)
    h2_hi, h2_lo = _split_bf16(h2)
    logit_ref[...] = _dot_split(h2_hi, h2_lo, rw_ref[...]) + rb_ref[...]


def _outproj_ln_router(a2d, x2d, w_out, g1, ln_g, ln_b, sc2, sh2, w_router, b_router, seq):
    n, d = x2d.shape
    da = a2d.shape[1]
    ne = w_router.shape[1]
    w = w_out.astype(BF16)
    rw = jnp.pad(w_router, ((0, 0), (0, LANES - ne)))
    rw_hilo = jnp.concatenate(_split_bf16(rw), axis=1)
    rb = jnp.pad(b_router, (0, LANES - ne), constant_values=NEG_BIG).reshape(1, LANES)
    tm = OUTPROJ_TILE
    tiles_per_seq = seq // tm
    row = lambda i: (i, 0)
    per_b = lambda i: (i // tiles_per_seq, 0, 0)
    vec = pl.BlockSpec((1, 1, d), per_b)
    return pl.pallas_call(
        _outproj_kernel,
        grid=(n // tm,),
        in_specs=[pl.BlockSpec((tm, da), row), pl.BlockSpec((tm, d), row), vec,
                  _const_spec((1, d)), _const_spec((1, d)), vec, vec,
                  _const_spec(w.shape), _const_spec(rw_hilo.shape), _const_spec(rb.shape)],
        out_specs=[pl.BlockSpec((tm, d), row), pl.BlockSpec((tm, d // 2), row), pl.BlockSpec((tm, LANES), row)],
        out_shape=[jax.ShapeDtypeStruct((n, d), F32), jax.ShapeDtypeStruct((n, d // 2), U32),
                   jax.ShapeDtypeStruct((n, LANES), F32)],
        compiler_params=_cparams(("arbitrary",), VMEM_LIMIT),
        name="outproj_ln_router",
    )(a2d, x2d, g1, ln_g.reshape(1, d), ln_b.reshape(1, d), sc2, sh2, w, rw_hilo, rb)


def _gate_log(g, is_input_gate):
    g = GATE_SOFTCAP * jnp.tanh(g * (1.0 / GATE_SOFTCAP))
    log_f = jnp.minimum(g, 0.0) - jnp.log(1.0 + jnp.exp(-jnp.abs(g)))
    return jnp.where(is_input_gate, g, log_f)


def _ml_inproj_kernel(x_ref, sc_ref, sh_ref, w_ref, gw_ref, gb_ref, gwth_ref, gwtl_ref, gbt_ref,
                      proj_ref, gcol_ref, grow_ref, h_scr):
    j = pl.program_id(1)

    @pl.when(j == 0)
    def _():
        h = x_ref[...] * (1.0 + sc_ref[0]) + sh_ref[0]
        h_hi, h_lo = _split_bf16(h)
        h_scr[...] = h_hi
        nh = ML_HEADS
        g = _dot_split(h_hi, h_lo, gw_ref[...]) + gb_ref[...]
        lane = lax.broadcasted_iota(I32, g.shape, 1)
        gcol_ref[...] = _gate_log(g, lane < nh)
        gt = (_dot_nt(gwth_ref[...], h_hi) + (_dot_nt(gwth_ref[...], h_lo) + _dot_nt(gwtl_ref[...], h_hi))
              + gbt_ref[...])
        sub = lax.broadcasted_iota(I32, gt.shape, 0)
        grow_ref[...] = _gate_log(gt, sub < nh)

    proj_ref[...] = _dot(h_scr[...], w_ref[...]).astype(proj_ref.dtype)


def _ml_inproj(x2d, sc, sh, w_in, b_gates, seq):
    n, d = x2d.shape
    ng = 2 * ML_HEADS
    wide = w_in.shape[1] - ng
    w = w_in[:, :wide].astype(BF16)
    gw = w_in[:, wide:]
    gw_hilo = jnp.concatenate(_split_bf16(jnp.pad(gw, ((0, 0), (0, LANES - ng)))), axis=1)
    gwt_hi, gwt_lo = _split_bf16(gw.T)
    gb = jnp.pad(b_gates, (0, LANES - ng)).reshape(1, LANES)
    gbt = b_gates.reshape(ng, 1)
    tm, tn = 1024, 1024
    tiles_per_seq = seq // tm
    row = lambda i, j: (i, 0)
    per_b = lambda i, j: (i // tiles_per_seq, 0, 0)
    const = lambda shape: pl.BlockSpec(shape, lambda i, j: (0,) * len(shape), pipeline_mode=pl.Buffered(1))
    return pl.pallas_call(
        _ml_inproj_kernel,
        grid=(n // tm, wide // tn),
        in_specs=[pl.BlockSpec((tm, d), row),
                  pl.BlockSpec((1, 1, d), per_b), pl.BlockSpec((1, 1, d), per_b),
                  pl.BlockSpec((d, tn), lambda i, j: (0, j)),
                  const(gw_hilo.shape), const(gb.shape),
                  const(gwt_hi.shape), const(gwt_lo.shape), const(gbt.shape)],
        out_specs=[pl.BlockSpec((tm, tn), lambda i, j: (i, j)),
                   pl.BlockSpec((tm, LANES), row),
                   pl.BlockSpec((ng, tm), lambda i, j: (0, i))],
        out_shape=[jax.ShapeDtypeStruct((n, wide), BF16), jax.ShapeDtypeStruct((n, LANES), F32),
                   jax.ShapeDtypeStruct((ng, n), F32)],
        scratch_shapes=[pltpu.VMEM((tm, d), BF16)],
        compiler_params=_cparams(("arbitrary", "arbitrary"), VMEM_LIMIT),
        name="mlstm_inproj",
    )(x2d, sc, sh, w, gw_hilo, gb, gwt_hi, gwt_lo, gbt)


def _split3_bf16(x):
    a = x.astype(BF16)
    r = x - a.astype(F32)
    b = r.astype(BF16)
    c = (r - b.astype(F32)).astype(BF16)
    return a, b, c


def _mlstm_kernel(q_ref, k_ref, v_ref, og_ref, gc_ref, gr_ref, hn_ref, o_ref, c_scr, m_scr, *, L, G):
    ci = pl.program_id(2)

    @pl.when(ci == 0)
    def _():
        c_scr[...] = jnp.zeros(c_scr.shape, F32)
        m_scr[...] = jnp.zeros(m_scr.shape, F32)

    row = lax.broadcasted_iota(I32, (L, L), 0)
    col = lax.broadcasted_iota(I32, (L, L), 1)
    causal = col <= row
    tri = jnp.where(causal, 1.0, 0.0).astype(BF16)
    ones_col = jnp.where(lax.broadcasted_iota(I32, (L, LANES), 1) == 0, 1.0, 0.0).astype(BF16)
    for g in range(G):
        _mlstm_head(q_ref[:, g * ML_QK:(g + 1) * ML_QK], k_ref[:, g * ML_QK:(g + 1) * ML_QK],
                    v_ref[:, g * ML_V:(g + 1) * ML_V], og_ref[:, g * ML_V:(g + 1) * ML_V],
                    gc_ref[g], gr_ref[g], hn_ref[:, g * ML_V:(g + 1) * ML_V],
                    o_ref.at[:, g * ML_V:(g + 1) * ML_V], c_scr.at[g], m_scr.at[g],
                    causal, tri, ones_col, L)


def _mlstm_head(q, k, v, og, gc, gr, head_gain, o_ref, c_scr, m_scr, causal, tri, ones_col, L):
    v_ext = jnp.concatenate([v, ones_col], axis=1)
    li_c, lf_c = gc[:, 0:1], gc[:, 1:2]
    li_r, lf_r = gr[0:1, :], gr[1:2, :]

    lfc3 = _split3_bf16(jnp.broadcast_to(lf_c, (L, LANES)))
    b_c = (_dot(tri, lfc3[0]) + (_dot(tri, lfc3[1]) + _dot(tri, lfc3[2])))[:, 0:1]
    lfr3 = _split3_bf16(jnp.broadcast_to(lf_r, (8, L)))
    b_r = (_dot_nt(lfr3[0], tri) + (_dot_nt(lfr3[1], tri) + _dot_nt(lfr3[2], tri)))[0:1, :]

    m_prev = m_scr[...]
    dm = jnp.where(causal, b_c - (b_r - li_r), NEG_BIG)
    inter = b_c + m_prev
    m_t = jnp.maximum(inter, jnp.max(dm, axis=1, keepdims=True))
    w_intra = jnp.exp(dm - m_t)
    w_inter = jnp.exp(inter - m_t)

    kscale = ML_QK ** -0.5
    qk = _dot_nt(q, k) * (w_intra * kscale)
    c_state = c_scr[...]
    num_ext = w_inter * _dot(q, c_state.astype(BF16)) + _dot(qk.astype(BF16), v_ext)
    vdim = v.shape[1]
    num = num_ext[:, :vdim]
    den = num_ext[:, vdim:vdim + 1]
    h = num / jnp.maximum(jnp.abs(den), jnp.exp(-m_t))
    hn = h * lax.rsqrt(jnp.mean(h * h, axis=-1, keepdims=True) + EPS) * head_gain
    o_ref[...] = (hn / (1.0 + jnp.exp(-og.astype(F32)))).astype(o_ref.dtype)

    b_end = b_r[:, L - 1:L]
    d_end_r = b_end - b_r + li_r
    m_new = jnp.maximum(b_end + m_prev, jnp.max(d_end_r, axis=1, keepdims=True))
    decay = jnp.exp(b_end + m_prev - m_new)
    w_s = jnp.exp(b_end - b_c + li_c - m_new) * kscale
    kw = (k.astype(F32) * w_s).astype(BF16)
    upd = lax.dot_general(kw, v_ext, (((0,), (0,)), ((), ())), preferred_element_type=F32)
    c_scr[...] = decay * c_state + upd
    m_scr[...] = m_new


def _mlstm_scan(proj, gcol, grow, head_norm, batch, seq):
    n = proj.shape[0]
    nh = ML_HEADS
    L = ML_CHUNK
    nc = seq // L
    gc = gcol[:, :2 * nh].reshape(n, 2, nh).transpose(2, 0, 1)
    gr = grow.reshape(2, nh, n).transpose(1, 0, 2)
    hn = head_norm.reshape(1, nh * ML_V)
    G = ML_HEADS_PER_STEP
    groups = nh // G
    k_off = groups
    v_off = 2 * nh * ML_QK // (G * ML_V)
    og_off = v_off + groups
    tok = lambda b, h, c: b * nc + c
    return pl.pallas_call(
        functools.partial(_mlstm_kernel, L=L, G=G),
        grid=(batch, groups, nc),
        in_specs=[pl.BlockSpec((L, G * ML_QK), lambda b, h, c: (tok(b, h, c), h)),
                  pl.BlockSpec((L, G * ML_QK), lambda b, h, c: (tok(b, h, c), k_off + h)),
                  pl.BlockSpec((L, G * ML_V), lambda b, h, c: (tok(b, h, c), v_off + h)),
                  pl.BlockSpec((L, G * ML_V), lambda b, h, c: (tok(b, h, c), og_off + h)),
                  pl.BlockSpec((G, L, 2), lambda b, h, c: (h, tok(b, h, c), 0)),
                  pl.BlockSpec((G, 2, L), lambda b, h, c: (h, 0, tok(b, h, c))),
                  pl.BlockSpec((1, G * ML_V), lambda b, h, c: (0, h))],
        out_specs=pl.BlockSpec((L, G * ML_V), lambda b, h, c: (tok(b, h, c), h)),
        out_shape=jax.ShapeDtypeStruct((n, nh * ML_V), BF16),
        scratch_shapes=[pltpu.VMEM((G, ML_QK, ML_V + LANES), F32), pltpu.VMEM((G, 1, 1), F32)],
        compiler_params=_cparams(("arbitrary", "arbitrary", "arbitrary"), VMEM_LIMIT),
        name="mlstm_scan",
    )(proj, proj, proj, proj, gc, gr, hn)


def _route_kernel(lg_ref, dest_ref, gate_ref, meta_ref, blk_ref, tri_scr, cnt_scr, run_scr, pst_scr,
                  *, T, E, K, BLK, NBP):
    ph = pl.program_id(0)
    t = pl.program_id(1)
    nt = pl.num_programs(1)

    @pl.when((ph == 0) & (t == 0))
    def _():
        r = lax.broadcasted_iota(I32, (T, T), 0)
        c = lax.broadcasted_iota(I32, (T, T), 1)
        tri_scr[...] = jnp.where(c < r, 1.0, 0.0).astype(BF16)
        cnt_scr[...] = jnp.zeros(cnt_scr.shape, F32)

    lane = lax.broadcasted_iota(I32, (T, LANES), 1)
    l = jnp.where(lane < E, lg_ref[...], -jnp.inf)
    vals, hots = [], []
    for _ in range(K):
        mx = jnp.max(l, axis=1, keepdims=True)
        idx = jnp.min(jnp.where(l == mx, lane, LANES), axis=1, keepdims=True)
        hot = lane == idx
        vals.append(mx)
        hots.append(hot)
        l = jnp.where(hot, -jnp.inf, l)
    hot_all = jnp.zeros((T, LANES), F32)
    for hot in hots:
        hot_all = hot_all + jnp.where(hot, 1.0, 0.0)
    colsum = jnp.sum(hot_all, axis=0, keepdims=True)

    @pl.when(ph == 0)
    def _():
        cnt_scr[...] = cnt_scr[...] + colsum

    @pl.when((ph == 1) & (t == 0))
    def _():
        cnt = cnt_scr[...].astype(I32)
        pc = ((cnt + (BLK - 1)) & (-BLK)).astype(F32)
        r = lax.broadcasted_iota(I32, (LANES, LANES), 0)
        c = lax.broadcasted_iota(I32, (LANES, LANES), 1)
        upper = jnp.where(r < c, 1.0, 0.0)
        pstart = jnp.dot(jnp.broadcast_to(pc, (8, LANES)), upper, precision=HIGHEST,
                         preferred_element_type=F32)[0:1, :]
        pst_scr[...] = pstart
        run_scr[...] = jnp.zeros(run_scr.shape, F32)
        pend = pstart + pc
        meta_ref[0:1, :] = cnt_scr[...].astype(I32)
        meta_ref[1:2, :] = pstart.astype(I32)
        meta_ref[2:3, :] = pend.astype(I32)
        meta_ref[3:8, :] = jnp.zeros((5, LANES), I32)
        jstart = (lax.broadcasted_iota(I32, (NBP, LANES), 0) * BLK).astype(F32)
        elane = lax.broadcasted_iota(I32, (NBP, LANES), 1)
        owned = jnp.where((pend <= jstart) & (elane < E), 1.0, 0.0)
        be = jnp.minimum(jnp.sum(owned, axis=1, keepdims=True), float(E - 1))
        blk_ref[...] = jnp.broadcast_to(be, (NBP, LANES)).astype(I32)

    @pl.when(ph == 1)
    def _():
        earlier = _dot(tri_scr[...], hot_all.astype(BF16))
        pos = earlier + (pst_scr[...] + run_scr[...])
        e0 = jnp.ones_like(vals[0])
        es = [e0] + [jnp.exp(v - vals[0]) for v in vals[1:]]
        tot = es[0]
        for e in es[1:]:
            tot = tot + e
        for kk in range(K):
            d = jnp.sum(jnp.where(hots[kk], pos, 0.0), axis=1, keepdims=True)
            dest_ref[:, kk:kk + 1] = d.astype(I32)
            gate_ref[:, kk:kk + 1] = es[kk] / tot
        run_scr[...] = run_scr[...] + colsum


def _route(logits, n_blocks):
    n = logits.shape[0]
    T = min(ROUTE_TILE, n)
    nbp = -(-n_blocks // 8) * 8
    kern = functools.partial(_route_kernel, T=T, E=N_EXPERTS, K=TOP_K, BLK=MOE_SLOT_BLOCK, NBP=nbp)
    tok_out = lambda ph, t: (t * ph, 0)
    fixed = lambda ph, t: (0, 0)
    return pl.pallas_call(
        kern,
        grid=(2, n // T),
        in_specs=[pl.BlockSpec((T, LANES), lambda ph, t: (t, 0))],
        out_specs=[pl.BlockSpec((T, TOP_K), tok_out), pl.BlockSpec((T, TOP_K), tok_out),
                   pl.BlockSpec((8, LANES), fixed), pl.BlockSpec((nbp, LANES), fixed)],
        out_shape=[jax.ShapeDtypeStruct((n, TOP_K), I32), jax.ShapeDtypeStruct((n, TOP_K), F32),
                   jax.ShapeDtypeStruct((8, LANES), I32), jax.ShapeDtypeStruct((nbp, LANES), I32)],
        scratch_shapes=[pltpu.VMEM((T, T), BF16), pltpu.VMEM((1, LANES), F32), pltpu.VMEM((1, LANES), F32),
                        pltpu.VMEM((1, LANES), F32)],
        compiler_params=_cparams(("arbitrary", "arbitrary")),
        name="moe_route",
    )(logits)


def _dispatch_kernel(meta_ref, dest_ref, h_ref, *rest, T, K, E, BLK, NB):
    if len(rest) == 4:
        xs_ref, zero_scr, sem, zsem = rest
    else:
        gu_ref, dn_ref, xs_ref, ogu_ref, odn_ref, zero_scr, sem, zsem = rest
        ogu_ref[...] = gu_ref[...].astype(BF16)
        odn_ref[...] = dn_ref[...].astype(BF16)
    i = pl.program_id(0)
    last = pl.num_programs(0) - 1

    def row_copy(r, kk):
        d = dest_ref[r * K + kk]
        return pltpu.make_async_copy(h_ref.at[pl.ds(r, 1)], xs_ref.at[pl.ds(d, 1)], sem)

    def issue(r, carry):
        for kk in range(K):
            row_copy(r, kk).start(priority=kk % 2)
        return carry

    lax.fori_loop(0, T, issue, 0, unroll=DMA_UNROLL)

    @pl.when(i == last)
    def _():
        zero_scr[...] = jnp.zeros(zero_scr.shape, zero_scr.dtype)

        def pad_copy(slot):
            return pltpu.make_async_copy(zero_scr.at[pl.ds(0, 1)], xs_ref.at[pl.ds(slot, 1)], zsem)

        for e in range(E):
            cnt = meta_ref[0, e]
            first = meta_ref[1, e] + cnt
            npad = meta_ref[2, e] - first

            def zissue(r, carry, first=first):
                pad_copy(first + r).start()
                return carry

            def zwait(r, carry, first=first):
                pad_copy(first + r).wait()
                return carry

            lax.fori_loop(0, npad, zissue, 0)
            lax.fori_loop(0, npad, zwait, 0)

        used = meta_ref[2, E - 1] // BLK

        def tail_copy(j):
            return pltpu.make_async_copy(zero_scr, xs_ref.at[pl.ds(pl.multiple_of(j * BLK, BLK), BLK)], zsem)

        def tissue(j, carry):
            tail_copy(j).start()
            return carry

        def twait(j, carry):
            tail_copy(j).wait()
            return carry

        lax.fori_loop(used, NB, tissue, 0)
        lax.fori_loop(used, NB, twait, 0)

    def drain(r, carry):
        for kk in range(K):
            row_copy(r, kk).wait()
        return carry

    lax.fori_loop(0, T, drain, 0, unroll=DMA_UNROLL)


def _cast_chunk_specs(w_all, layer, chunks, step_of):
    rows = w_all.shape[1] * w_all.shape[2]
    cols = w_all.shape[3]
    chunk = rows // chunks
    first = layer * chunks
    in_spec = pl.BlockSpec((chunk, cols), lambda *a: (first + jnp.minimum(step_of(*a), chunks - 1), 0))
    out_spec = pl.BlockSpec((chunk, cols), lambda *a: (jnp.minimum(step_of(*a), chunks - 1), 0))
    return (in_spec, out_spec, jax.ShapeDtypeStruct((rows, cols), BF16),
            w_all.reshape(w_all.shape[0] * rows, cols))


def _dispatch(h2, dest_flat, meta, n_blocks, cast=None):
    n, d = h2.shape
    T = MOE_ROW_TILE
    blk = MOE_SLOT_BLOCK
    steps = n // T
    kern = functools.partial(_dispatch_kernel, T=T, K=TOP_K, E=N_EXPERTS, BLK=blk, NB=n_blocks)
    in_specs = [pl.BlockSpec((T * TOP_K,), lambda i, m: (i,), memory_space=pltpu.SMEM),
                pl.BlockSpec((T, d), lambda i, m: (i, 0))]
    out_specs = [pl.BlockSpec(memory_space=pl.ANY)]
    out_shape = [jax.ShapeDtypeStruct((n_blocks * blk, d), h2.dtype)]
    operands = [meta, dest_flat, h2]
    if cast is not None:
        chunks = 1 << (steps.bit_length() - 1)
        for w_all in cast[:2]:
            i_spec, o_spec, o_shape, flat = _cast_chunk_specs(w_all, cast[2], chunks, lambda i, m: i)
            in_specs.append(i_spec)
            out_specs.append(o_spec)
            out_shape.append(o_shape)
            operands.append(flat)
    gs = pltpu.PrefetchScalarGridSpec(
        num_scalar_prefetch=1,
        grid=(steps,),
        in_specs=in_specs,
        out_specs=out_specs,
        scratch_shapes=[pltpu.VMEM((blk, d), h2.dtype), pltpu.SemaphoreType.DMA(()),
                        pltpu.SemaphoreType.DMA(())],
    )
    outs = pl.pallas_call(
        kern,
        grid_spec=gs,
        out_shape=out_shape,
        compiler_params=_cparams(("arbitrary",), VMEM_LIMIT),
        name="moe_dispatch",
    )(*operands)
    if cast is None:
        return outs[0], None
    return outs[0], (outs[1].reshape(cast[0].shape[1:]), outs[2].reshape(cast[1].shape[1:]))


def _expert_kernel(be_ref, used_ref, xs_ref, wgu_ref, bgu_ref, wd_ref, bd_ref, *rest):
    j = pl.program_id(0)
    if len(rest) == 1:
        (ys_ref,) = rest
    else:
        ngu_ref, ndn_ref, ys_ref, ogu_ref, odn_ref = rest
        ogu_ref[...] = ngu_ref[...].astype(BF16)
        odn_ref[...] = ndn_ref[...].astype(BF16)

    @pl.when(j < used_ref[0])
    def _():
        x_lo, x_hi = _unpack_rows(xs_ref[...])
        x = jnp.concatenate([x_lo.astype(BF16), x_hi.astype(BF16)], axis=1)
        gu = _dot(x, wgu_ref[0]) + bgu_ref[0]
        de = gu.shape[1] // 2
        a = jnp.minimum(gu[:, :de], SWIGLU_LIMIT)
        u = jnp.clip(gu[:, de:], -SWIGLU_LIMIT, SWIGLU_LIMIT)
        act = (u + 1.0) * (a / (1.0 + jnp.exp(-SWIGLU_ALPHA * a)))
        ys_ref[...] = _pack_rows(_dot(act.astype(BF16), wd_ref[0]) + bd_ref[0])

    @pl.when(j >= used_ref[0])
    def _():
        ys_ref[...] = jnp.zeros(ys_ref.shape, ys_ref.dtype)


def _experts(xs, block_e, used, w_gu, b_gu, w_down, b_down, cast_next=None):
    p, dw = xs.shape
    d = 2 * dw
    blk = MOE_SLOT_BLOCK
    ne, _, de2 = w_gu.shape
    nb = p // blk
    last_used = lambda j, be, used: jnp.minimum(j, jnp.maximum(used[0] - 1, 0))
    in_specs = [pl.BlockSpec((blk, dw), lambda j, be, used: (last_used(j, be, used), 0)),
                pl.BlockSpec((1, d, de2), lambda j, be, used: (be[j], 0, 0)),
                pl.BlockSpec((1, 1, de2), lambda j, be, used: (be[j], 0, 0)),
                pl.BlockSpec((1, de2 // 2, d), lambda j, be, used: (be[j], 0, 0)),
                pl.BlockSpec((1, 1, d), lambda j, be, used: (be[j], 0, 0))]
    out_specs = [pl.BlockSpec((blk, dw), lambda j, be, used: (j, 0))]
    out_shape = [jax.ShapeDtypeStruct((p, dw), U32)]
    operands = [block_e, used, xs, w_gu, b_gu.reshape(ne, 1, de2), w_down, b_down.reshape(ne, 1, d)]
    if cast_next is not None:
        chunks = 1 << (min(nb, CAST_CHUNKS).bit_length() - 1)
        for w_all in cast_next[:2]:
            i_spec, o_spec, o_shape, flat = _cast_chunk_specs(w_all, cast_next[2], chunks,
                                                              lambda j, be, used: j)
            in_specs.append(i_spec)
            out_specs.append(o_spec)
            out_shape.append(o_shape)
            operands.append(flat)
    gs = pltpu.PrefetchScalarGridSpec(num_scalar_prefetch=2, grid=(nb,), in_specs=in_specs, out_specs=out_specs)
    outs = pl.pallas_call(
        _expert_kernel,
        grid_spec=gs,
        out_shape=out_shape,
        compiler_params=_cparams(("arbitrary",), VMEM_LIMIT),
        name="moe_experts",
    )(*operands)
    if cast_next is None:
        return outs[0], None
    gu_all, down_all, _ = cast_next
    return outs[0], (outs[1].reshape(gu_all.shape[1:]), outs[2].reshape(down_all.shape[1:]))


def _combine_kernel(dest_ref, dest_next_ref, ys_ref, gate_ref, x_ref, g2_ref, lg_ref, lb_ref, o_ref, buf, sems,
                    *, T, K):
    i = pl.program_id(0)
    n = pl.num_programs(0)
    slot = i % 2

    def row_copy(idx_ref, dst_slot, r, kk):
        d = idx_ref[r * K + kk]
        return pltpu.make_async_copy(ys_ref.at[pl.ds(d, 1)], buf.at[dst_slot, kk, pl.ds(r, 1)],
                                     sems.at[dst_slot])

    def issue_tile(idx_ref, dst_slot):
        def issue(r, carry):
            for kk in range(K):
                row_copy(idx_ref, dst_slot, r, kk).start(priority=kk % 2)
            return carry
        lax.fori_loop(0, T, issue, 0, unroll=DMA_UNROLL)

    @pl.when(i == 0)
    def _():
        issue_tile(dest_ref, slot)

    @pl.when(i + 1 < n)
    def _():
        issue_tile(dest_next_ref, 1 - slot)

    def drain(r, carry):
        for kk in range(K):
            row_copy(dest_ref, slot, r, kk).wait()
        return carry

    lax.fori_loop(0, T, drain, 0, unroll=DMA_UNROLL)
    g = gate_ref[...]
    y_lo = y_hi = None
    for kk in range(K):
        lo, hi = _unpack_rows(buf[slot, kk])
        gk = g[:, kk:kk + 1]
        y_lo = gk * lo if y_lo is None else y_lo + gk * lo
        y_hi = gk * hi if y_hi is None else y_hi + gk * hi
    y = jnp.concatenate([y_lo, y_hi], axis=1)
    z = ALPHA * x_ref[...] + (1.0 + g2_ref[0]) * y
    o_ref[...] = _layer_norm(z, lg_ref[...], lb_ref[...])


def _combine(ys, dest_flat, gates, x1, g2, ln_g, ln_b, seq):
    n, d = x1.shape
    T = MOE_ROW_TILE
    tiles_per_seq = seq // T
    row = lambda i: (i, 0)
    last = n // T - 1
    return pl.pallas_call(
        functools.partial(_combine_kernel, T=T, K=TOP_K),
        grid=(n // T,),
        in_specs=[pl.BlockSpec((T * TOP_K,), lambda i: (i,), memory_space=pltpu.SMEM),
                  pl.BlockSpec((T * TOP_K,), lambda i: (jnp.minimum(i + 1, last),), memory_space=pltpu.SMEM),
                  pl.BlockSpec(memory_space=pl.ANY),
                  pl.BlockSpec((T, TOP_K), row), pl.BlockSpec((T, d), row),
                  pl.BlockSpec((1, 1, d), lambda i: (i // tiles_per_seq, 0, 0)),
                  _const_spec((1, d)), _const_spec((1, d))],
        out_specs=pl.BlockSpec((T, d), row),
        out_shape=jax.ShapeDtypeStruct((n, d), F32),
        scratch_shapes=[pltpu.VMEM((2, TOP_K, T, ys.shape[1]), ys.dtype), pltpu.SemaphoreType.DMA((2,))],
        compiler_params=_cparams(("arbitrary",), VMEM_LIMIT),
        name="moe_combine",
    )(dest_flat, dest_flat, ys, gates, x1, g2, ln_g.reshape(1, d), ln_b.reshape(1, d))


def _moe(h2, logits, x1, g2, ln_g, ln_b, layer, weights, w_gu_f32, b_gu, w_down_f32, b_down, seq):
    n = h2.shape[0]
    blk = MOE_SLOT_BLOCK
    n_blocks = -(-(n * TOP_K + N_EXPERTS * (blk - 1)) // blk)
    dest, gates, meta, block_e = _route(logits, n_blocks)
    dest_flat = dest.reshape(n * TOP_K)
    cast_here = (w_gu_f32, w_down_f32, layer) if weights is None else None
    xs, cast_weights = _dispatch(h2, dest_flat, meta, n_blocks, cast_here)
    w_gu, w_down = cast_weights if weights is None else weights
    used = (meta[2, N_EXPERTS - 1] // blk).reshape(1)
    cast_next = (w_gu_f32, w_down_f32, layer + 1) if layer + 1 < w_gu_f32.shape[0] else None
    ys, next_weights = _experts(xs, block_e[:n_blocks, 0], used, w_gu, b_gu[layer], w_down, b_down[layer],
                                cast_next)
    return _combine(ys, dest_flat, gates, x1, g2, ln_g, ln_b, seq), next_weights


def kernel(x, c, positions, ada_w, ada_b, ln_g, ln_b, mla_w_in, mla_q_norm, mla_w_uq, mla_kv_norm, mla_w_ukv,
           mla_w_out, ml_w_in, ml_b_gates, ml_head_norm, ml_w_out, moe_w_router, moe_b_router, moe_w_gu,
           moe_b_gu, moe_w_down, moe_b_down):
    batch, seq, d = x.shape
    n = batch * seq
    mods = _mods(c, ada_w, ada_b)
    xf = x.reshape(n, d)
    expert_w = None
    for i in range(DEPTH):
        j = i // 2
        sh1, sc1, g1, sh2, sc2, g2 = [mods[i, :, s * d:(s + 1) * d].reshape(batch, 1, d) for s in range(6)]
        if i % 2 == 0:
            cos, sin = _rope_tables(positions)
            q, k, v = _mla_front(xf, sc1, sh1, cos, sin, mla_w_in[j], mla_q_norm[j], mla_w_uq[j],
                                 mla_kv_norm[j], mla_w_ukv[j], batch, seq)
            a = _flash_attention(q, k, v).reshape(n, MLA_HEADS * MLA_V)
            w_out = mla_w_out[j]
        else:
            proj, gcol, grow = _ml_inproj(xf, sc1, sh1, ml_w_in[j], ml_b_gates[j], seq)
            a = _mlstm_scan(proj, gcol, grow, ml_head_norm[j], batch, seq)
            w_out = ml_w_out[j]
        x1, h2, logits = _outproj_ln_router(a, xf, w_out, g1, ln_g[i, 0], ln_b[i, 0], sc2, sh2,
                                            moe_w_router[i], moe_b_router[i], seq)
        xf, expert_w = _moe(h2, logits, x1, g2, ln_g[i, 1], ln_b[i, 1], i, expert_w, moe_w_gu, moe_b_gu,
                            moe_w_down, moe_b_down, seq)
    return xf.reshape(batch, seq, d)
```

```python
import functools
import math

import jax
import jax.numpy as jnp
from jax import lax
from jax.experimental import pallas as pl
from jax.experimental.pallas import tpu as pltpu

F32 = jnp.float32
BF16 = jnp.bfloat16
I32 = jnp.int32
U32 = jnp.uint32
HIGHEST = lax.Precision.HIGHEST
HIGH_HALF = 0xFFFF0000

CHUNK = 64
MLA_HEADS = 16
MLA_NOPE = 128
MLA_ROPE = 64
MLA_V = 128
MLA_Q_RANK = 448
MLA_KV_RANK = 512
ROPE_THETA = 10000.0
ML_HEADS = 4
ML_QK = 256
ML_V = 512
GATE_SOFTCAP = 15.0
N_EXPERTS = 32
TOP_K = 4
D_EXPERT = 1024
SWIGLU_LIMIT = 7.0
SWIGLU_ALPHA = 1.702
DEPTH = 2
ALPHA = (2 * DEPTH) ** 0.25
EPS = 1e-6
NEG_BIG = -1e30

LANES = 128
VMEM_LIMIT = 56 * 1024 * 1024

Q_RANK_PAD = 512
ATT_BLOCK = 512
ATT_KBLOCK = 1024
ATT_CHAINS = 8
ML_CHUNK = 256
ML_HEADS_PER_STEP = 4
MOE_SLOT_BLOCK = 512
CAST_CHUNKS = 256
ROUTE_TILE = 1024
ROW_TILE = 256
MOE_ROW_TILE = 512
OUTPROJ_TILE = 512
DMA_UNROLL = 8


def _cparams(sem, vmem=None):
    return pltpu.CompilerParams(dimension_semantics=sem, vmem_limit_bytes=vmem)


def _const_spec(shape):
    nd = len(shape)
    return pl.BlockSpec(shape, lambda *_: (0,) * nd, pipeline_mode=pl.Buffered(1))


def _split_bf16(x):
    hi = x.astype(BF16)
    lo = (x - hi.astype(F32)).astype(BF16)
    return hi, lo


def _dot(a, b):
    return jnp.dot(a, b, preferred_element_type=F32)


def _dot_nt(a, b):
    return lax.dot_general(a, b, (((1,), (1,)), ((), ())), preferred_element_type=F32)


def _dot_split(x_hi, x_lo, w_hilo):
    p = _dot(x_hi, w_hilo)
    return p[:, :LANES] + (p[:, LANES:] + _dot(x_lo, w_hilo[:, :LANES]))


def _pack_rows(x):
    w = x.shape[1] // 2
    lo = pltpu.bitcast(x[:, :w].astype(BF16).astype(F32), U32)
    hi = pltpu.bitcast(x[:, w:].astype(BF16).astype(F32), U32)
    return (hi & jnp.uint32(HIGH_HALF)) | lax.shift_right_logical(lo, jnp.uint32(16))


def _unpack_rows(words):
    lo = pltpu.bitcast(lax.shift_left(words, jnp.uint32(16)), F32)
    hi = pltpu.bitcast(words & jnp.uint32(HIGH_HALF), F32)
    return lo, hi


def _layer_norm(z, g, b):
    mu = jnp.mean(z, axis=-1, keepdims=True)
    zc = z - mu
    var = jnp.mean(zc * zc, axis=-1, keepdims=True)
    return zc * lax.rsqrt(var + EPS) * g + b


def _mods_kernel(c_ref, w_ref, b_ref, o_ref):
    c = c_ref[...]
    ca = c / (1.0 + jnp.exp(-c))
    o_ref[0] = jnp.dot(ca, w_ref[0], precision=HIGHEST, preferred_element_type=F32) + b_ref[0]


def _mods(c, ada_w, ada_b):
    depth, d, d6 = ada_w.shape
    b = c.shape[0]
    rows = 8
    c8 = jnp.pad(c, ((0, rows - b), (0, 0)))
    tn = 1024
    out = pl.pallas_call(
        _mods_kernel,
        grid=(depth, d6 // tn),
        in_specs=[pl.BlockSpec((rows, d), lambda i, j: (0, 0)),
                  pl.BlockSpec((1, d, tn), lambda i, j: (i, 0, j)),
                  pl.BlockSpec((1, 1, tn), lambda i, j: (i, 0, j))],
        out_specs=pl.BlockSpec((1, rows, tn), lambda i, j: (i, 0, j)),
        out_shape=jax.ShapeDtypeStruct((depth, rows, d6), F32),
        compiler_params=_cparams(("arbitrary", "arbitrary"), VMEM_LIMIT),
        name="adaln_mods",
    )(c8, ada_w, ada_b.reshape(depth, 1, d6))
    return out[:, :b]


def _rope_kernel(pos_ref, inv_ref, cos_ref, sin_ref):
    ang = pos_ref[...].astype(F32) * inv_ref[...]
    cos_ref[...] = jnp.cos(ang)
    sin_ref[...] = jnp.sin(ang)


def _rope_tables(positions):
    n = positions.size
    half = MLA_ROPE // 2
    inv = ROPE_THETA ** (-jnp.arange(0, MLA_ROPE, 2, dtype=F32) / MLA_ROPE)
    inv_row = jnp.tile(inv, LANES // half).reshape(1, LANES)
    pos = jnp.broadcast_to(positions.reshape(n, 1), (n, LANES))
    tm = 1024
    spec = pl.BlockSpec((tm, LANES), lambda i: (i, 0))
    return pl.pallas_call(
        _rope_kernel,
        grid=(n // tm,),
        in_specs=[spec, pl.BlockSpec((1, LANES), lambda i: (0, 0))],
        out_specs=[spec, spec],
        out_shape=[jax.ShapeDtypeStruct((n, LANES), F32)] * 2,
        compiler_params=_cparams(("arbitrary",)),
        name="rope_tables",
    )(pos, inv_row)


def _mla_front_kernel(x_ref, sc_ref, sh_ref, cos_ref, sin_ref, win_ref, qn_ref, kvn_ref, wq_ref, wkv_ref,
                      q_ref, k_ref, v_ref, *, qscale):
    h = (x_ref[...] * (1.0 + sc_ref[0]) + sh_ref[0]).astype(BF16)
    proj = _dot(h, win_ref[...])
    cq = proj[:, :Q_RANK_PAD]
    ckv = proj[:, Q_RANK_PAD:Q_RANK_PAD + MLA_KV_RANK]
    cq = cq * lax.rsqrt(jnp.sum(cq * cq, -1, keepdims=True) * (1.0 / MLA_Q_RANK) + EPS) * qn_ref[...]
    ckv = ckv * lax.rsqrt(jnp.mean(ckv * ckv, -1, keepdims=True) + EPS) * kvn_ref[...]
    cq = cq.astype(BF16)
    ckv = ckv.astype(BF16)
    cos = cos_ref[...]
    sin = sin_ref[...]
    o = Q_RANK_PAD + MLA_KV_RANK
    kr = proj[:, o:o + LANES] * cos + proj[:, o + LANES:o + 2 * LANES] * sin
    lane = lax.broadcasted_iota(I32, kr.shape, 1)
    kr_even = jnp.where(lane < MLA_ROPE, kr, 0.0).astype(BF16)
    kr_odd = jnp.where(lane >= MLA_ROPE, kr, 0.0).astype(BF16)
    ones_col = jnp.where(lane == 0, 1.0, 0.0).astype(BF16)

    nheads = MLA_HEADS
    nope_w = nheads * MLA_NOPE
    rope_w = nheads * MLA_ROPE
    group = 4
    for g in range(nheads // group):
        qn = _dot(cq, wq_ref[:, g * group * MLA_NOPE:(g + 1) * group * MLA_NOPE]) * qscale
        kv = _dot(ckv, wkv_ref[:, g * group * 2 * LANES:(g + 1) * group * 2 * LANES])
        for j in range(group):
            hh = g * group + j
            q_ref[0, hh, :, :LANES] = qn[:, j * LANES:(j + 1) * LANES].astype(BF16)
            k_ref[0, hh, :, :LANES] = kv[:, 2 * j * LANES:(2 * j + 1) * LANES].astype(BF16)
            v_ref[0, hh, :, :LANES] = kv[:, (2 * j + 1) * LANES:(2 * j + 2) * LANES].astype(BF16)
            k_ref[0, hh, :, LANES:] = kr_even if hh % 2 == 0 else kr_odd
            v_ref[0, hh, :, LANES:] = ones_col
    cos2 = jnp.concatenate([cos, cos], axis=1)
    sin2 = jnp.concatenate([sin, sin], axis=1)
    for p2 in range(nheads // 4):
        lo = nope_w + p2 * 2 * LANES
        qr = (_dot(cq, wq_ref[:, lo:lo + 2 * LANES]) * cos2
              + _dot(cq, wq_ref[:, rope_w + lo:rope_w + lo + 2 * LANES]) * sin2) * qscale
        qr = qr.astype(BF16)
        for half_slab in range(2):
            pair = 2 * p2 + half_slab
            slab = qr[:, half_slab * LANES:(half_slab + 1) * LANES]
            q_ref[0, 2 * pair, :, LANES:] = slab
            q_ref[0, 2 * pair + 1, :, LANES:] = slab


def _mla_front(x2d, sc, sh, cos, sin, w_in, q_norm, w_uq, kv_norm, w_ukv, batch, seq):
    n, d = x2d.shape
    nh = MLA_HEADS
    half = MLA_ROPE // 2
    wq_lat = jnp.pad(w_in[:, :MLA_Q_RANK], ((0, 0), (0, Q_RANK_PAD - MLA_Q_RANK)))
    wkv_lat = w_in[:, MLA_Q_RANK:MLA_Q_RANK + MLA_KV_RANK]
    wkr = w_in[:, MLA_Q_RANK + MLA_KV_RANK:]
    wkr_rot = jnp.concatenate([-wkr[:, half:], wkr[:, :half]], 1)
    win = jnp.concatenate([wq_lat, wkv_lat, wkr, wkr, wkr_rot, wkr_rot], 1).astype(BF16)
    wq3 = w_uq.reshape(MLA_Q_RANK, nh, MLA_NOPE + MLA_ROPE)
    wq_nope = wq3[:, :, :MLA_NOPE].reshape(MLA_Q_RANK, nh * MLA_NOPE)
    wq_r = wq3[:, :, MLA_NOPE:]
    wq_rope = wq_r.reshape(MLA_Q_RANK, nh * MLA_ROPE)
    wq_rot = jnp.concatenate([-wq_r[:, :, half:], wq_r[:, :, :half]], -1).reshape(MLA_Q_RANK, nh * MLA_ROPE)
    wq = jnp.pad(jnp.concatenate([wq_nope, wq_rope, wq_rot], 1),
                 ((0, Q_RANK_PAD - MLA_Q_RANK), (0, 0))).astype(BF16)
    wkv = w_ukv.astype(BF16)
    qn = jnp.pad(q_norm, (0, Q_RANK_PAD - MLA_Q_RANK)).reshape(1, Q_RANK_PAD)
    kvn = kv_norm.reshape(1, MLA_KV_RANK)
    qscale = (MLA_NOPE + MLA_ROPE) ** -0.5 * math.log2(math.e)

    tm = ROW_TILE
    tiles_per_seq = seq // tm
    row = lambda i: (i, 0)
    per_b = lambda i: (i // tiles_per_seq, 0, 0)
    head_out = pl.BlockSpec((1, nh, tm, 2 * LANES), lambda i: (i // tiles_per_seq, 0, i % tiles_per_seq, 0))
    out_sds = jax.ShapeDtypeStruct((batch, nh, seq, 2 * LANES), BF16)
    return pl.pallas_call(
        functools.partial(_mla_front_kernel, qscale=qscale),
        grid=(n // tm,),
        in_specs=[pl.BlockSpec((tm, d), row),
                  pl.BlockSpec((1, 1, d), per_b), pl.BlockSpec((1, 1, d), per_b),
                  pl.BlockSpec((tm, LANES), row), pl.BlockSpec((tm, LANES), row),
                  _const_spec(win.shape), _const_spec(qn.shape), _const_spec(kvn.shape),
                  _const_spec(wq.shape), _const_spec(wkv.shape)],
        out_specs=[head_out, head_out, head_out],
        out_shape=[out_sds, out_sds, out_sds],
        compiler_params=_cparams(("arbitrary",), VMEM_LIMIT),
        name="mla_front",
    )(x2d, sc, sh, cos, sin, win, qn, kvn, wq, wkv)


def _flash_kernel(q_ref, k_ref, v_ref, o_ref, m_ref, acc_ref, p_last, a_last, *, blk, kblk, nsub):
    i = pl.program_id(2)
    m_ref[...] = jnp.full(m_ref.shape, NEG_BIG, F32)
    acc_ref[...] = jnp.zeros(acc_ref.shape, F32)
    p_last[...] = jnp.zeros(p_last.shape, BF16)
    a_last[...] = jnp.ones(a_last.shape, F32)
    slabs = kblk // LANES
    ratio = kblk // blk
    shift = CHUNK.bit_length() - 1
    everyone = range(nsub)
    last = nsub - 1

    def accumulate(c, alpha, p, kb):
        v = v_ref[0, 0, pl.ds(pl.multiple_of(kb * kblk, kblk), kblk), :]
        pv = _dot(p, v)
        acc_ref[c, :, :LANES] = alpha * acc_ref[c, :, :LANES] + pv[:, :LANES]
        acc_ref[c, :, LANES:] = alpha * acc_ref[c, :, LANES:] + pv[:, LANES:]

    def step(c, kb, masked, defer=False):
        start = pl.multiple_of(kb * kblk, kblk)
        q = q_ref[0, 0, c * blk:(c + 1) * blk, :]
        k = k_ref[0, 0, pl.ds(start, kblk), :]
        s = _dot_nt(q, k)
        if masked:
            r = lax.shift_right_logical(lax.broadcasted_iota(I32, s.shape, 0) + (c % ratio) * blk, shift)
            cc = lax.shift_right_logical(lax.broadcasted_iota(I32, s.shape, 1), shift)
            s = jnp.where(cc <= r, s, NEG_BIG)
        m_prev = m_ref[c]
        m_new = jnp.maximum(m_prev, jnp.max(s, axis=1, keepdims=True))
        p = jnp.concatenate([jnp.exp2(s[:, j * LANES:(j + 1) * LANES] - m_new).astype(BF16)
                             for j in range(slabs)], axis=1)
        alpha = jnp.exp2(m_prev - m_new)
        m_ref[c] = m_new
        if defer:
            p_last[...] = p
            a_last[...] = alpha
        else:
            accumulate(c, alpha, p, kb)

    def flush_last(kb):
        accumulate(last, a_last[...], p_last[...], jnp.maximum(kb, 0))

    def full_steps(kb, carry):
        flush_last(kb - 1)
        for c in everyone:
            step(c, kb, False, defer=(c == last))
        return carry

    first_diag = (nsub // ratio) * i
    lax.fori_loop(0, first_diag, full_steps, 0)
    flush_last(first_diag - 1)
    for t in range(nsub // ratio):
        for c in everyone:
            if c // ratio >= t:
                step(c, first_diag + t, masked=(c // ratio == t))
    for c in everyone:
        l = acc_ref[c, :, LANES:LANES + 1]
        o_ref[0, c * blk:(c + 1) * blk, :] = (acc_ref[c, :, :MLA_V] / l).astype(o_ref.dtype)


def _flash_attention(q, k, v):
    batch, nh, seq, dk = q.shape
    blk = ATT_BLOCK
    nsub = ATT_CHAINS
    kblk = ATT_KBLOCK
    assert kblk % blk == 0 and nsub % (kblk // blk) == 0
    qspec = pl.BlockSpec((1, 1, nsub * blk, dk), lambda b, h, i: (b, h, i, 0))
    kvspec = pl.BlockSpec((1, 1, seq, dk), lambda b, h, i: (b, h, 0, 0))
    return pl.pallas_call(
        functools.partial(_flash_kernel, blk=blk, kblk=kblk, nsub=nsub),
        grid=(batch, nh, seq // (nsub * blk)),
        in_specs=[qspec, kvspec, kvspec],
        out_specs=pl.BlockSpec((1, nsub * blk, MLA_V), lambda b, h, i: (b, i, h)),
        out_shape=jax.ShapeDtypeStruct((batch, seq, nh * MLA_V), BF16),
        scratch_shapes=[pltpu.VMEM((nsub, blk, LANES), F32), pltpu.VMEM((nsub, blk, dk), F32),
                        pltpu.VMEM((blk, kblk), BF16), pltpu.VMEM((blk, LANES), F32)],
        compiler_params=_cparams(("arbitrary", "arbitrary", "arbitrary"), VMEM_LIMIT),
        name="flash_attention",
    )(q, k, v)


def _outproj_kernel(a_ref, x_ref, g1_ref, lg_ref, lb_ref, sc_ref, sh_ref, w_ref, rw_ref, rb_ref,
                    x1_ref, h2_ref, logit_ref):
    y = _dot(a_ref[...], w_ref[...])
    z = ALPHA * x_ref[...] + (1.0 + g1_ref[0]) * y
    x1 = _layer_norm(z, lg_ref[...], lb_ref[...])
    x1_ref[...] = x1
    h2 = x1 * (1.0 + sc_ref[0]) + sh_ref[0]
    h2_ref[...] = _pack_rows(h2)
    h2_hi, h2_lo = _split_bf16(h2)
    logit_ref[...] = _dot_split(h2_hi, h2_lo, rw_ref[...]) + rb_ref[...]


def _outproj_ln_router(a2d, x2d, w_out, g1, ln_g, ln_b, sc2, sh2, w_router, b_router, seq):
    n, d = x2d.shape
    da = a2d.shape[1]
    ne = w_router.shape[1]
    w = w_out.astype(BF16)
    rw = jnp.pad(w_router, ((0, 0), (0, LANES - ne)))
    rw_hilo = jnp.concatenate(_split_bf16(rw), axis=1)
    rb = jnp.pad(b_router, (0, LANES - ne), constant_values=NEG_BIG).reshape(1, LANES)
    tm = OUTPROJ_TILE
    tiles_per_seq = seq // tm
    row = lambda i: (i, 0)
    per_b = lambda i: (i // tiles_per_seq, 0, 0)
    vec = pl.BlockSpec((1, 1, d), per_b)
    return pl.pallas_call(
        _outproj_kernel,
        grid=(n // tm,),
        in_specs=[pl.BlockSpec((tm, da), row), pl.BlockSpec((tm, d), row), vec,
                  _const_spec((1, d)), _const_spec((1, d)), vec, vec,
                  _const_spec(w.shape), _const_spec(rw_hilo.shape), _const_spec(rb.shape)],
        out_specs=[pl.BlockSpec((tm, d), row), pl.BlockSpec((tm, d // 2), row), pl.BlockSpec((tm, LANES), row)],
        out_shape=[jax.ShapeDtypeStruct((n, d), F32), jax.ShapeDtypeStruct((n, d // 2), U32),
                   jax.ShapeDtypeStruct((n, LANES), F32)],
        compiler_params=_cparams(("arbitrary",), VMEM_LIMIT),
        name="outproj_ln_router",
    )(a2d, x2d, g1, ln_g.reshape(1, d), ln_b.reshape(1, d), sc2, sh2, w, rw_hilo, rb)


def _gate_log(g, is_input_gate):
    g = GATE_SOFTCAP * jnp.tanh(g * (1.0 / GATE_SOFTCAP))
    log_f = jnp.minimum(g, 0.0) - jnp.log(1.0 + jnp.exp(-jnp.abs(g)))
    return jnp.where(is_input_gate, g, log_f)


def _ml_inproj_kernel(x_ref, sc_ref, sh_ref, w_ref, gw_ref, gb_ref, gwth_ref, gwthl_ref, gbt_ref,
                      proj_ref, gcol_ref, grow_ref, h_scr):
    j = pl.program_id(1)

    @pl.when(j == 0)
    def _():
        h = x_ref[...] * (1.0 + sc_ref[0]) + sh_ref[0]
        h_hi, h_lo = _split_bf16(h)
        h_scr[...] = h_hi
        nh = ML_HEADS
        g = _dot_split(h_hi, h_lo, gw_ref[...]) + gb_ref[...]
        lane = lax.broadcasted_iota(I32, g.shape, 1)
        gcol_ref[...] = _gate_log(g, lane < nh)
        ng = gbt_ref.shape[0]
        both = _dot_nt(gwthl_ref[...], h_hi)
        gt = both[:ng] + (both[ng:] + _dot_nt(gwth_ref[...], h_lo)) + gbt_ref[...]
        sub = lax.broadcasted_iota(I32, gt.shape, 0)
        grow_ref[...] = _gate_log(gt, sub < nh)

    proj_ref[...] = _dot(h_scr[...], w_ref[...]).astype(proj_ref.dtype)


def _ml_inproj(x2d, sc, sh, w_in, b_gates, seq):
    n, d = x2d.shape
    ng = 2 * ML_HEADS
    wide = w_in.shape[1] - ng
    w = w_in[:, :wide].astype(BF16)
    gw = w_in[:, wide:]
    gw_hilo = jnp.concatenate(_split_bf16(jnp.pad(gw, ((0, 0), (0, LANES - ng)))), axis=1)
    gwt_hi, gwt_lo = _split_bf16(gw.T)
    gwt_hilo = jnp.concatenate([gwt_hi, gwt_lo], axis=0)
    gb = jnp.pad(b_gates, (0, LANES - ng)).reshape(1, LANES)
    gbt = b_gates.reshape(ng, 1)
    tm, tn = 1024, 1024
    tiles_per_seq = seq // tm
    row = lambda i, j: (i, 0)
    per_b = lambda i, j: (i // tiles_per_seq, 0, 0)
    const = lambda shape: pl.BlockSpec(shape, lambda i, j: (0,) * len(shape), pipeline_mode=pl.Buffered(1))
    return pl.pallas_call(
        _ml_inproj_kernel,
        grid=(n // tm, wide // tn),
        in_specs=[pl.BlockSpec((tm, d), row),
                  pl.BlockSpec((1, 1, d), per_b), pl.BlockSpec((1, 1, d), per_b),
                  pl.BlockSpec((d, tn), lambda i, j: (0, j)),
                  const(gw_hilo.shape), const(gb.shape),
                  const(gwt_hi.shape), const(gwt_hilo.shape), const(gbt.shape)],
        out_specs=[pl.BlockSpec((tm, tn), lambda i, j: (i, j)),
                   pl.BlockSpec((tm, LANES), row),
                   pl.BlockSpec((ng, tm), lambda i, j: (0, i))],
        out_shape=[jax.ShapeDtypeStruct((n, wide), BF16), jax.ShapeDtypeStruct((n, LANES), F32),
                   jax.ShapeDtypeStruct((ng, n), F32)],
        scratch_shapes=[pltpu.VMEM((tm, d), BF16)],
        compiler_params=_cparams(("arbitrary", "arbitrary"), VMEM_LIMIT),
        name="mlstm_inproj",
    )(x2d, sc, sh, w, gw_hilo, gb, gwt_hi, gwt_hilo, gbt)


def _split3_bf16(x):
    a = x.astype(BF16)
    r = x - a.astype(F32)
    b = r.astype(BF16)
    c = (r - b.astype(F32)).astype(BF16)
    return a, b, c


def _mlstm_kernel(q_ref, k_ref, v_ref, og_ref, gc_ref, gr_ref, hn_ref, o_ref, c_scr, m_scr, *, L, G):
    ci = pl.program_id(2)

    @pl.when(ci == 0)
    def _():
        c_scr[...] = jnp.zeros(c_scr.shape, F32)
        m_scr[...] = jnp.zeros(m_scr.shape, F32)

    row = lax.broadcasted_iota(I32, (L, L), 0)
    col = lax.broadcasted_iota(I32, (L, L), 1)
    causal = col <= row
    tri = jnp.where(causal, 1.0, 0.0).astype(BF16)
    ones_col = jnp.where(lax.broadcasted_iota(I32, (L, LANES), 1) == 0, 1.0, 0.0).astype(BF16)
    for g in range(G):
        _mlstm_head(q_ref[:, g * ML_QK:(g + 1) * ML_QK], k_ref[:, g * ML_QK:(g + 1) * ML_QK],
                    v_ref[:, g * ML_V:(g + 1) * ML_V], og_ref[:, g * ML_V:(g + 1) * ML_V],
                    gc_ref[g], gr_ref[g], hn_ref[:, g * ML_V:(g + 1) * ML_V],
                    o_ref.at[:, g * ML_V:(g + 1) * ML_V], c_scr.at[g], m_scr.at[g],
                    causal, tri, ones_col, L)


def _mlstm_head(q, k, v, og, gc, gr, head_gain, o_ref, c_scr, m_scr, causal, tri, ones_col, L):
    v_ext = jnp.concatenate([v, ones_col], axis=1)
    li_c, lf_c = gc[:, 0:1], gc[:, 1:2]
    li_r, lf_r = gr[0:1, :], gr[1:2, :]

    lfc3 = _split3_bf16(jnp.broadcast_to(lf_c, (L, LANES)))
    b_c = (_dot(tri, lfc3[0]) + (_dot(tri, lfc3[1]) + _dot(tri, lfc3[2])))[:, 0:1]
    lfr3 = _split3_bf16(jnp.broadcast_to(lf_r, (8, L)))
    b_r = (_dot_nt(lfr3[0], tri) + (_dot_nt(lfr3[1], tri) + _dot_nt(lfr3[2], tri)))[0:1, :]

    m_prev = m_scr[...]
    dm = jnp.where(causal, b_c - (b_r - li_r), NEG_BIG)
    inter = b_c + m_prev
    m_t = jnp.maximum(inter, jnp.max(dm, axis=1, keepdims=True))
    w_intra = jnp.exp(dm - m_t)
    w_inter = jnp.exp(inter - m_t)

    kscale = ML_QK ** -0.5
    qk = _dot_nt(q, k) * (w_intra * kscale)
    c_state = c_scr[...]
    num_ext = w_inter * _dot(q, c_state.astype(BF16)) + _dot(qk.astype(BF16), v_ext)
    vdim = v.shape[1]
    num = num_ext[:, :vdim]
    den = num_ext[:, vdim:vdim + 1]
    h = num / jnp.maximum(jnp.abs(den), jnp.exp(-m_t))
    hn = h * lax.rsqrt(jnp.mean(h * h, axis=-1, keepdims=True) + EPS) * head_gain
    o_ref[...] = (hn / (1.0 + jnp.exp(-og.astype(F32)))).astype(o_ref.dtype)

    b_end = b_r[:, L - 1:L]
    d_end_r = b_end - b_r + li_r
    m_new = jnp.maximum(b_end + m_prev, jnp.max(d_end_r, axis=1, keepdims=True))
    decay = jnp.exp(b_end + m_prev - m_new)
    w_s = jnp.exp(b_end - b_c + li_c - m_new) * kscale
    kw = (k.astype(F32) * w_s).astype(BF16)
    upd = lax.dot_general(kw, v_ext, (((0,), (0,)), ((), ())), preferred_element_type=F32)
    c_scr[...] = decay * c_state + upd
    m_scr[...] = m_new


def _mlstm_scan(proj, gcol, grow, head_norm, batch, seq):
    n = proj.shape[0]
    nh = ML_HEADS
    L = ML_CHUNK
    nc = seq // L
    gc = gcol[:, :2 * nh].reshape(n, 2, nh).transpose(2, 0, 1)
    gr = grow.reshape(2, nh, n).transpose(1, 0, 2)
    hn = head_norm.reshape(1, nh * ML_V)
    G = ML_HEADS_PER_STEP
    groups = nh // G
    k_off = groups
    v_off = 2 * nh * ML_QK // (G * ML_V)
    og_off = v_off + groups
    tok = lambda b, h, c: b * nc + c
    return pl.pallas_call(
        functools.partial(_mlstm_kernel, L=L, G=G),
        grid=(batch, groups, nc),
        in_specs=[pl.BlockSpec((L, G * ML_QK), lambda b, h, c: (tok(b, h, c), h)),
                  pl.BlockSpec((L, G * ML_QK), lambda b, h, c: (tok(b, h, c), k_off + h)),
                  pl.BlockSpec((L, G * ML_V), lambda b, h, c: (tok(b, h, c), v_off + h)),
                  pl.BlockSpec((L, G * ML_V), lambda b, h, c: (tok(b, h, c), og_off + h)),
                  pl.BlockSpec((G, L, 2), lambda b, h, c: (h, tok(b, h, c), 0)),
                  pl.BlockSpec((G, 2, L), lambda b, h, c: (h, 0, tok(b, h, c))),
                  pl.BlockSpec((1, G * ML_V), lambda b, h, c: (0, h))],
        out_specs=pl.BlockSpec((L, G * ML_V), lambda b, h, c: (tok(b, h, c), h)),
        out_shape=jax.ShapeDtypeStruct((n, nh * ML_V), BF16),
        scratch_shapes=[pltpu.VMEM((G, ML_QK, ML_V + LANES), F32), pltpu.VMEM((G, 1, 1), F32)],
        compiler_params=_cparams(("arbitrary", "arbitrary", "arbitrary"), VMEM_LIMIT),
        name="mlstm_scan",
    )(proj, proj, proj, proj, gc, gr, hn)


def _route_kernel(lg_ref, dest_ref, gate_ref, meta_ref, blk_ref, tri_scr, cnt_scr, run_scr, pst_scr,
                  *, T, E, K, BLK, NBP):
    ph = pl.program_id(0)
    t = pl.program_id(1)
    nt = pl.num_programs(1)

    @pl.when((ph == 0) & (t == 0))
    def _():
        r = lax.broadcasted_iota(I32, (T, T), 0)
        c = lax.broadcasted_iota(I32, (T, T), 1)
        tri_scr[...] = jnp.where(c < r, 1.0, 0.0).astype(BF16)
        cnt_scr[...] = jnp.zeros(cnt_scr.shape, F32)

    lane = lax.broadcasted_iota(I32, (T, LANES), 1)
    l = jnp.where(lane < E, lg_ref[...], -jnp.inf)
    vals, hots = [], []
    for _ in range(K):
        mx = jnp.max(l, axis=1, keepdims=True)
        idx = jnp.min(jnp.where(l == mx, lane, LANES), axis=1, keepdims=True)
        hot = lane == idx
        vals.append(mx)
        hots.append(hot)
        l = jnp.where(hot, -jnp.inf, l)
    hot_all = jnp.zeros((T, LANES), F32)
    for hot in hots:
        hot_all = hot_all + jnp.where(hot, 1.0, 0.0)
    colsum = jnp.sum(hot_all, axis=0, keepdims=True)

    @pl.when(ph == 0)
    def _():
        cnt_scr[...] = cnt_scr[...] + colsum

    @pl.when((ph == 1) & (t == 0))
    def _():
        cnt = cnt_scr[...].astype(I32)
        pc = ((cnt + (BLK - 1)) & (-BLK)).astype(F32)
        r = lax.broadcasted_iota(I32, (LANES, LANES), 0)
        c = lax.broadcasted_iota(I32, (LANES, LANES), 1)
        upper = jnp.where(r < c, 1.0, 0.0)
        pstart = jnp.dot(jnp.broadcast_to(pc, (8, LANES)), upper, precision=HIGHEST,
                         preferred_element_type=F32)[0:1, :]
        pst_scr[...] = pstart
        run_scr[...] = jnp.zeros(run_scr.shape, F32)
        pend = pstart + pc
        meta_ref[0:1, :] = cnt_scr[...].astype(I32)
        meta_ref[1:2, :] = pstart.astype(I32)
        meta_ref[2:3, :] = pend.astype(I32)
        meta_ref[3:8, :] = jnp.zeros((5, LANES), I32)
        jstart = (lax.broadcasted_iota(I32, (NBP, LANES), 0) * BLK).astype(F32)
        elane = lax.broadcasted_iota(I32, (NBP, LANES), 1)
        owned = jnp.where((pend <= jstart) & (elane < E), 1.0, 0.0)
        be = jnp.minimum(jnp.sum(owned, axis=1, keepdims=True), float(E - 1))
        blk_ref[...] = jnp.broadcast_to(be, (NBP, LANES)).astype(I32)

    @pl.when(ph == 1)
    def _():
        earlier = _dot(tri_scr[...], hot_all.astype(BF16))
        pos = earlier + (pst_scr[...] + run_scr[...])
        e0 = jnp.ones_like(vals[0])
        es = [e0] + [jnp.exp(v - vals[0]) for v in vals[1:]]
        tot = es[0]
        for e in es[1:]:
            tot = tot + e
        for kk in range(K):
            d = jnp.sum(jnp.where(hots[kk], pos, 0.0), axis=1, keepdims=True)
            dest_ref[:, kk:kk + 1] = d.astype(I32)
            gate_ref[:, kk:kk + 1] = es[kk] / tot
        run_scr[...] = run_scr[...] + colsum


def _route(logits, n_blocks):
    n = logits.shape[0]
    T = min(ROUTE_TILE, n)
    nbp = -(-n_blocks // 8) * 8
    kern = functools.partial(_route_kernel, T=T, E=N_EXPERTS, K=TOP_K, BLK=MOE_SLOT_BLOCK, NBP=nbp)
    tok_out = lambda ph, t: (t * ph, 0)
    fixed = lambda ph, t: (0, 0)
    return pl.pallas_call(
        kern,
        grid=(2, n // T),
        in_specs=[pl.BlockSpec((T, LANES), lambda ph, t: (t, 0))],
        out_specs=[pl.BlockSpec((T, TOP_K), tok_out), pl.BlockSpec((T, TOP_K), tok_out),
                   pl.BlockSpec((8, LANES), fixed), pl.BlockSpec((nbp, LANES), fixed)],
        out_shape=[jax.ShapeDtypeStruct((n, TOP_K), I32), jax.ShapeDtypeStruct((n, TOP_K), F32),
                   jax.ShapeDtypeStruct((8, LANES), I32), jax.ShapeDtypeStruct((nbp, LANES), I32)],
        scratch_shapes=[pltpu.VMEM((T, T), BF16), pltpu.VMEM((1, LANES), F32), pltpu.VMEM((1, LANES), F32),
                        pltpu.VMEM((1, LANES), F32)],
        compiler_params=_cparams(("arbitrary", "arbitrary")),
        name="moe_route",
    )(logits)


def _dispatch_kernel(meta_ref, dest_ref, h_ref, *rest, T, K, E, BLK, NB):
    if len(rest) == 4:
        xs_ref, zero_scr, sem, zsem = rest
    else:
        gu_ref, dn_ref, xs_ref, ogu_ref, odn_ref, zero_scr, sem, zsem = rest
        ogu_ref[...] = gu_ref[...].astype(BF16)
        odn_ref[...] = dn_ref[...].astype(BF16)
    i = pl.program_id(0)
    last = pl.num_programs(0) - 1

    def row_copy(r, kk):
        d = dest_ref[r * K + kk]
        return pltpu.make_async_copy(h_ref.at[pl.ds(r, 1)], xs_ref.at[pl.ds(d, 1)], sem)

    def issue(r, carry):
        for kk in range(K):
            row_copy(r, kk).start(priority=kk % 2)
        return carry

    lax.fori_loop(0, T, issue, 0, unroll=DMA_UNROLL)

    @pl.when(i == last)
    def _():
        zero_scr[...] = jnp.zeros(zero_scr.shape, zero_scr.dtype)

        def pad_copy(slot):
            return pltpu.make_async_copy(zero_scr.at[pl.ds(0, 1)], xs_ref.at[pl.ds(slot, 1)], zsem)

        for e in range(E):
            cnt = meta_ref[0, e]
            first = meta_ref[1, e] + cnt
            npad = meta_ref[2, e] - first

            def zissue(r, carry, first=first):
                pad_copy(first + r).start()
                return carry

            def zwait(r, carry, first=first):
                pad_copy(first + r).wait()
                return carry

            lax.fori_loop(0, npad, zissue, 0)
            lax.fori_loop(0, npad, zwait, 0)

        used = meta_ref[2, E - 1] // BLK

        def tail_copy(j):
            return pltpu.make_async_copy(zero_scr, xs_ref.at[pl.ds(pl.multiple_of(j * BLK, BLK), BLK)], zsem)

        def tissue(j, carry):
            tail_copy(j).start()
            return carry

        def twait(j, carry):
            tail_copy(j).wait()
            return carry

        lax.fori_loop(used, NB, tissue, 0)
        lax.fori_loop(used, NB, twait, 0)

    def drain(r, carry):
        for kk in range(K):
            row_copy(r, kk).wait()
        return carry

    lax.fori_loop(0, T, drain, 0, unroll=DMA_UNROLL)


def _cast_chunk_specs(w_all, layer, chunks, step_of):
    rows = w_all.shape[1] * w_all.shape[2]
    cols = w_all.shape[3]
    chunk = rows // chunks
    first = layer * chunks
    in_spec = pl.BlockSpec((chunk, cols), lambda *a: (first + jnp.minimum(step_of(*a), chunks - 1), 0))
    out_spec = pl.BlockSpec((chunk, cols), lambda *a: (jnp.minimum(step_of(*a), chunks - 1), 0))
    return (in_spec, out_spec, jax.ShapeDtypeStruct((rows, cols), BF16),
            w_all.reshape(w_all.shape[0] * rows, cols))


def _dispatch(h2, dest_flat, meta, n_blocks, cast=None):
    n, d = h2.shape
    T = MOE_ROW_TILE
    blk = MOE_SLOT_BLOCK
    steps = n // T
    kern = functools.partial(_dispatch_kernel, T=T, K=TOP_K, E=N_EXPERTS, BLK=blk, NB=n_blocks)
    in_specs = [pl.BlockSpec((T * TOP_K,), lambda i, m: (i,), memory_space=pltpu.SMEM),
                pl.BlockSpec((T, d), lambda i, m: (i, 0))]
    out_specs = [pl.BlockSpec(memory_space=pl.ANY)]
    out_shape = [jax.ShapeDtypeStruct((n_blocks * blk, d), h2.dtype)]
    operands = [meta, dest_flat, h2]
    if cast is not None:
        chunks = 1 << (steps.bit_length() - 1)
        for w_all in cast[:2]:
            i_spec, o_spec, o_shape, flat = _cast_chunk_specs(w_all, cast[2], chunks, lambda i, m: i)
            in_specs.append(i_spec)
            out_specs.append(o_spec)
            out_shape.append(o_shape)
            operands.append(flat)
    gs = pltpu.PrefetchScalarGridSpec(
        num_scalar_prefetch=1,
        grid=(steps,),
        in_specs=in_specs,
        out_specs=out_specs,
        scratch_shapes=[pltpu.VMEM((blk, d), h2.dtype), pltpu.SemaphoreType.DMA(()),
                        pltpu.SemaphoreType.DMA(())],
    )
    outs = pl.pallas_call(
        kern,
        grid_spec=gs,
        out_shape=out_shape,
        compiler_params=_cparams(("arbitrary",), VMEM_LIMIT),
        name="moe_dispatch",
    )(*operands)
    if cast is None:
        return outs[0], None
    return outs[0], (outs[1].reshape(cast[0].shape[1:]), outs[2].reshape(cast[1].shape[1:]))


def _expert_kernel(be_ref, used_ref, xs_ref, wgu_ref, bgu_ref, wd_ref, bd_ref, *rest):
    j = pl.program_id(0)
    if len(rest) == 1:
        (ys_ref,) = rest
    else:
        ngu_ref, ndn_ref, ys_ref, ogu_ref, odn_ref = rest
        ogu_ref[...] = ngu_ref[...].astype(BF16)
        odn_ref[...] = ndn_ref[...].astype(BF16)

    @pl.when(j < used_ref[0])
    def _():
        x_lo, x_hi = _unpack_rows(xs_ref[...])
        x = jnp.concatenate([x_lo.astype(BF16), x_hi.astype(BF16)], axis=1)
        gu = _dot(x, wgu_ref[0]) + bgu_ref[0]
        de = gu.shape[1] // 2
        a = jnp.minimum(gu[:, :de], SWIGLU_LIMIT)
        u = jnp.clip(gu[:, de:], -SWIGLU_LIMIT, SWIGLU_LIMIT)
        act = (u + 1.0) * (a / (1.0 + jnp.exp(-SWIGLU_ALPHA * a)))
        ys_ref[...] = _pack_rows(_dot(act.astype(BF16), wd_ref[0]) + bd_ref[0])

    @pl.when(j >= used_ref[0])
    def _():
        ys_ref[...] = jnp.zeros(ys_ref.shape, ys_ref.dtype)


def _experts(xs, block_e, used, w_gu, b_gu, w_down, b_down, cast_next=None):
    p, dw = xs.shape
    d = 2 * dw
    blk = MOE_SLOT_BLOCK
    ne, _, de2 = w_gu.shape
    nb = p // blk
    last_used = lambda j, be, used: jnp.minimum(j, jnp.maximum(used[0] - 1, 0))
    in_specs = [pl.BlockSpec((blk, dw), lambda j, be, used: (last_used(j, be, used), 0)),
                pl.BlockSpec((1, d, de2), lambda j, be, used: (be[j], 0, 0)),
                pl.BlockSpec((1, 1, de2), lambda j, be, used: (be[j], 0, 0)),
                pl.BlockSpec((1, de2 // 2, d), lambda j, be, used: (be[j], 0, 0)),
                pl.BlockSpec((1, 1, d), lambda j, be, used: (be[j], 0, 0))]
    out_specs = [pl.BlockSpec((blk, dw), lambda j, be, used: (j, 0))]
    out_shape = [jax.ShapeDtypeStruct((p, dw), U32)]
    operands = [block_e, used, xs, w_gu, b_gu.reshape(ne, 1, de2), w_down, b_down.reshape(ne, 1, d)]
    if cast_next is not None:
        chunks = 1 << (min(nb, CAST_CHUNKS).bit_length() - 1)
        for w_all in cast_next[:2]:
            i_spec, o_spec, o_shape, flat = _cast_chunk_specs(w_all, cast_next[2], chunks,
                                                              lambda j, be, used: j)
            in_specs.append(i_spec)
            out_specs.append(o_spec)
            out_shape.append(o_shape)
            operands.append(flat)
    gs = pltpu.PrefetchScalarGridSpec(num_scalar_prefetch=2, grid=(nb,), in_specs=in_specs, out_specs=out_specs)
    outs = pl.pallas_call(
        _expert_kernel,
        grid_spec=gs,
        out_shape=out_shape,
        compiler_params=_cparams(("arbitrary",), VMEM_LIMIT),
        name="moe_experts",
    )(*operands)
    if cast_next is None:
        return outs[0], None
    gu_all, down_all, _ = cast_next
    return outs[0], (outs[1].reshape(gu_all.shape[1:]), outs[2].reshape(down_all.shape[1:]))


def _combine_kernel(dest_ref, dest_next_ref, ys_ref, gate_ref, x_ref, g2_ref, lg_ref, lb_ref, o_ref, buf, sems,
                    *, T, K):
    i = pl.program_id(0)
    n = pl.num_programs(0)
    slot = i % 2

    def row_copy(idx_ref, dst_slot, r, kk):
        d = idx_ref[r * K + kk]
        return pltpu.make_async_copy(ys_ref.at[pl.ds(d, 1)], buf.at[dst_slot, kk, pl.ds(r, 1)],
                                     sems.at[dst_slot])

    def issue_tile(idx_ref, dst_slot):
        def issue(r, carry):
            for kk in range(K):
                row_copy(idx_ref, dst_slot, r, kk).start(priority=kk % 2)
            return carry
        lax.fori_loop(0, T, issue, 0, unroll=DMA_UNROLL)

    @pl.when(i == 0)
    def _():
        issue_tile(dest_ref, slot)

    @pl.when(i + 1 < n)
    def _():
        issue_tile(dest_next_ref, 1 - slot)

    def drain(r, carry):
        for kk in range(K):
            row_copy(dest_ref, slot, r, kk).wait()
        return carry

    lax.fori_loop(0, T, drain, 0, unroll=DMA_UNROLL)
    g = gate_ref[...]
    y_lo = y_hi = None
    for kk in range(K):
        lo, hi = _unpack_rows(buf[slot, kk])
        gk = g[:, kk:kk + 1]
        y_lo = gk * lo if y_lo is None else y_lo + gk * lo
        y_hi = gk * hi if y_hi is None else y_hi + gk * hi
    y = jnp.concatenate([y_lo, y_hi], axis=1)
    z = ALPHA * x_ref[...] + (1.0 + g2_ref[0]) * y
    o_ref[...] = _layer_norm(z, lg_ref[...], lb_ref[...])


def _combine(ys, dest_flat, gates, x1, g2, ln_g, ln_b, seq):
    n, d = x1.shape
    T = MOE_ROW_TILE
    tiles_per_seq = seq // T
    row = lambda i: (i, 0)
    last = n // T - 1
    return pl.pallas_call(
        functools.partial(_combine_kernel, T=T, K=TOP_K),
        grid=(n // T,),
        in_specs=[pl.BlockSpec((T * TOP_K,), lambda i: (i,), memory_space=pltpu.SMEM),
                  pl.BlockSpec((T * TOP_K,), lambda i: (jnp.minimum(i + 1, last),), memory_space=pltpu.SMEM),
                  pl.BlockSpec(memory_space=pl.ANY),
                  pl.BlockSpec((T, TOP_K), row), pl.BlockSpec((T, d), row),
                  pl.BlockSpec((1, 1, d), lambda i: (i // tiles_per_seq, 0, 0)),
                  _const_spec((1, d)), _const_spec((1, d))],
        out_specs=pl.BlockSpec((T, d), row),
        out_shape=jax.ShapeDtypeStruct((n, d), F32),
        scratch_shapes=[pltpu.VMEM((2, TOP_K, T, ys.shape[1]), ys.dtype), pltpu.SemaphoreType.DMA((2,))],
        compiler_params=_cparams(("arbitrary",), VMEM_LIMIT),
        name="moe_combine",
    )(dest_flat, dest_flat, ys, gates, x1, g2, ln_g.reshape(1, d), ln_b.reshape(1, d))


def _moe(h2, logits, x1, g2, ln_g, ln_b, layer, weights, w_gu_f32, b_gu, w_down_f32, b_down, seq):
    n = h2.shape[0]
    blk = MOE_SLOT_BLOCK
    n_blocks = -(-(n * TOP_K + N_EXPERTS * (blk - 1)) // blk)
    dest, gates, meta, block_e = _route(logits, n_blocks)
    dest_flat = dest.reshape(n * TOP_K)
    cast_here = (w_gu_f32, w_down_f32, layer) if weights is None else None
    xs, cast_weights = _dispatch(h2, dest_flat, meta, n_blocks, cast_here)
    w_gu, w_down = cast_weights if weights is None else weights
    used = (meta[2, N_EXPERTS - 1] // blk).reshape(1)
    cast_next = (w_gu_f32, w_down_f32, layer + 1) if layer + 1 < w_gu_f32.shape[0] else None
    ys, next_weights = _experts(xs, block_e[:n_blocks, 0], used, w_gu, b_gu[layer], w_down, b_down[layer],
                                cast_next)
    return _combine(ys, dest_flat, gates, x1, g2, ln_g, ln_b, seq), next_weights


def kernel(x, c, positions, ada_w, ada_b, ln_g, ln_b, mla_w_in, mla_q_norm, mla_w_uq, mla_kv_norm, mla_w_ukv,
           mla_w_out, ml_w_in, ml_b_gates, ml_head_norm, ml_w_out, moe_w_router, moe_b_router, moe_w_gu,
           moe_b_gu, moe_w_down, moe_b_down):
    batch, seq, d = x.shape
    n = batch * seq
    mods = _mods(c, ada_w, ada_b)
    xf = x.reshape(n, d)
    expert_w = None
    for i in range(DEPTH):
        j = i // 2
        sh1, sc1, g1, sh2, sc2, g2 = [mods[i, :, s * d:(s + 1) * d].reshape(batch, 1, d) for s in range(6)]
        if i % 2 == 0:
            cos, sin = _rope_tables(positions)
            q, k, v = _mla_front(xf, sc1, sh1, cos, sin, mla_w_in[j], mla_q_norm[j], mla_w_uq[j],
                                 mla_kv_norm[j], mla_w_ukv[j], batch, seq)
            a = _flash_attention(q, k, v).reshape(n, MLA_HEADS * MLA_V)
            w_out = mla_w_out[j]
        else:
            proj, gcol, grow = _ml_inproj(xf, sc1, sh1, ml_w_in[j], ml_b_gates[j], seq)
            a = _mlstm_scan(proj, gcol, grow, ml_head_norm[j], batch, seq)
            w_out = ml_w_out[j]
        x1, h2, logits = _outproj_ln_router(a, xf, w_out, g1, ln_g[i, 0], ln_b[i, 0], sc2, sh2,
                                            moe_w_router[i], moe_b_router[i], seq)
        xf, expert_w = _moe(h2, logits, x1, g2, ln_g[i, 1], ln_b[i, 1], i, expert_w, moe_w_gu, moe_b_gu,
                            moe_w_down, moe_b_down, seq)
    return xf.reshape(batch, seq, d)
```
